```python
import math
import jax, jax.numpy as jnp
from jax import lax
import numpy as np

D_MODEL = 2048
BATCH = 8
SEQ = 2048
DEPTH = 1

RNN_WIDTH = D_MODEL
RNN_BLOCKS = 16
RNN_BLOCK = RNN_WIDTH // RNN_BLOCKS
CONV_WIDTH = 4
RG_C = 8.0
HEAD_DIM = 128
ATTN_HEADS = D_MODEL // HEAD_DIM
KV_HEADS = 4
GROUPS = ATTN_HEADS // KV_HEADS
Q_WIDTH = ATTN_HEADS * HEAD_DIM
KV_WIDTH = KV_HEADS * HEAD_DIM
IDX_HEADS = 16
IDX_DIM = 64
IDX_Q_WIDTH = IDX_HEADS * IDX_DIM
TOPK_MAX = 256
Q_BLOCK = 128
NUM_BUCKETS = 32
MAX_DISTANCE = 128
D_FF = 5504
RMS_EPS = 1e-6

IN_SPLITS = (RNN_WIDTH, RNN_WIDTH, Q_WIDTH, KV_WIDTH, KV_WIDTH, IDX_Q_WIDTH, IDX_DIM, IDX_HEADS, D_MODEL, D_MODEL)
N_IN = 2 * RNN_WIDTH + Q_WIDTH + 2 * KV_WIDTH + IDX_Q_WIDTH + IDX_DIM + IDX_HEADS + 2 * D_MODEL

kernel_name = "hybrid_rglru_dsa_macaron"


def rms_norm(x, g):
    xf = x.astype(jnp.float32)
    y = xf * lax.rsqrt(jnp.mean(xf * xf, axis=-1, keepdims=True) + RMS_EPS)
    return (y * g.astype(jnp.float32)).astype(x.dtype)


def swiglu(x, w_gate, w_up, w_down):
    return (jax.nn.silu(x @ w_gate) * (x @ w_up)) @ w_down


def causal_depthwise_conv(x, w, b):
    s = x.shape[1]
    xp = jnp.pad(x, ((0, 0), (CONV_WIDTH - 1, 0), (0, 0)))
    y = b
    for k in range(CONV_WIDTH):
        y = y + xp[:, k:k + s] * w[k]
    return y


def rg_lru(x, w_a, b_a, w_x, b_x, lam):
    bsz, s, w = x.shape
    xb = x.reshape(bsz, s, RNN_BLOCKS, RNN_BLOCK)
    r = jax.nn.sigmoid(jnp.einsum('bsnc,ncd->bsnd', xb, w_a).reshape(bsz, s, w) + b_a).astype(jnp.float32)
    i = jax.nn.sigmoid(jnp.einsum('bsnc,ncd->bsnd', xb, w_x).reshape(bsz, s, w) + b_x).astype(jnp.float32)
    log_a = -RG_C * r * jax.nn.softplus(-lam.astype(jnp.float32))
    a = jnp.exp(log_a)
    mult = jnp.sqrt(-jnp.expm1(2.0 * log_a))
    u = mult * (i * x.astype(jnp.float32))

    def combine(left, right):
        a_l, b_l = left
        a_r, b_r = right
        return a_l * a_r, a_r * b_l + b_r

    _, h = lax.associative_scan(combine, (a, u), axis=1)
    return h.astype(x.dtype)


def t5_causal_bucket(rel):
    n = jnp.maximum(rel, 0)
    max_exact = NUM_BUCKETS // 2
    nf = jnp.maximum(n, 1).astype(jnp.float32)
    large = max_exact + (jnp.log(nf / max_exact) / math.log(MAX_DISTANCE / max_exact)
                         * (NUM_BUCKETS - max_exact)).astype(jnp.int32)
    large = jnp.minimum(large, NUM_BUCKETS - 1)
    return jnp.where(n < max_exact, n, large)


def dsa_attention(q, k, v, iq, ik, iw, rel_bias):
    bsz, s, _ = q.shape
    k_top = min(TOPK_MAX, s // 4)
    nb = s // Q_BLOCK
    k = k.reshape(bsz, s, KV_HEADS, HEAD_DIM)
    v = v.reshape(bsz, s, KV_HEADS, HEAD_DIM)
    q_blk = jnp.moveaxis(q.reshape(bsz, nb, Q_BLOCK, KV_HEADS, GROUPS, HEAD_DIM), 1, 0)
    iq_blk = jnp.moveaxis(iq.reshape(bsz, nb, Q_BLOCK, IDX_HEADS, IDX_DIM), 1, 0)
    iw_blk = jnp.moveaxis((iw * (IDX_HEADS ** -0.5)).reshape(bsz, nb, Q_BLOCK, IDX_HEADS), 1, 0)
    pos_blk = jnp.arange(s, dtype=jnp.int32).reshape(nb, Q_BLOCK)
    key_pos = jnp.arange(s, dtype=jnp.int32)
    scale = HEAD_DIM ** -0.5
    idx_scale = IDX_DIM ** -0.5

    def block_fn(args):
        qb, iqb, iwb, pos = args
        dots = jnp.einsum('bqhd,bsd->bqhs', iqb, ik).astype(jnp.float32) * idx_scale
        scores = jnp.einsum('bqhs,bqh->bqs', jax.nn.relu(dots), iwb.astype(jnp.float32))
        causal = key_pos[None, None, :] <= pos[None, :, None]
        scores = jnp.where(causal, scores, -jnp.inf)
        _, idx = lax.top_k(scores, k_top)
        kg = jax.vmap(lambda kb, ib: kb[ib])(k, idx)
        vg = jax.vmap(lambda vb, ib: vb[ib])(v, idx)
        logits = jnp.einsum('bqhgd,bqshd->bqhgs', qb, kg).astype(jnp.float32) * scale
        rel = pos[None, :, None] - idx
        bias = rel_bias[t5_causal_bucket(rel)].astype(jnp.float32)
        bias = jnp.transpose(bias.reshape(bsz, Q_BLOCK, k_top, KV_HEADS, GROUPS), (0, 1, 3, 4, 2))
        valid = (rel >= 0)[:, :, None, None, :]
        logits = jnp.where(valid, logits + bias, jnp.finfo(jnp.float32).min)
        p = jax.nn.softmax(logits, axis=-1).astype(vg.dtype)
        out = jnp.einsum('bqhgs,bqshd->bqhgd', p, vg)
        return out.reshape(bsz, Q_BLOCK, Q_WIDTH)

    outs = lax.map(block_fn, (q_blk, iq_blk, iw_blk, pos_blk))
    return jnp.moveaxis(outs, 0, 1).reshape(bsz, s, Q_WIDTH)


def hybrid_mixer(h, w_in, conv_w, conv_b, rg_w_a, rg_b_a, rg_w_x, rg_b_x, rg_lambda,
                 rel_bias, w_proj_rnn, w_proj_attn, w_out):
    proj = h @ w_in
    split_pts = [int(p) for p in np.cumsum(IN_SPLITS)[:-1]]
    rx, rgate, q, k, v, iq, ik, iw, g_rnn, g_attn = jnp.split(proj, split_pts, axis=-1)
    xc = causal_depthwise_conv(rx, conv_w, conv_b)
    y_rnn = rg_lru(xc, rg_w_a, rg_b_a, rg_w_x, rg_b_x, rg_lambda) * jax.nn.gelu(rgate)
    y_attn = dsa_attention(q, k, v, iq, ik, iw, rel_bias)
    merged = (jax.nn.sigmoid(g_rnn) * (y_rnn @ w_proj_rnn)
              + jax.nn.sigmoid(g_attn) * (y_attn @ w_proj_attn))
    return merged @ w_out


def setup_inputs(seed: int = 0) -> dict:
    key = jax.random.key(seed)
    ks = jax.random.split(key, 24)
    f32 = jnp.float32

    def nrm(k, shape, fan_in):
        return jax.random.normal(k, shape, f32) * (fan_in ** -0.5)

    def gain(k, shape):
        return 1.0 + 0.02 * jax.random.normal(k, shape, f32)

    u = jax.random.uniform(ks[13], (DEPTH, RNN_WIDTH), f32, 0.9, 0.999)
    a_base = u ** (1.0 / RG_C)
    rg_lambda = jnp.log(a_base) - jnp.log1p(-a_base)
    return {
        "x": jax.random.normal(ks[0], (BATCH, SEQ, D_MODEL), f32),
        "ffn1_norm": gain(ks[1], (DEPTH, D_MODEL)),
        "ffn1_w_gate": nrm(ks[2], (DEPTH, D_MODEL, D_FF), D_MODEL),
        "ffn1_w_up": nrm(ks[3], (DEPTH, D_MODEL, D_FF), D_MODEL),
        "ffn1_w_down": nrm(ks[4], (DEPTH, D_FF, D_MODEL), D_FF),
        "mix_norm": gain(ks[5], (DEPTH, D_MODEL)),
        "w_in": nrm(ks[6], (DEPTH, D_MODEL, N_IN), D_MODEL),
        "conv_w": nrm(ks[7], (DEPTH, CONV_WIDTH, RNN_WIDTH), CONV_WIDTH),
        "conv_b": 0.01 * jax.random.normal(ks[8], (DEPTH, RNN_WIDTH), f32),
        "rg_w_a": nrm(ks[9], (DEPTH, RNN_BLOCKS, RNN_BLOCK, RNN_BLOCK), RNN_BLOCK),
        "rg_b_a": 0.01 * jax.random.normal(ks[10], (DEPTH, RNN_WIDTH), f32),
        "rg_w_x": nrm(ks[11], (DEPTH, RNN_BLOCKS, RNN_BLOCK, RNN_BLOCK), RNN_BLOCK),
        "rg_b_x": 0.01 * jax.random.normal(ks[12], (DEPTH, RNN_WIDTH), f32),
        "rg_lambda": rg_lambda,
        "rel_bias": 0.5 * jax.random.normal(ks[14], (NUM_BUCKETS, ATTN_HEADS), f32),
        "w_proj_rnn": nrm(ks[15], (DEPTH, RNN_WIDTH, D_MODEL), RNN_WIDTH),
        "w_proj_attn": nrm(ks[16], (DEPTH, Q_WIDTH, D_MODEL), Q_WIDTH),
        "w_out": nrm(ks[17], (DEPTH, D_MODEL, D_MODEL), D_MODEL),
        "ffn2_norm": gain(ks[18], (DEPTH, D_MODEL)),
        "ffn2_w_gate": nrm(ks[19], (DEPTH, D_MODEL, D_FF), D_MODEL),
        "ffn2_w_up": nrm(ks[20], (DEPTH, D_MODEL, D_FF), D_MODEL),
        "ffn2_w_down": nrm(ks[21], (DEPTH, D_FF, D_MODEL), D_FF),
        "final_norm": gain(ks[22], (D_MODEL,)),
    }


def reference(x, ffn1_norm, ffn1_w_gate, ffn1_w_up, ffn1_w_down, mix_norm, w_in, conv_w, conv_b,
              rg_w_a, rg_b_a, rg_w_x, rg_b_x, rg_lambda, rel_bias, w_proj_rnn, w_proj_attn, w_out,
              ffn2_norm, ffn2_w_gate, ffn2_w_up, ffn2_w_down, final_norm):
    for l in range(DEPTH):
        x = x + 0.5 * swiglu(rms_norm(x, ffn1_norm[l]), ffn1_w_gate[l], ffn1_w_up[l], ffn1_w_down[l])
        x = x + hybrid_mixer(rms_norm(x, mix_norm[l]), w_in[l], conv_w[l], conv_b[l],
                             rg_w_a[l], rg_b_a[l], rg_w_x[l], rg_b_x[l], rg_lambda[l],
                             rel_bias, w_proj_rnn[l], w_proj_attn[l], w_out[l])
        x = x + 0.5 * swiglu(rms_norm(x, ffn2_norm[l]), ffn2_w_gate[l], ffn2_w_up[l], ffn2_w_down[l])
    return rms_norm(x, final_norm)
```

```python
import functools
import math

import jax
import jax.numpy as jnp
from jax import lax
from jax.experimental import pallas as pl
from jax.experimental.pallas import tpu as pltpu

F32 = jnp.float32
BF16 = jnp.bfloat16
I32 = jnp.int32

RMS_EPS = 1e-6
CONV_WIDTH = 4
RG_C = 8.0
RNN_BLOCK = 128
HEAD_DIM = 128
KV_HEADS = 4
IDX_HEADS = 16
IDX_DIM = 64
TOPK_MAX = 256
NUM_BUCKETS = 32
MAX_DISTANCE = 128

LANES = 128
SUBLANES = 8
VMEM_LIMIT_BYTES = 56 * 1024 * 1024

KEY_BLOCK = 128
NEG_LOGIT = -1e30
INT32_MIN = -(2 ** 31)
KEY_NEG_INF = -2139095041


def _params(semantics):
    return pltpu.CompilerParams(dimension_semantics=semantics,
                                vmem_limit_bytes=VMEM_LIMIT_BYTES)


def _sigmoid(x):
    return 0.5 * (jnp.tanh(0.5 * x) + 1.0)


def _rms(x, g):
    ms = jnp.mean(x * x, axis=-1, keepdims=True)
    return x * lax.rsqrt(ms + RMS_EPS) * g


def _ffn_kernel(x_ref, g_ref, wg_ref, wu_ref, wd_ref, gn_ref, *rest, emit_resid):
    if emit_resid:
        acc_ref, hn_ref, xn_sc = rest
    else:
        acc_ref, xn_sc = rest
        hn_ref = acc_ref
    j = pl.program_id(1)

    @pl.when(j == 0)
    def _():
        xn_sc[...] = _rms(x_ref[...], g_ref[...]).astype(BF16)
        acc_ref[...] = jnp.zeros_like(acc_ref)

    xn = xn_sc[...]
    h = jnp.dot(xn, wg_ref[...], preferred_element_type=F32)
    u = jnp.dot(xn, wu_ref[...], preferred_element_type=F32)
    a = (h * _sigmoid(h) * u).astype(BF16)
    acc_ref[...] += jnp.dot(a, wd_ref[...], preferred_element_type=F32)

    @pl.when(j == pl.num_programs(1) - 1)
    def _():
        y = x_ref[...] + 0.5 * acc_ref[...]
        if emit_resid:
            acc_ref[...] = y
        hn_ref[...] = _rms(y, gn_ref[...]).astype(hn_ref.dtype)


def _ffn(x, g, wg, wu, wd, gn, *, emit_resid, tm, tf):
    m, d = x.shape
    fp = wg.shape[1]
    row = pl.BlockSpec((tm, d), lambda i, j: (i, 0))
    vec = pl.BlockSpec((1, d), lambda i, j: (0, 0))
    out_shape = [jax.ShapeDtypeStruct((m, d), F32)]
    out_specs = [row]
    if emit_resid:
        out_shape.append(jax.ShapeDtypeStruct((m, d), BF16))
        out_specs.append(row)
    return pl.pallas_call(
        functools.partial(_ffn_kernel, emit_resid=emit_resid),
        grid=(m // tm, fp // tf),
        in_specs=[row, vec,
                  pl.BlockSpec((d, tf), lambda i, j: (0, j)),
                  pl.BlockSpec((d, tf), lambda i, j: (0, j)),
                  pl.BlockSpec((tf, d), lambda i, j: (j, 0)),
                  vec],
        out_specs=out_specs,
        out_shape=out_shape,
        scratch_shapes=[pltpu.VMEM((tm, d), BF16)],
        compiler_params=_params(("parallel", "arbitrary")),
        name="ffn_resid" if emit_resid else "ffn_final",
    )(x, g, wg, wu, wd, gn)


def _mm_kernel(x_ref, w_ref, o_ref):
    o_ref[...] = jnp.dot(x_ref[...], w_ref[...],
                         preferred_element_type=F32).astype(o_ref.dtype)


def _matmul(x, w, out_dtype, *, tm, tn, name):
    m, k = x.shape
    n = w.shape[1]
    return pl.pallas_call(
        _mm_kernel,
        grid=(n // tn, m // tm),
        in_specs=[pl.BlockSpec((tm, k), lambda j, i: (i, 0)),
                  pl.BlockSpec((k, tn), lambda j, i: (0, j))],
        out_specs=pl.BlockSpec((tm, tn), lambda j, i: (i, j)),
        out_shape=jax.ShapeDtypeStruct((m, n), out_dtype),
        compiler_params=_params(("parallel", "parallel")),
        name=name,
    )(x, w)


def _mm_res_kernel(x_ref, w_ref, r_ref, o_ref):
    o_ref[...] = r_ref[...] + jnp.dot(x_ref[...], w_ref[...], preferred_element_type=F32)


def _matmul_residual(x, w, r, *, tm, tn):
    m, k = x.shape
    n = w.shape[1]
    return pl.pallas_call(
        _mm_res_kernel,
        grid=(n // tn, m // tm),
        in_specs=[pl.BlockSpec((tm, k), lambda j, i: (i, 0)),
                  pl.BlockSpec((k, tn), lambda j, i: (0, j)),
                  pl.BlockSpec((tm, tn), lambda j, i: (i, j))],
        out_specs=pl.BlockSpec((tm, tn), lambda j, i: (i, j)),
        out_shape=jax.ShapeDtypeStruct((m, n), F32),
        compiler_params=_params(("parallel", "parallel")),
        name="out_proj",
    )(x, w, r)


def _merge_kernel(yr_ref, ya_ref, wr_ref, wa_ref, gr_ref, ga_ref, o_ref):
    pr = jnp.dot(yr_ref[...], wr_ref[...], preferred_element_type=F32)
    pa = jnp.dot(ya_ref[...], wa_ref[...], preferred_element_type=F32)
    o_ref[...] = (_sigmoid(gr_ref[...]) * pr + _sigmoid(ga_ref[...]) * pa).astype(o_ref.dtype)


def _merge(y_rnn, y_attn, w_r, w_a, pf, gr_col, ga_col, *, tm, tn):
    m, k = y_rnn.shape
    n = w_r.shape[1]
    act = pl.BlockSpec((tm, k), lambda j, i: (i, 0))
    wsp = pl.BlockSpec((k, tn), lambda j, i: (0, j))
    return pl.pallas_call(
        _merge_kernel,
        grid=(n // tn, m // tm),
        in_specs=[act, act, wsp, wsp,
                  pl.BlockSpec((tm, tn), lambda j, i: (i, gr_col // tn + j)),
                  pl.BlockSpec((tm, tn), lambda j, i: (i, ga_col // tn + j))],
        out_specs=pl.BlockSpec((tm, tn), lambda j, i: (i, j)),
        out_shape=jax.ShapeDtypeStruct((m, n), BF16),
        compiler_params=_params(("parallel", "parallel")),
        name="merge",
    )(y_rnn, y_attn, w_r, w_a, pf, pf)


def _rglru_kernel(rx_ref, rg_ref, cw_ref, cb_ref, wax_ref, ba_ref, bx_ref, lam_ref,
                  y_ref, xs_sc, h_sc):
    ts, tc = rx_ref.shape
    ng = ts // SUBLANES

    @pl.when(pl.program_id(2) == 0)
    def _():
        xs_sc[0:SUBLANES, :] = jnp.zeros((SUBLANES, tc), F32)
        h_sc[...] = jnp.zeros_like(h_sc)

    x = rx_ref[...]
    xs_sc[SUBLANES:SUBLANES + ts, :] = x
    cw = cw_ref[...]
    xc = cb_ref[...]
    for k in range(CONV_WIDTH - 1):
        off = SUBLANES - (CONV_WIDTH - 1) + k
        xc = xc + xs_sc[off:off + ts, :] * cw[k:k + 1, :]
    xc = xc + x * cw[CONV_WIDTH - 1:CONV_WIDTH, :]
    xs_sc[0:SUBLANES, :] = x[ts - SUBLANES:ts, :]

    xcb = xc.astype(BF16)
    r_parts, i_parts = [], []
    for n in range(tc // RNN_BLOCK):
        g = jnp.dot(xcb[:, n * RNN_BLOCK:(n + 1) * RNN_BLOCK], wax_ref[n],
                    preferred_element_type=F32)
        r_parts.append(g[:, :RNN_BLOCK])
        i_parts.append(g[:, RNN_BLOCK:])
    r = _sigmoid(jnp.concatenate(r_parts, axis=1) + ba_ref[...])
    i = _sigmoid(jnp.concatenate(i_parts, axis=1) + bx_ref[...])

    nl = -lam_ref[...]
    softplus = jnp.maximum(nl, 0.0) + jnp.log1p(jnp.exp(-jnp.abs(nl)))
    log_a = (-RG_C * r) * softplus
    a = jnp.exp(log_a)
    mult = jnp.sqrt(-jnp.tanh(log_a) * (a * a + 1.0))
    u = mult * (i * xc)

    av = a.reshape(ng, SUBLANES, tc)
    bv = u.reshape(ng, SUBLANES, tc)
    ri = lax.broadcasted_iota(I32, (ng, SUBLANES, tc), 1)
    for d in (1, 2, 4):
        a_prev = jnp.where(ri >= d, pltpu.roll(av, d, axis=1), 1.0)
        b_prev = jnp.where(ri >= d, pltpu.roll(bv, d, axis=1), 0.0)
        bv = av * b_prev + bv
        av = av * a_prev
    carry = h_sc[0:1, :]
    hs = []
    for k in range(ng):
        hk = bv[k] + av[k] * carry
        hs.append(hk)
        carry = hk[SUBLANES - 1:SUBLANES, :]
    h_sc[0:1, :] = carry
    h = jnp.concatenate(hs, axis=0)

    gt = rg_ref[...]
    gelu = 0.5 * gt * (1.0 + jnp.tanh(math.sqrt(2.0 / math.pi) * (gt + 0.044715 * (gt * gt * gt))))
    y_ref[...] = (h * gelu).astype(y_ref.dtype)


def _rglru(pf, rx_col, rg_col, conv_w, conv_b, wax, b_a, b_x, lam, *, width, ts, tc):
    bsz, s, _ = pf.shape
    chan = lambda rows: pl.BlockSpec((rows, tc), lambda b, c, t: (0, c))
    return pl.pallas_call(
        _rglru_kernel,
        grid=(bsz, width // tc, s // ts),
        in_specs=[pl.BlockSpec((None, ts, tc), lambda b, c, t: (b, t, rx_col // tc + c)),
                  pl.BlockSpec((None, ts, tc), lambda b, c, t: (b, t, rg_col // tc + c)),
                  chan(CONV_WIDTH), chan(1),
                  pl.BlockSpec((tc // RNN_BLOCK, RNN_BLOCK, 2 * RNN_BLOCK),
                               lambda b, c, t: (c, 0, 0)),
                  chan(1), chan(1), chan(1)],
        out_specs=pl.BlockSpec((None, ts, tc), lambda b, c, t: (b, t, c)),
        out_shape=jax.ShapeDtypeStruct((bsz, s, width), BF16),
        scratch_shapes=[pltpu.VMEM((SUBLANES + ts, tc), F32),
                        pltpu.VMEM((SUBLANES, tc), F32)],
        compiler_params=_params(("parallel", "parallel", "arbitrary")),
        name="rglru",
    )(pf, pf, conv_w, conv_b, wax, b_a, b_x, lam)


def _bias_kernel(rb_ref, o_ref):
    h = pl.program_id(0)
    ii = lax.broadcasted_iota(I32, (KEY_BLOCK, KEY_BLOCK), 0)
    jj = lax.broadcasted_iota(I32, (KEY_BLOCK, KEY_BLOCK), 1)
    max_exact = NUM_BUCKETS // 2
    for d in range(2):
        n = jnp.maximum(ii - jj + KEY_BLOCK * d, 0)
        nf = jnp.maximum(n, 1).astype(F32)
        large = max_exact + (jnp.log(nf / max_exact) / math.log(MAX_DISTANCE / max_exact)
                             * (NUM_BUCKETS - max_exact)).astype(I32)
        large = jnp.minimum(large, NUM_BUCKETS - 1)
        bucket = jnp.where(n < max_exact, n, large)
        acc = jnp.zeros((KEY_BLOCK, KEY_BLOCK), F32)
        for b in range(NUM_BUCKETS):
            acc = jnp.where(bucket == b, rb_ref[b, h], acc)
        o_ref[0, d] = acc
    o_ref[0, 2] = jnp.full((KEY_BLOCK, KEY_BLOCK), rb_ref[NUM_BUCKETS - 1, h], F32)


def _bias_tiles(rel_bias):
    heads = rel_bias.shape[1]
    return pl.pallas_call(
        _bias_kernel,
        grid=(heads,),
        in_specs=[pl.BlockSpec(memory_space=pltpu.SMEM)],
        out_specs=pl.BlockSpec((1, 3, KEY_BLOCK, KEY_BLOCK), lambda h: (h, 0, 0, 0)),
        out_shape=jax.ShapeDtypeStruct((heads, 3, KEY_BLOCK, KEY_BLOCK), F32),
        compiler_params=_params(("parallel",)),
        name="bias_tiles",
    )(rel_bias)


def _attn_kernel(q_ref, k_ref, v_ref, iq_ref, ik_ref, iw_ref, bt_ref, y_ref,
                 keys_sc, am_sc, wb_sc, lg_sc, m_sc, l_sc, acc_sc, *, k_top):
    tq = q_ref.shape[0]
    groups = q_ref.shape[1] // (KV_HEADS * HEAD_DIM)
    qi = pl.program_id(1)
    nkb = qi + 1
    nquad = (qi + 4) // 4
    row = qi * tq + lax.broadcasted_iota(I32, (tq, KEY_BLOCK), 0)
    lane = lax.broadcasted_iota(I32, (tq, KEY_BLOCK), 1)

    w_scale = (IDX_HEADS ** -0.5) * (IDX_DIM ** -0.5)
    iw = iw_ref[...]
    for h in range(IDX_HEADS):
        col = iw[:, IDX_DIM + h:IDX_DIM + h + 1] * w_scale
        wb_sc[h] = jnp.broadcast_to(col, (tq, KEY_BLOCK))
    iq = iq_ref[...]
    iq_rows = jnp.concatenate(
        [iq[:, h * IDX_DIM:(h + 1) * IDX_DIM] for h in range(IDX_HEADS)], axis=0)

    def score_block(kb, carry):
        start = pl.multiple_of(kb * KEY_BLOCK, KEY_BLOCK)
        ikb = ik_ref[pl.ds(start, KEY_BLOCK), 0:IDX_DIM].astype(BF16)
        dots = lax.dot_general(iq_rows, ikb, (((1,), (1,)), ((), ())),
                               preferred_element_type=F32)
        acc = jnp.zeros((tq, KEY_BLOCK), F32)
        for h in range(IDX_HEADS):
            acc = acc + jnp.maximum(dots[h * tq:(h + 1) * tq], 0.0) * wb_sc[h]
        score = jnp.where(start + lane <= row, acc, -jnp.inf)
        bits = pltpu.bitcast(score, I32)
        keys_sc[kb] = bits ^ ((bits >> 31) & 0x7FFFFFFF)
        return carry

    lax.fori_loop(0, nkb, score_block, 0)

    def pad_block(kb, carry):
        keys_sc[kb] = jnp.full((tq, KEY_BLOCK), KEY_NEG_INF, I32)
        return carry

    lax.fori_loop(nkb, 4 * nquad, pad_block, 0)

    def search(_):
        def bit_step(it, thr):
            cand = thr + lax.shift_left(jnp.int32(1), 31 - it)
            cand_b = jnp.broadcast_to(cand, (tq, KEY_BLOCK))

            def count_quad(qd, cnt):
                for r in range(4):
                    cnt = cnt + (keys_sc[4 * qd + r] >= cand_b).astype(I32)
                return cnt

            cnt = lax.fori_loop(0, nquad, count_quad, jnp.zeros((tq, KEY_BLOCK), I32))
            total = jnp.sum(cnt, axis=1, keepdims=True)
            return jnp.where(total >= k_top, cand, thr)

        return lax.fori_loop(0, 32, bit_step, jnp.full((tq, 1), INT32_MIN, I32))

    thr = lax.cond(nkb * KEY_BLOCK > k_top, search,
                   lambda _: jnp.full((tq, 1), INT32_MIN, I32), 0)
    thr_b = jnp.broadcast_to(jnp.maximum(thr, KEY_NEG_INF + 1), (tq, KEY_BLOCK))

    def mask_block(kb, carry):
        am_sc[kb] = jnp.where(keys_sc[kb] >= thr_b, 0.0, NEG_LOGIT)
        return carry

    lax.fori_loop(0, nkb, mask_block, 0)

    scale = HEAD_DIM ** -0.5
    for g in range(KV_HEADS):
        c0 = g * HEAD_DIM
        qg = jnp.concatenate(
            [q_ref[:, (g * groups + j) * HEAD_DIM:(g * groups + j + 1) * HEAD_DIM]
             for j in range(groups)], axis=0)
        m_sc[...] = jnp.full(m_sc.shape, NEG_LOGIT, F32)

        def logit_block(kb, carry):
            start = pl.multiple_of(kb * KEY_BLOCK, KEY_BLOCK)
            kblk = k_ref[pl.ds(start, KEY_BLOCK), c0:c0 + HEAD_DIM]
            s = lax.dot_general(qg, kblk, (((1,), (1,)), ((), ())),
                                preferred_element_type=F32) * scale
            dd = jnp.minimum(qi - kb, 2)
            am = am_sc[kb]
            for j in range(groups):
                sj = s[j * tq:(j + 1) * tq] + bt_ref[g * groups + j, dd] + am
                lg_sc[kb, j * tq:(j + 1) * tq, :] = sj
                m_sc[j * tq:(j + 1) * tq, :] = jnp.maximum(m_sc[j * tq:(j + 1) * tq, :], sj)
            return carry

        lax.fori_loop(0, nkb, logit_block, 0)
        m_row = jnp.max(m_sc[...], axis=1, keepdims=True)
        l_sc[...] = jnp.zeros_like(l_sc)
        acc_sc[...] = jnp.zeros_like(acc_sc)

        def value_block(kb, carry):
            start = pl.multiple_of(kb * KEY_BLOCK, KEY_BLOCK)
            p = jnp.exp(lg_sc[kb] - m_row)
            l_sc[...] += p
            acc_sc[...] += jnp.dot(p.astype(BF16), v_ref[pl.ds(start, KEY_BLOCK), c0:c0 + HEAD_DIM],
                                   preferred_element_type=F32)
            return carry

        lax.fori_loop(0, nkb, value_block, 0)
        out = acc_sc[...] / jnp.sum(l_sc[...], axis=1, keepdims=True)
        for j in range(groups):
            hd = (g * groups + j) * HEAD_DIM
            y_ref[:, hd:hd + HEAD_DIM] = out[j * tq:(j + 1) * tq].astype(y_ref.dtype)


def _attention(pb, ps, btab, *, q_width, kv_width, tq):
    bsz, s, _ = pb.shape
    iq_width = IDX_HEADS * IDX_DIM
    k_col = q_width
    v_col = q_width + kv_width
    iq_col = q_width + 2 * kv_width
    nkb_max = s // KEY_BLOCK
    nkb_pad = 4 * ((nkb_max + 3) // 4)
    groups = q_width // (KV_HEADS * HEAD_DIM)
    k_top = min(TOPK_MAX, s // 4)
    return pl.pallas_call(
        functools.partial(_attn_kernel, k_top=k_top),
        grid=(bsz, s // tq),
        in_specs=[pl.BlockSpec((None, tq, q_width), lambda b, i: (b, i, 0)),
                  pl.BlockSpec((None, s, kv_width), lambda b, i: (b, 0, k_col // kv_width)),
                  pl.BlockSpec((None, s, kv_width), lambda b, i: (b, 0, v_col // kv_width)),
                  pl.BlockSpec((None, tq, iq_width), lambda b, i: (b, i, iq_col // iq_width)),
                  pl.BlockSpec((None, s, LANES), lambda b, i: (b, 0, 0)),
                  pl.BlockSpec((None, tq, LANES), lambda b, i: (b, i, 0)),
                  pl.BlockSpec(btab.shape, lambda b, i: (0, 0, 0, 0))],
        out_specs=pl.BlockSpec((None, tq, q_width), lambda b, i: (b, i, 0)),
        out_shape=jax.ShapeDtypeStruct((bsz, s, q_width), BF16),
        scratch_shapes=[pltpu.VMEM((nkb_pad, tq, KEY_BLOCK), I32),
                        pltpu.VMEM((nkb_max, tq, KEY_BLOCK), F32),
                        pltpu.VMEM((IDX_HEADS, tq, KEY_BLOCK), F32),
                        pltpu.VMEM((nkb_max, groups * tq, KEY_BLOCK), F32),
                        pltpu.VMEM((groups * tq, KEY_BLOCK), F32),
                        pltpu.VMEM((groups * tq, KEY_BLOCK), F32),
                        pltpu.VMEM((groups * tq, HEAD_DIM), F32)],
        compiler_params=_params(("parallel", "arbitrary")),
        name="sparse_attn",
    )(pb, pb, pb, pb, ps, ps, btab)


def _pad_cols(w, n):
    return jnp.pad(w, ((0, 0), (0, n - w.shape[1])))


def _pad_rows(w, n):
    return jnp.pad(w, ((0, n - w.shape[0]), (0, 0)))


def _layer(x, ffn1_norm, ffn1_w_gate, ffn1_w_up, ffn1_w_down, mix_norm, w_in, conv_w, conv_b,
           rg_w_a, rg_b_a, rg_w_x, rg_b_x, rg_lambda, btab, w_proj_rnn, w_proj_attn, w_out,
           ffn2_norm, ffn2_w_gate, ffn2_w_up, ffn2_w_down, next_norm, *, last):
    bsz, s, d = x.shape
    m = bsz * s
    width = d
    q_width = d
    kv_width = KV_HEADS * HEAD_DIM
    iq_width = IDX_HEADS * IDX_DIM
    tf = 512
    f_pad = -(-ffn1_w_gate.shape[1] // tf) * tf
    tm = min(512, m)
    tmm = min(1024, m)

    o_rx, o_rg = 0, width
    o_q = 2 * width
    o_k = o_q + q_width
    o_v = o_k + kv_width
    o_iq = o_v + kv_width
    o_ik = o_iq + iq_width
    o_iw = o_ik + IDX_DIM
    o_gr = o_iw + IDX_HEADS
    o_ga = o_gr + d

    def ffn_weights(wg, wu, wd):
        return (_pad_cols(wg, f_pad).astype(BF16), _pad_cols(wu, f_pad).astype(BF16),
                _pad_rows(wd, f_pad).astype(BF16))

    vec = lambda p: p.reshape(1, -1)

    x1, hn = _ffn(x.reshape(m, d), vec(ffn1_norm), *ffn_weights(ffn1_w_gate, ffn1_w_up, ffn1_w_down),
                  vec(mix_norm), emit_resid=True, tm=tm, tf=tf)

    w_f = jnp.concatenate([w_in[:, o_rx:o_q], w_in[:, o_gr:]], axis=1).astype(BF16)
    w_b = w_in[:, o_q:o_ik].astype(BF16)
    w_s = _pad_cols(w_in[:, o_ik:o_gr], LANES).astype(BF16)
    pf = _matmul(hn, w_f, F32, tm=tmm, tn=1024, name="in_proj_f32")
    pb = _matmul(hn, w_b, BF16, tm=tmm, tn=1024, name="in_proj_bf16")
    ps = _matmul(hn, w_s, F32, tm=tmm, tn=LANES, name="in_proj_idx")

    wax = jnp.concatenate([rg_w_a, rg_w_x], axis=-1).astype(BF16)
    y_rnn = _rglru(pf.reshape(bsz, s, -1), 0, width, conv_w, vec(conv_b), wax,
                   vec(rg_b_a), vec(rg_b_x), vec(rg_lambda),
                   width=width, ts=min(256, s), tc=512)

    y_attn = _attention(pb.reshape(bsz, s, -1), ps.reshape(bsz, s, -1), btab,
                        q_width=q_width, kv_width=kv_width, tq=KEY_BLOCK)

    merged = _merge(y_rnn.reshape(m, width), y_attn.reshape(m, q_width),
                    w_proj_rnn.astype(BF16), w_proj_attn.astype(BF16), pf,
                    2 * width, 2 * width + d, tm=tm, tn=1024)
    x2 = _matmul_residual(merged, w_out.astype(BF16), x1, tm=tmm, tn=1024)

    outs = _ffn(x2, vec(ffn2_norm), *ffn_weights(ffn2_w_gate, ffn2_w_up, ffn2_w_down),
                vec(next_norm), emit_resid=not last, tm=tm, tf=tf)
    return outs[0].reshape(bsz, s, d)


def kernel(x, ffn1_norm, ffn1_w_gate, ffn1_w_up, ffn1_w_down, mix_norm, w_in, conv_w, conv_b,
           rg_w_a, rg_b_a, rg_w_x, rg_b_x, rg_lambda, rel_bias, w_proj_rnn, w_proj_attn, w_out,
           ffn2_norm, ffn2_w_gate, ffn2_w_up, ffn2_w_down, final_norm):
    depth = ffn1_norm.shape[0]
    assert depth == 1, "the fused final norm assumes a single layer"
    btab = _bias_tiles(rel_bias)
    l = 0
    return _layer(x, ffn1_norm[l], ffn1_w_gate[l], ffn1_w_up[l], ffn1_w_down[l], mix_norm[l],
                  w_in[l], conv_w[l], conv_b[l], rg_w_a[l], rg_b_a[l], rg_w_x[l], rg_b_x[l],
                  rg_lambda[l], btab, w_proj_rnn[l], w_proj_attn[l], w_out[l], ffn2_norm[l],
                  ffn2_w_gate[l], ffn2_w_up[l], ffn2_w_down[l], final_norm, last=True)
```

```python
import functools
import math

import jax
import jax.numpy as jnp
from jax import lax
from jax.experimental import pallas as pl
from jax.experimental.pallas import tpu as pltpu

F32 = jnp.float32
BF16 = jnp.bfloat16
I32 = jnp.int32

RMS_EPS = 1e-6
CONV_WIDTH = 4
RG_C = 8.0
RNN_BLOCK = 128
HEAD_DIM = 128
KV_HEADS = 4
IDX_HEADS = 16
IDX_DIM = 64
TOPK_MAX = 256
NUM_BUCKETS = 32
MAX_DISTANCE = 128

LANES = 128
SUBLANES = 8
VMEM_LIMIT_BYTES = 56 * 1024 * 1024

KEY_BLOCK = 128
NEG_LOGIT = -1e30
INT32_MIN = -(2 ** 31)
KEY_NEG_INF = -2139095041


def _params(semantics):
    return pltpu.CompilerParams(dimension_semantics=semantics,
                                vmem_limit_bytes=VMEM_LIMIT_BYTES)


def _sigmoid(x):
    return 0.5 * (jnp.tanh(0.5 * x) + 1.0)


def _rms(x, g):
    ms = jnp.mean(x * x, axis=-1, keepdims=True)
    return x * lax.rsqrt(ms + RMS_EPS) * g


def _tree_sum(parts):
    while len(parts) > 1:
        parts = [a + b for a, b in zip(parts[::2], parts[1::2])] + (
            [parts[-1]] if len(parts) % 2 else [])
    return parts[0]


def _ffn_kernel(x_ref, g_ref, wg_ref, wu_ref, wd_ref, gn_ref, *rest, emit_resid):
    if emit_resid:
        acc_ref, hn_ref, xn_sc = rest
    else:
        acc_ref, xn_sc = rest
        hn_ref = acc_ref
    j = pl.program_id(1)

    @pl.when(j == 0)
    def _():
        xn_sc[...] = _rms(x_ref[...], g_ref[...]).astype(BF16)
        acc_ref[...] = jnp.zeros_like(acc_ref)

    xn = xn_sc[...]
    h = jnp.dot(xn, wg_ref[...], preferred_element_type=F32)
    u = jnp.dot(xn, wu_ref[...], preferred_element_type=F32)
    a = (h * _sigmoid(h) * u).astype(BF16)
    acc_ref[...] += jnp.dot(a, wd_ref[...], preferred_element_type=F32)

    @pl.when(j == pl.num_programs(1) - 1)
    def _():
        y = x_ref[...] + 0.5 * acc_ref[...]
        if emit_resid:
            acc_ref[...] = y
        hn_ref[...] = _rms(y, gn_ref[...]).astype(hn_ref.dtype)


def _ffn(x, g, wg, wu, wd, gn, *, emit_resid, tm, tf):
    m, d = x.shape
    fp = wg.shape[1]
    row = pl.BlockSpec((tm, d), lambda i, j: (i, 0))
    vec = pl.BlockSpec((1, d), lambda i, j: (0, 0))
    out_shape = [jax.ShapeDtypeStruct((m, d), F32)]
    out_specs = [row]
    if emit_resid:
        out_shape.append(jax.ShapeDtypeStruct((m, d), BF16))
        out_specs.append(row)
    return pl.pallas_call(
        functools.partial(_ffn_kernel, emit_resid=emit_resid),
        grid=(m // tm, fp // tf),
        in_specs=[row, vec,
                  pl.BlockSpec((d, tf), lambda i, j: (0, j)),
                  pl.BlockSpec((d, tf), lambda i, j: (0, j)),
                  pl.BlockSpec((tf, d), lambda i, j: (j, 0)),
                  vec],
        out_specs=out_specs,
        out_shape=out_shape,
        scratch_shapes=[pltpu.VMEM((tm, d), BF16)],
        compiler_params=_params(("parallel", "arbitrary")),
        name="ffn_resid" if emit_resid else "ffn_final",
    )(x, g, wg, wu, wd, gn)


def _mm_kernel(x_ref, w_ref, o_ref):
    o_ref[...] = jnp.dot(x_ref[...], w_ref[...],
                         preferred_element_type=F32).astype(o_ref.dtype)


def _matmul(x, w, out_dtype, *, tm, tn, name):
    m, k = x.shape
    n = w.shape[1]
    return pl.pallas_call(
        _mm_kernel,
        grid=(n // tn, m // tm),
        in_specs=[pl.BlockSpec((tm, k), lambda j, i: (i, 0)),
                  pl.BlockSpec((k, tn), lambda j, i: (0, j))],
        out_specs=pl.BlockSpec((tm, tn), lambda j, i: (i, j)),
        out_shape=jax.ShapeDtypeStruct((m, n), out_dtype),
        compiler_params=_params(("parallel", "parallel")),
        name=name,
    )(x, w)


def _mm_res_kernel(x_ref, w_ref, r_ref, o_ref):
    o_ref[...] = r_ref[...] + jnp.dot(x_ref[...], w_ref[...], preferred_element_type=F32)


def _matmul_residual(x, w, r, *, tm, tn):
    m, k = x.shape
    n = w.shape[1]
    return pl.pallas_call(
        _mm_res_kernel,
        grid=(n // tn, m // tm),
        in_specs=[pl.BlockSpec((tm, k), lambda j, i: (i, 0)),
                  pl.BlockSpec((k, tn), lambda j, i: (0, j)),
                  pl.BlockSpec((tm, tn), lambda j, i: (i, j))],
        out_specs=pl.BlockSpec((tm, tn), lambda j, i: (i, j)),
        out_shape=jax.ShapeDtypeStruct((m, n), F32),
        compiler_params=_params(("parallel", "parallel")),
        name="out_proj",
    )(x, w, r)


def _merge_kernel(yr_ref, ya_ref, wr_ref, wa_ref, gr_ref, ga_ref, o_ref):
    pr = jnp.dot(yr_ref[...], wr_ref[...], preferred_element_type=F32)
    pa = jnp.dot(ya_ref[...], wa_ref[...], preferred_element_type=F32)
    o_ref[...] = (_sigmoid(gr_ref[...]) * pr + _sigmoid(ga_ref[...]) * pa).astype(o_ref.dtype)


def _merge(y_rnn, y_attn, w_r, w_a, pf, gr_col, ga_col, *, tm, tn):
    m, k = y_rnn.shape
    n = w_r.shape[1]
    act = pl.BlockSpec((tm, k), lambda j, i: (i, 0))
    wsp = pl.BlockSpec((k, tn), lambda j, i: (0, j))
    return pl.pallas_call(
        _merge_kernel,
        grid=(n // tn, m // tm),
        in_specs=[act, act, wsp, wsp,
                  pl.BlockSpec((tm, tn), lambda j, i: (i, gr_col // tn + j)),
                  pl.BlockSpec((tm, tn), lambda j, i: (i, ga_col // tn + j))],
        out_specs=pl.BlockSpec((tm, tn), lambda j, i: (i, j)),
        out_shape=jax.ShapeDtypeStruct((m, n), BF16),
        compiler_params=_params(("parallel", "parallel")),
        name="merge",
    )(y_rnn, y_attn, w_r, w_a, pf, pf)


def _rglru_kernel(rx_ref, rg_ref, cw_ref, cb_ref, wax_ref, ba_ref, bx_ref, lam_ref,
                  y_ref, xs_sc, h_sc):
    ts, tc = rx_ref.shape
    ng = ts // SUBLANES

    @pl.when(pl.program_id(2) == 0)
    def _():
        xs_sc[0:SUBLANES, :] = jnp.zeros((SUBLANES, tc), F32)
        h_sc[...] = jnp.zeros_like(h_sc)

    x = rx_ref[...]
    xs_sc[SUBLANES:SUBLANES + ts, :] = x
    cw = cw_ref[...]
    xc = cb_ref[...]
    for k in range(CONV_WIDTH - 1):
        off = SUBLANES - (CONV_WIDTH - 1) + k
        xc = xc + xs_sc[off:off + ts, :] * cw[k:k + 1, :]
    xc = xc + x * cw[CONV_WIDTH - 1:CONV_WIDTH, :]
    xs_sc[0:SUBLANES, :] = x[ts - SUBLANES:ts, :]

    xcb = xc.astype(BF16)
    r_parts, i_parts = [], []
    for n in range(tc // RNN_BLOCK):
        g = jnp.dot(xcb[:, n * RNN_BLOCK:(n + 1) * RNN_BLOCK], wax_ref[n],
                    preferred_element_type=F32)
        r_parts.append(g[:, :RNN_BLOCK])
        i_parts.append(g[:, RNN_BLOCK:])
    r = _sigmoid(jnp.concatenate(r_parts, axis=1) + ba_ref[...])
    i = _sigmoid(jnp.concatenate(i_parts, axis=1) + bx_ref[...])

    nl = -lam_ref[...]
    softplus = jnp.maximum(nl, 0.0) + jnp.log1p(jnp.exp(-jnp.abs(nl)))
    log_a = (-RG_C * r) * softplus
    a = jnp.exp(log_a)
    mult = jnp.sqrt(-jnp.tanh(log_a) * (a * a + 1.0))
    u = mult * (i * xc)

    av = a.reshape(ng, SUBLANES, tc)
    bv = u.reshape(ng, SUBLANES, tc)
    ri = lax.broadcasted_iota(I32, (ng, SUBLANES, tc), 1)
    for d in (1, 2, 4):
        a_prev = jnp.where(ri >= d, pltpu.roll(av, d, axis=1), 1.0)
        b_prev = jnp.where(ri >= d, pltpu.roll(bv, d, axis=1), 0.0)
        bv = av * b_prev + bv
        av = av * a_prev
    carry = h_sc[0:1, :]
    hs = []
    for k in range(ng):
        hk = bv[k] + av[k] * carry
        hs.append(hk)
        carry = hk[SUBLANES - 1:SUBLANES, :]
    h_sc[0:1, :] = carry
    h = jnp.concatenate(hs, axis=0)

    gt = rg_ref[...]
    gelu = 0.5 * gt * (1.0 + jnp.tanh(math.sqrt(2.0 / math.pi) * (gt + 0.044715 * (gt * gt * gt))))
    y_ref[...] = (h * gelu).astype(y_ref.dtype)


def _rglru(pf, rx_col, rg_col, conv_w, conv_b, wax, b_a, b_x, lam, *, width, ts, tc):
    bsz, s, _ = pf.shape
    chan = lambda rows: pl.BlockSpec((rows, tc), lambda b, c, t: (0, c))
    return pl.pallas_call(
        _rglru_kernel,
        grid=(bsz, width // tc, s // ts),
        in_specs=[pl.BlockSpec((None, ts, tc), lambda b, c, t: (b, t, rx_col // tc + c)),
                  pl.BlockSpec((None, ts, tc), lambda b, c, t: (b, t, rg_col // tc + c)),
                  chan(CONV_WIDTH), chan(1),
                  pl.BlockSpec((tc // RNN_BLOCK, RNN_BLOCK, 2 * RNN_BLOCK),
                               lambda b, c, t: (c, 0, 0)),
                  chan(1), chan(1), chan(1)],
        out_specs=pl.BlockSpec((None, ts, tc), lambda b, c, t: (b, t, c)),
        out_shape=jax.ShapeDtypeStruct((bsz, s, width), BF16),
        scratch_shapes=[pltpu.VMEM((SUBLANES + ts, tc), F32),
                        pltpu.VMEM((SUBLANES, tc), F32)],
        compiler_params=_params(("parallel", "parallel", "arbitrary")),
        name="rglru",
    )(pf, pf, conv_w, conv_b, wax, b_a, b_x, lam)


def _bias_kernel(rb_ref, o_ref):
    h = pl.program_id(0)
    ii = lax.broadcasted_iota(I32, (KEY_BLOCK, KEY_BLOCK), 0)
    jj = lax.broadcasted_iota(I32, (KEY_BLOCK, KEY_BLOCK), 1)
    max_exact = NUM_BUCKETS // 2
    for d in range(2):
        n = jnp.maximum(ii - jj + KEY_BLOCK * d, 0)
        nf = jnp.maximum(n, 1).astype(F32)
        large = max_exact + (jnp.log(nf / max_exact) / math.log(MAX_DISTANCE / max_exact)
                             * (NUM_BUCKETS - max_exact)).astype(I32)
        large = jnp.minimum(large, NUM_BUCKETS - 1)
        bucket = jnp.where(n < max_exact, n, large)
        acc = jnp.zeros((KEY_BLOCK, KEY_BLOCK), F32)
        for b in range(NUM_BUCKETS):
            acc = jnp.where(bucket == b, rb_ref[b, h], acc)
        o_ref[0, d] = acc
    o_ref[0, 2] = jnp.full((KEY_BLOCK, KEY_BLOCK), rb_ref[NUM_BUCKETS - 1, h], F32)


def _bias_tiles(rel_bias):
    heads = rel_bias.shape[1]
    return pl.pallas_call(
        _bias_kernel,
        grid=(heads,),
        in_specs=[pl.BlockSpec(memory_space=pltpu.SMEM)],
        out_specs=pl.BlockSpec((1, 3, KEY_BLOCK, KEY_BLOCK), lambda h: (h, 0, 0, 0)),
        out_shape=jax.ShapeDtypeStruct((heads, 3, KEY_BLOCK, KEY_BLOCK), F32),
        compiler_params=_params(("parallel",)),
        name="bias_tiles",
    )(rel_bias)


SCORE_CHUNK = 4
LOGIT_CHUNK = 2


def _attn_body(q_ref, k_ref, v_ref, iq_ref, ik_ref, iw_ref, bt_ref, y_ref,
               keys_sc, keyst_sc, am_sc, wb_sc, lg_sc, pb_sc, *, k_top, nkb):
    tq = q_ref.shape[0]
    groups = q_ref.shape[1] // (KV_HEADS * HEAD_DIM)
    nk = nkb * KEY_BLOCK
    qi = pl.program_id(1)
    row = qi * tq + lax.broadcasted_iota(I32, (tq, KEY_BLOCK), 0)
    lane = lax.broadcasted_iota(I32, (tq, KEY_BLOCK), 1)

    w_scale = (IDX_HEADS ** -0.5) * (IDX_DIM ** -0.5)
    iw = iw_ref[...]
    for h in range(IDX_HEADS):
        col = iw[:, IDX_DIM + h:IDX_DIM + h + 1] * w_scale
        wb_sc[h] = jnp.broadcast_to(col, (tq, KEY_BLOCK))
    iq = iq_ref[...]
    iq_rows = jnp.concatenate(
        [iq[:, h * IDX_DIM:(h + 1) * IDX_DIM] for h in range(IDX_HEADS)], axis=0)

    for c0 in range(0, nkb, SCORE_CHUNK):
        nb = min(SCORE_CHUNK, nkb - c0)
        ikc = ik_ref[c0 * KEY_BLOCK:(c0 + nb) * KEY_BLOCK, 0:IDX_DIM].astype(BF16)
        dots = lax.dot_general(iq_rows, ikc, (((1,), (1,)), ((), ())),
                               preferred_element_type=F32)
        for sub in range(nb):
            kb = c0 + sub
            acc = jnp.zeros((tq, KEY_BLOCK), F32)
            for h in range(IDX_HEADS):
                d = dots[h * tq:(h + 1) * tq, sub * KEY_BLOCK:(sub + 1) * KEY_BLOCK]
                acc = acc + jnp.maximum(d, 0.0) * wb_sc[h]
            score = jnp.where(kb * KEY_BLOCK + lane <= row, acc, -jnp.inf)
            bits = pltpu.bitcast(score, I32)
            key = bits ^ ((bits >> 31) & 0x7FFFFFFF)
            keys_sc[kb] = key
            keyst_sc[kb] = key.T.reshape(KEY_BLOCK // SUBLANES, SUBLANES, tq)

    def search(_):
        def bit_step(it, thr):
            cand = thr + lax.shift_left(jnp.int32(1), 31 - it)
            cand8 = jnp.broadcast_to(cand, (SUBLANES, tq))[None]
            parts = [jnp.sum((keyst_sc[kb] >= cand8).astype(I32), axis=0)
                     for kb in range(nkb)]
            total = jnp.sum(_tree_sum(parts), axis=0, keepdims=True)
            return jnp.where(total >= k_top, cand, thr)

        return lax.fori_loop(0, 32, bit_step, jnp.full((1, tq), INT32_MIN, I32))

    thr = lax.cond((qi + 1) * tq > k_top, search,
                   lambda _: jnp.full((1, tq), INT32_MIN, I32), 0)
    thr = jnp.maximum(thr, KEY_NEG_INF + 1)
    thr_b = jnp.broadcast_to(thr, (KEY_BLOCK, tq)).T
    for kb in range(nkb):
        am_sc[kb] = jnp.where(keys_sc[kb] >= thr_b, 0.0, NEG_LOGIT)

    scale = HEAD_DIM ** -0.5
    rows = groups * tq
    for g in range(KV_HEADS):
        c0 = g * HEAD_DIM
        qg = jnp.concatenate(
            [q_ref[:, (g * groups + j) * HEAD_DIM:(g * groups + j + 1) * HEAD_DIM]
             for j in range(groups)], axis=0)
        s = lax.dot_general(qg, k_ref[0:nk, c0:c0 + HEAD_DIM], (((1,), (1,)), ((), ())),
                            preferred_element_type=F32) * scale
        m_parts = [[] for _ in range(groups)]
        for kb in range(nkb):
            dd = jnp.clip(qi - kb, 0, 2)
            am = am_sc[kb]
            for j in range(groups):
                sj = (s[j * tq:(j + 1) * tq, kb * KEY_BLOCK:(kb + 1) * KEY_BLOCK]
                      + bt_ref[g * groups + j, dd] + am)
                lg_sc[j * tq:(j + 1) * tq, kb * KEY_BLOCK:(kb + 1) * KEY_BLOCK] = sj
                m_parts[j].append(sj)
        m_row = jnp.concatenate(
            [jnp.max(functools.reduce(jnp.maximum, mp), axis=1, keepdims=True) for mp in m_parts],
            axis=0)
        l_parts = []
        for kb in range(0, nkb, LOGIT_CHUNK):
            nb = min(LOGIT_CHUNK, nkb - kb)
            cols = slice(kb * KEY_BLOCK, (kb + nb) * KEY_BLOCK)
            p = jnp.exp(lg_sc[:, cols] - m_row)
            pb_sc[:, cols] = p.astype(BF16)
            for sub in range(nb):
                l_parts.append(p[:, sub * KEY_BLOCK:(sub + 1) * KEY_BLOCK])
        l_row = jnp.sum(_tree_sum(l_parts), axis=1, keepdims=True)
        out = jnp.dot(pb_sc[:, 0:nk], v_ref[0:nk, c0:c0 + HEAD_DIM],
                      preferred_element_type=F32) / l_row
        for j in range(groups):
            hd = (g * groups + j) * HEAD_DIM
            y_ref[:, hd:hd + HEAD_DIM] = out[j * tq:(j + 1) * tq].astype(y_ref.dtype)


def _attn_kernel(*refs, k_top, class_blocks, nq):
    qi = pl.program_id(1)
    for c in range(-(-nq // class_blocks)):
        nkb = min((c + 1) * class_blocks, nq)

        @pl.when(qi // class_blocks == c)
        def _(nkb=nkb):
            _attn_body(*refs, k_top=k_top, nkb=nkb)


def _attention(pb, ps, btab, *, q_width, kv_width, tq, class_blocks):
    bsz, s, _ = pb.shape
    iq_width = IDX_HEADS * IDX_DIM
    k_col = q_width
    v_col = q_width + kv_width
    iq_col = q_width + 2 * kv_width
    nkb_max = s // KEY_BLOCK
    groups = q_width // (KV_HEADS * HEAD_DIM)
    k_top = min(TOPK_MAX, s // 4)
    return pl.pallas_call(
        functools.partial(_attn_kernel, k_top=k_top, class_blocks=class_blocks, nq=s // tq),
        grid=(bsz, s // tq),
        in_specs=[pl.BlockSpec((None, tq, q_width), lambda b, i: (b, i, 0)),
                  pl.BlockSpec((None, s, kv_width), lambda b, i: (b, 0, k_col // kv_width)),
                  pl.BlockSpec((None, s, kv_width), lambda b, i: (b, 0, v_col // kv_width)),
                  pl.BlockSpec((None, tq, iq_width), lambda b, i: (b, i, iq_col // iq_width)),
                  pl.BlockSpec((None, s, LANES), lambda b, i: (b, 0, 0)),
                  pl.BlockSpec((None, tq, LANES), lambda b, i: (b, i, 0)),
                  pl.BlockSpec(btab.shape, lambda b, i: (0, 0, 0, 0))],
        out_specs=pl.BlockSpec((None, tq, q_width), lambda b, i: (b, i, 0)),
        out_shape=jax.ShapeDtypeStruct((bsz, s, q_width), BF16),
        scratch_shapes=[
            pltpu.VMEM((nkb_max, tq, KEY_BLOCK), I32),
            pltpu.VMEM((nkb_max, KEY_BLOCK // SUBLANES, SUBLANES, tq), I32),
            pltpu.VMEM((nkb_max, tq, KEY_BLOCK), F32),
            pltpu.VMEM((IDX_HEADS, tq, KEY_BLOCK), F32),
            pltpu.VMEM((groups * tq, s), F32),
            pltpu.VMEM((groups * tq, s), BF16)],
        compiler_params=_params(("parallel", "arbitrary")),
        name="sparse_attn",
    )(pb, pb, pb, pb, ps, ps, btab)


def _pad_cols(w, n):
    return jnp.pad(w, ((0, 0), (0, n - w.shape[1])))


def _pad_rows(w, n):
    return jnp.pad(w, ((0, n - w.shape[0]), (0, 0)))


def _layer(x, ffn1_norm, ffn1_w_gate, ffn1_w_up, ffn1_w_down, mix_norm, w_in, conv_w, conv_b,
           rg_w_a, rg_b_a, rg_w_x, rg_b_x, rg_lambda, btab, w_proj_rnn, w_proj_attn, w_out,
           ffn2_norm, ffn2_w_gate, ffn2_w_up, ffn2_w_down, next_norm, *, last):
    bsz, s, d = x.shape
    m = bsz * s
    width = d
    q_width = d
    kv_width = KV_HEADS * HEAD_DIM
    iq_width = IDX_HEADS * IDX_DIM
    tf = 512
    f_pad = -(-ffn1_w_gate.shape[1] // tf) * tf
    tm = min(512, m)
    tmm = min(1024, m)

    o_rx, o_rg = 0, width
    o_q = 2 * width
    o_k = o_q + q_width
    o_v = o_k + kv_width
    o_iq = o_v + kv_width
    o_ik = o_iq + iq_width
    o_iw = o_ik + IDX_DIM
    o_gr = o_iw + IDX_HEADS
    o_ga = o_gr + d

    def ffn_weights(wg, wu, wd):
        return (_pad_cols(wg, f_pad).astype(BF16), _pad_cols(wu, f_pad).astype(BF16),
                _pad_rows(wd, f_pad).astype(BF16))

    vec = lambda p: p.reshape(1, -1)

    x1, hn = _ffn(x.reshape(m, d), vec(ffn1_norm), *ffn_weights(ffn1_w_gate, ffn1_w_up, ffn1_w_down),
                  vec(mix_norm), emit_resid=True, tm=tm, tf=tf)

    w_f = jnp.concatenate([w_in[:, o_rx:o_q], w_in[:, o_gr:]], axis=1).astype(BF16)
    w_b = w_in[:, o_q:o_ik].astype(BF16)
    w_s = _pad_cols(w_in[:, o_ik:o_gr], LANES).astype(BF16)
    pf = _matmul(hn, w_f, F32, tm=tmm, tn=1024, name="in_proj_f32")
    pb = _matmul(hn, w_b, BF16, tm=tmm, tn=1024, name="in_proj_bf16")
    ps = _matmul(hn, w_s, F32, tm=tmm, tn=LANES, name="in_proj_idx")

    wax = jnp.concatenate([rg_w_a, rg_w_x], axis=-1).astype(BF16)
    y_rnn = _rglru(pf.reshape(bsz, s, -1), 0, width, conv_w, vec(conv_b), wax,
                   vec(rg_b_a), vec(rg_b_x), vec(rg_lambda),
                   width=width, ts=min(256, s), tc=512)

    y_attn = _attention(pb.reshape(bsz, s, -1), ps.reshape(bsz, s, -1), btab,
                        q_width=q_width, kv_width=kv_width, tq=KEY_BLOCK, class_blocks=4)

    merged = _merge(y_rnn.reshape(m, width), y_attn.reshape(m, q_width),
                    w_proj_rnn.astype(BF16), w_proj_attn.astype(BF16), pf,
                    2 * width, 2 * width + d, tm=tm, tn=1024)
    x2 = _matmul_residual(merged, w_out.astype(BF16), x1, tm=tmm, tn=1024)

    outs = _ffn(x2, vec(ffn2_norm), *ffn_weights(ffn2_w_gate, ffn2_w_up, ffn2_w_down),
                vec(next_norm), emit_resid=not last, tm=tm, tf=tf)
    return outs[0].reshape(bsz, s, d)


def kernel(x, ffn1_norm, ffn1_w_gate, ffn1_w_up, ffn1_w_down, mix_norm, w_in, conv_w, conv_b,
           rg_w_a, rg_b_a, rg_w_x, rg_b_x, rg_lambda, rel_bias, w_proj_rnn, w_proj_attn, w_out,
           ffn2_norm, ffn2_w_gate, ffn2_w_up, ffn2_w_down, final_norm):
    depth = ffn1_norm.shape[0]
    assert depth == 1, "the fused final norm assumes a single layer"
    btab = _bias_tiles(rel_bias)
    l = 0
    return _layer(x, ffn1_norm[l], ffn1_w_gate[l], ffn1_w_up[l], ffn1_w_down[l], mix_norm[l],
                  w_in[l], conv_w[l], conv_b[l], rg_w_a[l], rg_b_a[l], rg_w_x[l], rg_b_x[l],
                  rg_lambda[l], btab, w_proj_rnn[l], w_proj_attn[l], w_out[l], ffn2_norm[l],
                  ffn2_w_gate[l], ffn2_w_up[l], ffn2_w_down[l], final_norm, last=True)
```

```python
import functools
import math

import jax
import jax.numpy as jnp
from jax import lax
from jax.experimental import pallas as pl
from jax.experimental.pallas import tpu as pltpu

F32 = jnp.float32
BF16 = jnp.bfloat16
I32 = jnp.int32

RMS_EPS = 1e-6
CONV_WIDTH = 4
RG_C = 8.0
RNN_BLOCK = 128
HEAD_DIM = 128
KV_HEADS = 4
IDX_HEADS = 16
IDX_DIM = 64
TOPK_MAX = 256
NUM_BUCKETS = 32
MAX_DISTANCE = 128

LANES = 128
SUBLANES = 8
VMEM_LIMIT_BYTES = 56 * 1024 * 1024

KEY_BLOCK = 128
NEG_LOGIT = -1e30
INT32_MIN = -(2 ** 31)
KEY_NEG_INF = -2139095041
LOG2E = 1.4426950408889634


def _params(semantics):
    return pltpu.CompilerParams(dimension_semantics=semantics,
                                vmem_limit_bytes=VMEM_LIMIT_BYTES)


def _sigmoid(x):
    return 0.5 * (jnp.tanh(0.5 * x) + 1.0)


def _rms(x, g):
    ms = jnp.mean(x * x, axis=-1, keepdims=True)
    return x * lax.rsqrt(ms + RMS_EPS) * g


def _tree_sum(parts):
    while len(parts) > 1:
        parts = [a + b for a, b in zip(parts[::2], parts[1::2])] + (
            [parts[-1]] if len(parts) % 2 else [])
    return parts[0]


def _ffn_kernel(x_ref, g_ref, wg_ref, wu_ref, wd_ref, wgt_ref, wut_ref, wdt_ref, gn_ref, *rest,
                emit_resid, nfull):
    if emit_resid:
        acc_ref, hn_ref, xn_sc = rest
    else:
        acc_ref, xn_sc = rest
        hn_ref = acc_ref
    j = pl.program_id(1)

    @pl.when(j == 0)
    def _():
        xn_sc[...] = _rms(x_ref[...], g_ref[...]).astype(BF16)
        acc_ref[...] = jnp.zeros_like(acc_ref)

    def accumulate(wg, wu, wd):
        xn = xn_sc[...]
        h = jnp.dot(xn, wg, preferred_element_type=F32)
        u = jnp.dot(xn, wu, preferred_element_type=F32)
        a = (h * _sigmoid(h) * u).astype(BF16)
        acc_ref[...] += jnp.dot(a, wd, preferred_element_type=F32)

    @pl.when(j < nfull)
    def _():
        accumulate(wg_ref[...], wu_ref[...], wd_ref[...])

    @pl.when(j == nfull)
    def _():
        accumulate(wgt_ref[...], wut_ref[...], wdt_ref[...])
        y = x_ref[...] + 0.5 * acc_ref[...]
        if emit_resid:
            acc_ref[...] = y
        hn_ref[...] = _rms(y, gn_ref[...]).astype(hn_ref.dtype)


def _ffn(x, g, wg, wu, wd, gn, *, emit_resid, tm, tf):
    m, d = x.shape
    f = wg.shape[1]
    nfull = (f - 1) // tf
    tail = f - nfull * tf
    assert nfull >= 1 and tail % LANES == 0
    wgt, wut, wdt = wg[:, nfull * tf:], wu[:, nfull * tf:], wd[nfull * tf:, :]
    last = nfull - 1
    row = pl.BlockSpec((tm, d), lambda i, j: (i, 0))
    vec = pl.BlockSpec((1, d), lambda i, j: (0, 0))
    out_shape = [jax.ShapeDtypeStruct((m, d), F32)]
    out_specs = [row]
    if emit_resid:
        out_shape.append(jax.ShapeDtypeStruct((m, d), BF16))
        out_specs.append(row)
    return pl.pallas_call(
        functools.partial(_ffn_kernel, emit_resid=emit_resid, nfull=nfull),
        grid=(m // tm, nfull + 1),
        in_specs=[row, vec,
                  pl.BlockSpec((d, tf), lambda i, j: (0, jnp.minimum(j, last))),
                  pl.BlockSpec((d, tf), lambda i, j: (0, jnp.minimum(j, last))),
                  pl.BlockSpec((tf, d), lambda i, j: (jnp.minimum(j, last), 0)),
                  pl.BlockSpec((d, tail), lambda i, j: (0, 0)),
                  pl.BlockSpec((d, tail), lambda i, j: (0, 0)),
                  pl.BlockSpec((tail, d), lambda i, j: (0, 0)),
                  vec],
        out_specs=out_specs,
        out_shape=out_shape,
        scratch_shapes=[pltpu.VMEM((tm, d), BF16)],
        compiler_params=_params(("parallel", "arbitrary")),
        name="ffn_resid" if emit_resid else "ffn_final",
    )(x, g, wg, wu, wd, wgt, wut, wdt, gn)


def _mm_kernel(x_ref, w_ref, o_ref):
    o_ref[...] = jnp.dot(x_ref[...], w_ref[...],
                         preferred_element_type=F32).astype(o_ref.dtype)


def _matmul(x, w, out_dtype, *, tm, tn, name):
    m, k = x.shape
    n = w.shape[1]
    return pl.pallas_call(
        _mm_kernel,
        grid=(n // tn, m // tm),
        in_specs=[pl.BlockSpec((tm, k), lambda j, i: (i, 0)),
                  pl.BlockSpec((k, tn), lambda j, i: (0, j))],
        out_specs=pl.BlockSpec((tm, tn), lambda j, i: (i, j)),
        out_shape=jax.ShapeDtypeStruct((m, n), out_dtype),
        compiler_params=_params(("parallel", "parallel")),
        name=name,
    )(x, w)


def _mm_castw_kernel(x_ref, w_ref, o_ref, wb_sc):
    @pl.when(pl.program_id(1) == 0)
    def _():
        wb_sc[...] = w_ref[...].astype(BF16)

    o_ref[...] = jnp.dot(x_ref[...], wb_sc[...],
                         preferred_element_type=F32).astype(o_ref.dtype)


def _matmul_f32w(x, w, col0, n, out_dtype, *, tm, tn, name):
    m, k = x.shape
    assert col0 % tn == 0 and n % tn == 0
    return pl.pallas_call(
        _mm_castw_kernel,
        grid=(n // tn, m // tm),
        in_specs=[pl.BlockSpec((tm, k), lambda j, i: (i, 0)),
                  pl.BlockSpec((k, tn), lambda j, i: (0, col0 // tn + j))],
        out_specs=pl.BlockSpec((tm, tn), lambda j, i: (i, j)),
        out_shape=jax.ShapeDtypeStruct((m, n), out_dtype),
        scratch_shapes=[pltpu.VMEM((k, tn), BF16)],
        compiler_params=_params(("parallel", "arbitrary")),
        name=name,
    )(x, w)


def _mm_res_kernel(x_ref, w_ref, r_ref, o_ref):
    o_ref[...] = r_ref[...] + jnp.dot(x_ref[...], w_ref[...], preferred_element_type=F32)


def _matmul_residual(x, w, r, *, tm, tn):
    m, k = x.shape
    n = w.shape[1]
    return pl.pallas_call(
        _mm_res_kernel,
        grid=(n // tn, m // tm),
        in_specs=[pl.BlockSpec((tm, k), lambda j, i: (i, 0)),
                  pl.BlockSpec((k, tn), lambda j, i: (0, j)),
                  pl.BlockSpec((tm, tn), lambda j, i: (i, j))],
        out_specs=pl.BlockSpec((tm, tn), lambda j, i: (i, j)),
        out_shape=jax.ShapeDtypeStruct((m, n), F32),
        compiler_params=_params(("parallel", "parallel")),
        name="out_proj",
    )(x, w, r)


def _merge_kernel(yr_ref, ya_ref, wr_ref, wa_ref, gr_ref, ga_ref, o_ref):
    pr = jnp.dot(yr_ref[...], wr_ref[...], preferred_element_type=F32)
    pa = jnp.dot(ya_ref[...], wa_ref[...], preferred_element_type=F32)
    o_ref[...] = (_sigmoid(gr_ref[...]) * pr + _sigmoid(ga_ref[...]) * pa).astype(o_ref.dtype)


def _merge(y_rnn, y_attn, w_r, w_a, pf, gr_col, ga_col, *, tm, tn):
    m, k = y_rnn.shape
    n = w_r.shape[1]
    act = pl.BlockSpec((tm, k), lambda j, i: (i, 0))
    wsp = pl.BlockSpec((k, tn), lambda j, i: (0, j))
    return pl.pallas_call(
        _merge_kernel,
        grid=(n // tn, m // tm),
        in_specs=[act, act, wsp, wsp,
                  pl.BlockSpec((tm, tn), lambda j, i: (i, gr_col // tn + j)),
                  pl.BlockSpec((tm, tn), lambda j, i: (i, ga_col // tn + j))],
        out_specs=pl.BlockSpec((tm, tn), lambda j, i: (i, j)),
        out_shape=jax.ShapeDtypeStruct((m, n), BF16),
        compiler_params=_params(("parallel", "parallel")),
        name="merge",
    )(y_rnn, y_attn, w_r, w_a, pf, pf)


def _rglru_kernel(rx_ref, rg_ref, cw_ref, cb_ref, wax_ref, ba_ref, bx_ref, lam_ref,
                  y_ref, xs_sc, h_sc):
    ts, tc = rx_ref.shape
    ng = ts // SUBLANES

    @pl.when(pl.program_id(2) == 0)
    def _():
        xs_sc[0:SUBLANES, :] = jnp.zeros((SUBLANES, tc), F32)
        h_sc[...] = jnp.zeros_like(h_sc)

    x = rx_ref[...]
    xs_sc[SUBLANES:SUBLANES + ts, :] = x
    cw = cw_ref[...]
    xc = cb_ref[...]
    for k in range(CONV_WIDTH - 1):
        off = SUBLANES - (CONV_WIDTH - 1) + k
        xc = xc + xs_sc[off:off + ts, :] * cw[k:k + 1, :]
    xc = xc + x * cw[CONV_WIDTH - 1:CONV_WIDTH, :]
    xs_sc[0:SUBLANES, :] = x[ts - SUBLANES:ts, :]

    xcb = xc.astype(BF16)
    r_parts, i_parts = [], []
    for n in range(tc // RNN_BLOCK):
        g = jnp.dot(xcb[:, n * RNN_BLOCK:(n + 1) * RNN_BLOCK], wax_ref[n],
                    preferred_element_type=F32)
        r_parts.append(g[:, :RNN_BLOCK])
        i_parts.append(g[:, RNN_BLOCK:])
    tr = jnp.tanh(jnp.concatenate(r_parts, axis=1) + ba_ref[...])
    ti = jnp.tanh(jnp.concatenate(i_parts, axis=1) + bx_ref[...])

    nl = -lam_ref[...]
    softplus = jnp.maximum(nl, 0.0) + jnp.log1p(jnp.exp(-jnp.abs(nl)))
    log_a = (tr + 1.0) * ((-0.5 * RG_C) * softplus)
    a = jnp.exp(log_a)
    mult = jnp.sqrt(-jnp.tanh(log_a) * (a * a + 1.0))
    u = mult * ((0.5 * ti + 0.5) * xc)

    av = a.reshape(ng, SUBLANES, tc)
    bv = u.reshape(ng, SUBLANES, tc)
    ri = lax.broadcasted_iota(I32, (ng, SUBLANES, tc), 1)
    for d in (1, 2, 4):
        a_prev = jnp.where(ri >= d, pltpu.roll(av, d, axis=1), 1.0)
        b_prev = jnp.where(ri >= d, pltpu.roll(bv, d, axis=1), 0.0)
        bv = av * b_prev + bv
        av = av * a_prev
    carry = h_sc[0:1, :]
    hs = []
    for k in range(ng):
        hk = bv[k] + av[k] * carry
        hs.append(hk)
        carry = hk[SUBLANES - 1:SUBLANES, :]
    h_sc[0:1, :] = carry
    h = jnp.concatenate(hs, axis=0)

    gt = rg_ref[...]
    c1 = math.sqrt(2.0 / math.pi)
    inner = gt * (c1 + (c1 * 0.044715) * (gt * gt))
    y_ref[...] = ((h * (0.5 * gt)) * (1.0 + jnp.tanh(inner))).astype(y_ref.dtype)


def _rglru(pf, rx_col, rg_col, conv_w, conv_b, wax, b_a, b_x, lam, *, width, ts, tc):
    bsz, s, _ = pf.shape
    chan = lambda rows: pl.BlockSpec((rows, tc), lambda b, c, t: (0, c))
    return pl.pallas_call(
        _rglru_kernel,
        grid=(bsz, width // tc, s // ts),
        in_specs=[pl.BlockSpec((None, ts, tc), lambda b, c, t: (b, t, rx_col // tc + c)),
                  pl.BlockSpec((None, ts, tc), lambda b, c, t: (b, t, rg_col // tc + c)),
                  chan(CONV_WIDTH), chan(1),
                  pl.BlockSpec((tc // RNN_BLOCK, RNN_BLOCK, 2 * RNN_BLOCK),
                               lambda b, c, t: (c, 0, 0)),
                  chan(1), chan(1), chan(1)],
        out_specs=pl.BlockSpec((None, ts, tc), lambda b, c, t: (b, t, c)),
        out_shape=jax.ShapeDtypeStruct((bsz, s, width), BF16),
        scratch_shapes=[pltpu.VMEM((SUBLANES + ts, tc), F32),
                        pltpu.VMEM((SUBLANES, tc), F32)],
        compiler_params=_params(("parallel", "parallel", "arbitrary")),
        name="rglru",
    )(pf, pf, conv_w, conv_b, wax, b_a, b_x, lam)


def _bias_kernel(rb_ref, o_ref):
    h = pl.program_id(0)
    far = rb_ref[NUM_BUCKETS - 1, h]
    ii = lax.broadcasted_iota(I32, (KEY_BLOCK, KEY_BLOCK), 0)
    jj = lax.broadcasted_iota(I32, (KEY_BLOCK, KEY_BLOCK), 1)
    max_exact = NUM_BUCKETS // 2
    for d in range(2):
        n = jnp.maximum(ii - jj + KEY_BLOCK * d, 0)
        nf = jnp.maximum(n, 1).astype(F32)
        large = max_exact + (jnp.log(nf / max_exact) / math.log(MAX_DISTANCE / max_exact)
                             * (NUM_BUCKETS - max_exact)).astype(I32)
        large = jnp.minimum(large, NUM_BUCKETS - 1)
        bucket = jnp.where(n < max_exact, n, large)
        acc = jnp.zeros((KEY_BLOCK, KEY_BLOCK), F32)
        for b in range(NUM_BUCKETS):
            acc = jnp.where(bucket == b, rb_ref[b, h], acc)
        o_ref[0, d] = (acc - far) * LOG2E
    o_ref[0, 2] = jnp.zeros((KEY_BLOCK, KEY_BLOCK), F32)


def _bias_tiles(rel_bias):
    heads = rel_bias.shape[1]
    return pl.pallas_call(
        _bias_kernel,
        grid=(heads,),
        in_specs=[pl.BlockSpec(memory_space=pltpu.SMEM)],
        out_specs=pl.BlockSpec((1, 3, KEY_BLOCK, KEY_BLOCK), lambda h: (h, 0, 0, 0)),
        out_shape=jax.ShapeDtypeStruct((heads, 3, KEY_BLOCK, KEY_BLOCK), F32),
        compiler_params=_params(("parallel",)),
        name="bias_tiles",
    )(rel_bias)


SCORE_CHUNK = 4


def _attn_body(q_ref, k_ref, v_ref, iq_ref, ik_ref, iw_ref, bt_ref, y_ref,
               keys_sc, keyst_sc, am_sc, wb_sc, lg_sc, pb_sc, *, k_top, nkb, class_blocks):
    tq = q_ref.shape[0]
    groups = q_ref.shape[1] // (KV_HEADS * HEAD_DIM)
    nk = nkb * KEY_BLOCK
    qi = pl.program_id(1)
    row = qi * tq + lax.broadcasted_iota(I32, (tq, KEY_BLOCK), 0)
    lane = lax.broadcasted_iota(I32, (tq, KEY_BLOCK), 1)

    w_scale = (IDX_HEADS ** -0.5) * (IDX_DIM ** -0.5)
    iw = iw_ref[...]
    for h in range(IDX_HEADS):
        col = iw[:, IDX_DIM + h:IDX_DIM + h + 1] * w_scale
        wb_sc[h] = jnp.broadcast_to(col, (tq, KEY_BLOCK))
    iq = iq_ref[...]
    iq_rows = jnp.concatenate(
        [iq[:, h * IDX_DIM:(h + 1) * IDX_DIM] for h in range(IDX_HEADS)], axis=0)

    for c0 in range(0, nkb, SCORE_CHUNK):
        nb = min(SCORE_CHUNK, nkb - c0)
        ikc = ik_ref[c0 * KEY_BLOCK:(c0 + nb) * KEY_BLOCK, 0:IDX_DIM].astype(BF16)
        dots = lax.dot_general(iq_rows, ikc, (((1,), (1,)), ((), ())),
                               preferred_element_type=F32)
        for sub in range(nb):
            kb = c0 + sub
            acc = jnp.zeros((tq, KEY_BLOCK), F32)
            for h in range(IDX_HEADS):
                d = dots[h * tq:(h + 1) * tq, sub * KEY_BLOCK:(sub + 1) * KEY_BLOCK]
                acc = acc + jnp.maximum(d, 0.0) * wb_sc[h]
            score = jnp.where(kb * KEY_BLOCK + lane <= row, acc, -jnp.inf)
            bits = pltpu.bitcast(score, I32)
            key = bits ^ ((bits >> 31) & 0x7FFFFFFF)
            keys_sc[kb] = key
            keyst_sc[kb] = key.T.reshape(KEY_BLOCK // SUBLANES, SUBLANES, tq)

    def search(_):
        def bit_step(it, thr):
            cand = thr + lax.shift_left(jnp.int32(1), 31 - it)
            cand8 = jnp.broadcast_to(cand, (SUBLANES, tq))[None]
            parts = [jnp.sum((keyst_sc[kb] >= cand8).astype(I32), axis=0)
                     for kb in range(nkb)]
            total = jnp.sum(_tree_sum(parts), axis=0, keepdims=True)
            return jnp.where(total >= k_top, cand, thr)

        return lax.fori_loop(0, 32, bit_step, jnp.full((1, tq), INT32_MIN, I32))

    thr = lax.cond((qi + 1) * tq > k_top, search,
                   lambda _: jnp.full((1, tq), INT32_MIN, I32), 0)
    thr = jnp.maximum(thr, KEY_NEG_INF + 1)
    thr_b = jnp.broadcast_to(thr, (KEY_BLOCK, tq)).T
    for kb in range(nkb):
        am_sc[kb] = jnp.where(keys_sc[kb] >= thr_b, 0.0, NEG_LOGIT)

    scale2 = (HEAD_DIM ** -0.5) * LOG2E
    first_near = max(nkb - class_blocks - 1, 0)
    for g in range(KV_HEADS):
        c0 = g * HEAD_DIM
        qg = jnp.concatenate(
            [q_ref[:, (g * groups + j) * HEAD_DIM:(g * groups + j + 1) * HEAD_DIM]
             for j in range(groups)], axis=0)
        s = lax.dot_general(qg, k_ref[0:nk, c0:c0 + HEAD_DIM], (((1,), (1,)), ((), ())),
                            preferred_element_type=F32) * scale2
        m_parts = [[] for _ in range(groups)]
        for kb in range(nkb):
            am = am_sc[kb]
            for j in range(groups):
                sj = s[j * tq:(j + 1) * tq, kb * KEY_BLOCK:(kb + 1) * KEY_BLOCK] + am
                if kb >= first_near:
                    sj = sj + bt_ref[g * groups + j, jnp.clip(qi - kb, 0, 2)]
                lg_sc[j * tq:(j + 1) * tq, kb * KEY_BLOCK:(kb + 1) * KEY_BLOCK] = sj
                m_parts[j].append(sj)
        m_row = jnp.concatenate(
            [jnp.max(functools.reduce(jnp.maximum, mp), axis=1, keepdims=True) for mp in m_parts],
            axis=0)
        m_b = jnp.broadcast_to(m_row, (groups * tq, KEY_BLOCK))
        l_parts = []
        for kb in range(nkb):
            cols = slice(kb * KEY_BLOCK, (kb + 1) * KEY_BLOCK)
            p = jnp.exp2(lg_sc[:, cols] - m_b)
            pb_sc[:, cols] = p.astype(BF16)
            l_parts.append(p)
        l_row = jnp.sum(_tree_sum(l_parts), axis=1, keepdims=True)
        out = jnp.dot(pb_sc[:, 0:nk], v_ref[0:nk, c0:c0 + HEAD_DIM],
                      preferred_element_type=F32) / l_row
        for j in range(groups):
            hd = (g * groups + j) * HEAD_DIM
            y_ref[:, hd:hd + HEAD_DIM] = out[j * tq:(j + 1) * tq].astype(y_ref.dtype)


def _attn_kernel(*refs, k_top, class_blocks, nq):
    qi = pl.program_id(1)
    for c in range(-(-nq // class_blocks)):
        nkb = min((c + 1) * class_blocks, nq)

        @pl.when(qi // class_blocks == c)
        def _(nkb=nkb):
            _attn_body(*refs, k_top=k_top, nkb=nkb, class_blocks=class_blocks)


def _attention(pb, ps, btab, *, q_width, kv_width, tq, class_blocks):
    bsz, s, _ = pb.shape
    iq_width = IDX_HEADS * IDX_DIM
    k_col = q_width
    v_col = q_width + kv_width
    iq_col = q_width + 2 * kv_width
    nkb_max = s // KEY_BLOCK
    groups = q_width // (KV_HEADS * HEAD_DIM)
    k_top = min(TOPK_MAX, s // 4)
    assert tq == KEY_BLOCK and (s // tq) % class_blocks == 0
    return pl.pallas_call(
        functools.partial(_attn_kernel, k_top=k_top, class_blocks=class_blocks, nq=s // tq),
        grid=(bsz, s // tq),
        in_specs=[pl.BlockSpec((None, tq, q_width), lambda b, i: (b, i, 0)),
                  pl.BlockSpec((None, s, kv_width), lambda b, i: (b, 0, k_col // kv_width)),
                  pl.BlockSpec((None, s, kv_width), lambda b, i: (b, 0, v_col // kv_width)),
                  pl.BlockSpec((None, tq, iq_width), lambda b, i: (b, i, iq_col // iq_width)),
                  pl.BlockSpec((None, s, LANES), lambda b, i: (b, 0, 0)),
                  pl.BlockSpec((None, tq, LANES), lambda b, i: (b, i, 0)),
                  pl.BlockSpec(btab.shape, lambda b, i: (0, 0, 0, 0))],
        out_specs=pl.BlockSpec((None, tq, q_width), lambda b, i: (b, i, 0)),
        out_shape=jax.ShapeDtypeStruct((bsz, s, q_width), BF16),
        scratch_shapes=[
            pltpu.VMEM((nkb_max, tq, KEY_BLOCK), I32),
            pltpu.VMEM((nkb_max, KEY_BLOCK // SUBLANES, SUBLANES, tq), I32),
            pltpu.VMEM((nkb_max, tq, KEY_BLOCK), F32),
            pltpu.VMEM((IDX_HEADS, tq, KEY_BLOCK), F32),
            pltpu.VMEM((groups * tq, s), F32),
            pltpu.VMEM((groups * tq, s), BF16)],
        compiler_params=_params(("parallel", "arbitrary")),
        name="sparse_attn",
    )(pb, pb, pb, pb, ps, ps, btab)


FFN_TM, FFN_TF = 512, 512
MM_TM, MM_TN = 1024, 1024
RGLRU_TS, RGLRU_TC = 256, 512
ATTN_CLASS_BLOCKS = 2


def _pad_cols(w, n):
    return jnp.pad(w, ((0, 0), (0, n - w.shape[1])))


def _layer(x, ffn1_norm, ffn1_w_gate, ffn1_w_up, ffn1_w_down, mix_norm, w_in, conv_w, conv_b,
           rg_w_a, rg_b_a, rg_w_x, rg_b_x, rg_lambda, btab, w_proj_rnn, w_proj_attn, w_out,
           ffn2_norm, ffn2_w_gate, ffn2_w_up, ffn2_w_down, next_norm, *, last):
    bsz, s, d = x.shape
    m = bsz * s
    width = d
    q_width = d
    kv_width = KV_HEADS * HEAD_DIM
    iq_width = IDX_HEADS * IDX_DIM
    tm = min(FFN_TM, m)
    tmm = min(MM_TM, m)

    o_q = 2 * width
    o_ik = o_q + q_width + 2 * kv_width + iq_width
    o_gr = o_ik + IDX_DIM + IDX_HEADS

    bf = lambda w: w.astype(BF16)
    vec = lambda p: p.reshape(1, -1)

    x1, hn = _ffn(x.reshape(m, d), vec(ffn1_norm), bf(ffn1_w_gate), bf(ffn1_w_up), bf(ffn1_w_down),
                  vec(mix_norm), emit_resid=True, tm=tm, tf=FFN_TF)

    pf = _matmul_f32w(hn, w_in, 0, o_q, F32, tm=tmm, tn=MM_TN, name="in_proj_rnn")
    pb = _matmul_f32w(hn, w_in, o_q, o_ik - o_q, BF16, tm=tmm, tn=MM_TN, name="in_proj_attn")
    pg = _matmul(hn, bf(w_in[:, o_gr:]), F32, tm=tmm, tn=MM_TN, name="in_proj_gates")
    ps = _matmul(hn, bf(_pad_cols(w_in[:, o_ik:o_gr], LANES)), F32, tm=tmm, tn=LANES,
                 name="in_proj_idx")

    wax = bf(0.5 * jnp.concatenate([rg_w_a, rg_w_x], axis=-1))
    y_rnn = _rglru(pf.reshape(bsz, s, -1), 0, width, conv_w, vec(conv_b), wax,
                   vec(0.5 * rg_b_a), vec(0.5 * rg_b_x), vec(rg_lambda),
                   width=width, ts=min(RGLRU_TS, s), tc=RGLRU_TC)

    y_attn = _attention(pb.reshape(bsz, s, -1), ps.reshape(bsz, s, -1), btab,
                        q_width=q_width, kv_width=kv_width, tq=KEY_BLOCK,
                        class_blocks=ATTN_CLASS_BLOCKS)

    merged = _merge(y_rnn.reshape(m, width), y_attn.reshape(m, q_width),
                    bf(w_proj_rnn), bf(w_proj_attn), pg, 0, d, tm=tm, tn=MM_TN)
    x2 = _matmul_residual(merged, bf(w_out), x1, tm=tmm, tn=MM_TN)

    outs = _ffn(x2, vec(ffn2_norm), bf(ffn2_w_gate), bf(ffn2_w_up), bf(ffn2_w_down),
                vec(next_norm), emit_resid=not last, tm=tm, tf=FFN_TF)
    return outs[0].reshape(bsz, s, d)


def kernel(x, ffn1_norm, ffn1_w_gate, ffn1_w_up, ffn1_w_down, mix_norm, w_in, conv_w, conv_b,
           rg_w_a, rg_b_a, rg_w_x, rg_b_x, rg_lambda, rel_bias, w_proj_rnn, w_proj_attn, w_out,
           ffn2_norm, ffn2_w_gate, ffn2_w_up, ffn2_w_down, final_norm):
    depth = ffn1_norm.shape[0]
    assert depth == 1, "the fused final norm assumes a single layer"
    btab = _bias_tiles(rel_bias)
    l = 0
    return _layer(x, ffn1_norm[l], ffn1_w_gate[l], ffn1_w_up[l], ffn1_w_down[l], mix_norm[l],
                  w_in[l], conv_w[l], conv_b[l], rg_w_a[l], rg_b_a[l], rg_w_x[l], rg_b_x[l],
                  rg_lambda[l], btab, w_proj_rnn[l], w_proj_attn[l], w_out[l], ffn2_norm[l],
                  ffn2_w_gate[l], ffn2_w_up[l], ffn2_w_down[l], final_norm, last=True)
```

```python
import functools
import math

import jax
import jax.numpy as jnp
from jax import lax
from jax.experimental import pallas as pl
from jax.experimental.pallas import tpu as pltpu

F32 = jnp.float32
BF16 = jnp.bfloat16
I32 = jnp.int32

RMS_EPS = 1e-6
CONV_WIDTH = 4
RG_C = 8.0
RNN_BLOCK = 128
HEAD_DIM = 128
KV_HEADS = 4
IDX_HEADS = 16
IDX_DIM = 64
TOPK_MAX = 256
NUM_BUCKETS = 32
MAX_DISTANCE = 128

LANES = 128
SUBLANES = 8
VMEM_LIMIT_BYTES = 56 * 1024 * 1024

KEY_BLOCK = 128
NEG_LOGIT = -1e30
INT32_MIN = -(2 ** 31)
KEY_NEG_INF = -2139095041
LOG2E = 1.4426950408889634


def _params(semantics):
    return pltpu.CompilerParams(dimension_semantics=semantics,
                                vmem_limit_bytes=VMEM_LIMIT_BYTES)


def _sigmoid(x):
    return 0.5 * (jnp.tanh(0.5 * x) + 1.0)


def _rms(x, g):
    ms = jnp.mean(x * x, axis=-1, keepdims=True)
    return x * lax.rsqrt(ms + RMS_EPS) * g


def _tree_sum(parts):
    while len(parts) > 1:
        parts = [a + b for a, b in zip(parts[::2], parts[1::2])] + (
            [parts[-1]] if len(parts) % 2 else [])
    return parts[0]


def _ffn_kernel(x_ref, g_ref, wg_ref, wu_ref, wd_ref, wgt_ref, wut_ref, wdt_ref, gn_ref, *rest,
                emit_resid, nfull):
    if emit_resid:
        acc_ref, hn_ref, xn_sc = rest
    else:
        acc_ref, xn_sc = rest
        hn_ref = acc_ref
    j = pl.program_id(1)

    @pl.when(j == 0)
    def _():
        xn_sc[...] = _rms(x_ref[...], g_ref[...]).astype(BF16)
        acc_ref[...] = jnp.zeros_like(acc_ref)

    def accumulate(wg, wu, wd):
        xn = xn_sc[...]
        h = jnp.dot(xn, wg, preferred_element_type=F32)
        u = jnp.dot(xn, wu, preferred_element_type=F32)
        a = (h * _sigmoid(h) * u).astype(BF16)
        acc_ref[...] += jnp.dot(a, wd, preferred_element_type=F32)

    @pl.when(j < nfull)
    def _():
        accumulate(wg_ref[...], wu_ref[...], wd_ref[...])

    @pl.when(j == nfull)
    def _():
        accumulate(wgt_ref[...], wut_ref[...], wdt_ref[...])
        y = x_ref[...] + 0.5 * acc_ref[...]
        if emit_resid:
            acc_ref[...] = y
        hn_ref[...] = _rms(y, gn_ref[...]).astype(hn_ref.dtype)


def _ffn(x, g, wg, wu, wd, gn, *, emit_resid, tm, tf):
    m, d = x.shape
    f = wg.shape[1]
    nfull = (f - 1) // tf
    tail = f - nfull * tf
    assert nfull >= 1 and tail % LANES == 0
    wgt, wut, wdt = wg[:, nfull * tf:], wu[:, nfull * tf:], wd[nfull * tf:, :]
    last = nfull - 1
    row = pl.BlockSpec((tm, d), lambda i, j: (i, 0))
    vec = pl.BlockSpec((1, d), lambda i, j: (0, 0))
    out_shape = [jax.ShapeDtypeStruct((m, d), F32)]
    out_specs = [row]
    if emit_resid:
        out_shape.append(jax.ShapeDtypeStruct((m, d), BF16))
        out_specs.append(row)
    return pl.pallas_call(
        functools.partial(_ffn_kernel, emit_resid=emit_resid, nfull=nfull),
        grid=(m // tm, nfull + 1),
        in_specs=[row, vec,
                  pl.BlockSpec((d, tf), lambda i, j: (0, jnp.minimum(j, last))),
                  pl.BlockSpec((d, tf), lambda i, j: (0, jnp.minimum(j, last))),
                  pl.BlockSpec((tf, d), lambda i, j: (jnp.minimum(j, last), 0)),
                  pl.BlockSpec((d, tail), lambda i, j: (0, 0)),
                  pl.BlockSpec((d, tail), lambda i, j: (0, 0)),
                  pl.BlockSpec((tail, d), lambda i, j: (0, 0)),
                  vec],
        out_specs=out_specs,
        out_shape=out_shape,
        scratch_shapes=[pltpu.VMEM((tm, d), BF16)],
        compiler_params=_params(("parallel", "arbitrary")),
        name="ffn_resid" if emit_resid else "ffn_final",
    )(x, g, wg, wu, wd, wgt, wut, wdt, gn)


def _mm_kernel(x_ref, w_ref, o_ref):
    o_ref[...] = jnp.dot(x_ref[...], w_ref[...],
                         preferred_element_type=F32).astype(o_ref.dtype)


def _matmul(x, w, out_dtype, *, tm, tn, name):
    m, k = x.shape
    n = w.shape[1]
    return pl.pallas_call(
        _mm_kernel,
        grid=(n // tn, m // tm),
        in_specs=[pl.BlockSpec((tm, k), lambda j, i: (i, 0)),
                  pl.BlockSpec((k, tn), lambda j, i: (0, j))],
        out_specs=pl.BlockSpec((tm, tn), lambda j, i: (i, j)),
        out_shape=jax.ShapeDtypeStruct((m, n), out_dtype),
        compiler_params=_params(("parallel", "parallel")),
        name=name,
    )(x, w)


def _mm_castw_kernel(x_ref, w_ref, o_ref, wb_sc):
    @pl.when(pl.program_id(1) == 0)
    def _():
        wb_sc[...] = w_ref[...].astype(BF16)

    o_ref[...] = jnp.dot(x_ref[...], wb_sc[...],
                         preferred_element_type=F32).astype(o_ref.dtype)


def _matmul_f32w(x, w, col0, n, out_dtype, *, tm, tn, name):
    m, k = x.shape
    assert col0 % tn == 0 and n % tn == 0
    return pl.pallas_call(
        _mm_castw_kernel,
        grid=(n // tn, m // tm),
        in_specs=[pl.BlockSpec((tm, k), lambda j, i: (i, 0)),
                  pl.BlockSpec((k, tn), lambda j, i: (0, col0 // tn + j))],
        out_specs=pl.BlockSpec((tm, tn), lambda j, i: (i, j)),
        out_shape=jax.ShapeDtypeStruct((m, n), out_dtype),
        scratch_shapes=[pltpu.VMEM((k, tn), BF16)],
        compiler_params=_params(("parallel", "arbitrary")),
        name=name,
    )(x, w)


def _mm_res_kernel(x_ref, w_ref, r_ref, o_ref):
    o_ref[...] = r_ref[...] + jnp.dot(x_ref[...], w_ref[...], preferred_element_type=F32)


def _matmul_residual(x, w, r, *, tm, tn):
    m, k = x.shape
    n = w.shape[1]
    return pl.pallas_call(
        _mm_res_kernel,
        grid=(n // tn, m // tm),
        in_specs=[pl.BlockSpec((tm, k), lambda j, i: (i, 0)),
                  pl.BlockSpec((k, tn), lambda j, i: (0, j)),
                  pl.BlockSpec((tm, tn), lambda j, i: (i, j))],
        out_specs=pl.BlockSpec((tm, tn), lambda j, i: (i, j)),
        out_shape=jax.ShapeDtypeStruct((m, n), F32),
        compiler_params=_params(("parallel", "parallel")),
        name="out_proj",
    )(x, w, r)


def _merge_kernel(yr_ref, ya_ref, wr_ref, wa_ref, gr_ref, ga_ref, o_ref):
    pr = jnp.dot(yr_ref[...], wr_ref[...], preferred_element_type=F32)
    pa = jnp.dot(ya_ref[...], wa_ref[...], preferred_element_type=F32)
    o_ref[...] = (_sigmoid(gr_ref[...]) * pr + _sigmoid(ga_ref[...]) * pa).astype(o_ref.dtype)


def _merge(y_rnn, y_attn, w_r, w_a, pf, gr_col, ga_col, *, tm, tn):
    m, k = y_rnn.shape
    n = w_r.shape[1]
    act = pl.BlockSpec((tm, k), lambda j, i: (i, 0))
    wsp = pl.BlockSpec((k, tn), lambda j, i: (0, j))
    return pl.pallas_call(
        _merge_kernel,
        grid=(n // tn, m // tm),
        in_specs=[act, act, wsp, wsp,
                  pl.BlockSpec((tm, tn), lambda j, i: (i, gr_col // tn + j)),
                  pl.BlockSpec((tm, tn), lambda j, i: (i, ga_col // tn + j))],
        out_specs=pl.BlockSpec((tm, tn), lambda j, i: (i, j)),
        out_shape=jax.ShapeDtypeStruct((m, n), BF16),
        compiler_params=_params(("parallel", "parallel")),
        name="merge",
    )(y_rnn, y_attn, w_r, w_a, pf, pf)


def _rglru_kernel(rx_ref, rg_ref, cw_ref, cb_ref, wax_ref, ba_ref, bx_ref, lam_ref,
                  y_ref, xs_sc, h_sc):
    ts, tc = rx_ref.shape
    ng = ts // SUBLANES

    @pl.when(pl.program_id(2) == 0)
    def _():
        xs_sc[0:SUBLANES, :] = jnp.zeros((SUBLANES, tc), F32)
        h_sc[...] = jnp.zeros_like(h_sc)

    x = rx_ref[...]
    xs_sc[SUBLANES:SUBLANES + ts, :] = x
    cw = cw_ref[...]
    xc = cb_ref[...]
    for k in range(CONV_WIDTH - 1):
        off = SUBLANES - (CONV_WIDTH - 1) + k
        xc = xc + xs_sc[off:off + ts, :] * cw[k:k + 1, :]
    xc = xc + x * cw[CONV_WIDTH - 1:CONV_WIDTH, :]
    xs_sc[0:SUBLANES, :] = x[ts - SUBLANES:ts, :]

    xcb = xc.astype(BF16)
    r_parts, i_parts = [], []
    for n in range(tc // RNN_BLOCK):
        g = jnp.dot(xcb[:, n * RNN_BLOCK:(n + 1) * RNN_BLOCK], wax_ref[n],
                    preferred_element_type=F32)
        r_parts.append(g[:, :RNN_BLOCK])
        i_parts.append(g[:, RNN_BLOCK:])
    tr = jnp.tanh(jnp.concatenate(r_parts, axis=1) + ba_ref[...])
    ti = jnp.tanh(jnp.concatenate(i_parts, axis=1) + bx_ref[...])

    nl = -lam_ref[...]
    softplus = jnp.maximum(nl, 0.0) + jnp.log1p(jnp.exp(-jnp.abs(nl)))
    log_a = (tr + 1.0) * ((-0.5 * RG_C) * softplus)
    a = jnp.exp(log_a)
    mult = jnp.sqrt(-jnp.tanh(log_a) * (a * a + 1.0))
    u = mult * ((0.5 * ti + 0.5) * xc)

    av = a.reshape(ng, SUBLANES, tc)
    bv = u.reshape(ng, SUBLANES, tc)
    ri = lax.broadcasted_iota(I32, (ng, SUBLANES, tc), 1)
    for d in (1, 2, 4):
        a_prev = jnp.where(ri >= d, pltpu.roll(av, d, axis=1), 1.0)
        b_prev = jnp.where(ri >= d, pltpu.roll(bv, d, axis=1), 0.0)
        bv = av * b_prev + bv
        av = av * a_prev
    carry = h_sc[0:1, :]
    hs = []
    for k in range(ng):
        hk = bv[k] + av[k] * carry
        hs.append(hk)
        carry = hk[SUBLANES - 1:SUBLANES, :]
    h_sc[0:1, :] = carry
    h = jnp.concatenate(hs, axis=0)

    gt = rg_ref[...]
    c1 = math.sqrt(2.0 / math.pi)
    inner = gt * (c1 + (c1 * 0.044715) * (gt * gt))
    y_ref[...] = ((h * (0.5 * gt)) * (1.0 + jnp.tanh(inner))).astype(y_ref.dtype)


def _rglru(pf, rx_col, rg_col, conv_w, conv_b, wax, b_a, b_x, lam, *, width, ts, tc):
    bsz, s, _ = pf.shape
    chan = lambda rows: pl.BlockSpec((rows, tc), lambda b, c, t: (0, c))
    return pl.pallas_call(
        _rglru_kernel,
        grid=(bsz, width // tc, s // ts),
        in_specs=[pl.BlockSpec((None, ts, tc), lambda b, c, t: (b, t, rx_col // tc + c)),
                  pl.BlockSpec((None, ts, tc), lambda b, c, t: (b, t, rg_col // tc + c)),
                  chan(CONV_WIDTH), chan(1),
                  pl.BlockSpec((tc // RNN_BLOCK, RNN_BLOCK, 2 * RNN_BLOCK),
                               lambda b, c, t: (c, 0, 0)),
                  chan(1), chan(1), chan(1)],
        out_specs=pl.BlockSpec((None, ts, tc), lambda b, c, t: (b, t, c)),
        out_shape=jax.ShapeDtypeStruct((bsz, s, width), BF16),
        scratch_shapes=[pltpu.VMEM((SUBLANES + ts, tc), F32),
                        pltpu.VMEM((SUBLANES, tc), F32)],
        compiler_params=_params(("parallel", "parallel", "arbitrary")),
        name="rglru",
    )(pf, pf, conv_w, conv_b, wax, b_a, b_x, lam)


def _bias_kernel(rb_ref, o_ref):
    h = pl.program_id(0)
    far = rb_ref[NUM_BUCKETS - 1, h]
    ii = lax.broadcasted_iota(I32, (KEY_BLOCK, KEY_BLOCK), 0)
    jj = lax.broadcasted_iota(I32, (KEY_BLOCK, KEY_BLOCK), 1)
    max_exact = NUM_BUCKETS // 2
    for d in range(2):
        n = jnp.maximum(ii - jj + KEY_BLOCK * d, 0)
        nf = jnp.maximum(n, 1).astype(F32)
        large = max_exact + (jnp.log(nf / max_exact) / math.log(MAX_DISTANCE / max_exact)
                             * (NUM_BUCKETS - max_exact)).astype(I32)
        large = jnp.minimum(large, NUM_BUCKETS - 1)
        bucket = jnp.where(n < max_exact, n, large)
        acc = jnp.zeros((KEY_BLOCK, KEY_BLOCK), F32)
        for b in range(NUM_BUCKETS):
            acc = jnp.where(bucket == b, rb_ref[b, h], acc)
        o_ref[0, d] = (acc - far) * LOG2E
    o_ref[0, 2] = jnp.zeros((KEY_BLOCK, KEY_BLOCK), F32)


def _bias_tiles(rel_bias):
    heads = rel_bias.shape[1]
    return pl.pallas_call(
        _bias_kernel,
        grid=(heads,),
        in_specs=[pl.BlockSpec(memory_space=pltpu.SMEM)],
        out_specs=pl.BlockSpec((1, 3, KEY_BLOCK, KEY_BLOCK), lambda h: (h, 0, 0, 0)),
        out_shape=jax.ShapeDtypeStruct((heads, 3, KEY_BLOCK, KEY_BLOCK), F32),
        compiler_params=_params(("parallel",)),
        name="bias_tiles",
    )(rel_bias)


SCORE_CHUNK = 4


def _attn_body(q_ref, k_ref, v_ref, iq_ref, ik_ref, iw_ref, bt_ref, y_ref,
               keys_sc, keyst_sc, am_sc, wb_sc, lg_sc, pb_sc, *, k_top, nkb, class_blocks):
    tq = q_ref.shape[0]
    groups = q_ref.shape[1] // (KV_HEADS * HEAD_DIM)
    nk = nkb * KEY_BLOCK
    qi = pl.program_id(1)
    row = qi * tq + lax.broadcasted_iota(I32, (tq, KEY_BLOCK), 0)
    lane = lax.broadcasted_iota(I32, (tq, KEY_BLOCK), 1)

    w_scale = (IDX_HEADS ** -0.5) * (IDX_DIM ** -0.5)
    iw = iw_ref[...]
    for h in range(IDX_HEADS):
        col = iw[:, IDX_DIM + h:IDX_DIM + h + 1] * w_scale
        wb_sc[h] = jnp.broadcast_to(col, (tq, KEY_BLOCK))
    iq = iq_ref[...]
    iq_rows = jnp.concatenate(
        [iq[:, h * IDX_DIM:(h + 1) * IDX_DIM] for h in range(IDX_HEADS)], axis=0)

    for c0 in range(0, nkb, SCORE_CHUNK):
        nb = min(SCORE_CHUNK, nkb - c0)
        ikc = ik_ref[c0 * KEY_BLOCK:(c0 + nb) * KEY_BLOCK, 0:IDX_DIM].astype(BF16)
        dots = lax.dot_general(iq_rows, ikc, (((1,), (1,)), ((), ())),
                               preferred_element_type=F32)
        for sub in range(nb):
            kb = c0 + sub
            acc = jnp.zeros((tq, KEY_BLOCK), F32)
            for h in range(IDX_HEADS):
                d = dots[h * tq:(h + 1) * tq, sub * KEY_BLOCK:(sub + 1) * KEY_BLOCK]
                acc = acc + jnp.maximum(d, 0.0) * wb_sc[h]
            score = jnp.where(kb * KEY_BLOCK + lane <= row, acc, -jnp.inf)
            bits = pltpu.bitcast(score, I32)
            key = bits ^ ((bits >> 31) & 0x7FFFFFFF)
            keys_sc[kb] = key
            keyst_sc[kb] = key.T.reshape(KEY_BLOCK // SUBLANES, SUBLANES, tq)

    def search(_):
        def bit_step(it, thr):
            cand = thr + lax.shift_left(jnp.int32(1), 31 - it)
            cand8 = jnp.broadcast_to(cand, (SUBLANES, tq))[None]
            parts = [jnp.sum((keyst_sc[kb] >= cand8).astype(I32), axis=0)
                     for kb in range(nkb)]
            total = jnp.sum(_tree_sum(parts), axis=0, keepdims=True)
            return jnp.where(total >= k_top, cand, thr)

        return lax.fori_loop(0, 32, bit_step, jnp.full((1, tq), INT32_MIN, I32))

    thr = lax.cond((qi + 1) * tq > k_top, search,
                   lambda _: jnp.full((1, tq), INT32_MIN, I32), 0)
    thr = jnp.maximum(thr, KEY_NEG_INF + 1)
    thr_b = jnp.broadcast_to(thr, (KEY_BLOCK, tq)).T
    for kb in range(nkb):
        am_sc[kb] = jnp.where(keys_sc[kb] >= thr_b, 0.0, NEG_LOGIT)

    scale2 = (HEAD_DIM ** -0.5) * LOG2E
    first_near = max(nkb - class_blocks - 1, 0)
    for g in range(KV_HEADS):
        c0 = g * HEAD_DIM
        qg = jnp.concatenate(
            [q_ref[:, (g * groups + j) * HEAD_DIM:(g * groups + j + 1) * HEAD_DIM]
             for j in range(groups)], axis=0)
        s = lax.dot_general(qg, k_ref[0:nk, c0:c0 + HEAD_DIM], (((1,), (1,)), ((), ())),
                            preferred_element_type=F32) * scale2
        m_parts = [[] for _ in range(groups)]
        for kb in range(nkb):
            am = am_sc[kb]
            for j in range(groups):
                sj = s[j * tq:(j + 1) * tq, kb * KEY_BLOCK:(kb + 1) * KEY_BLOCK] + am
                if kb >= first_near:
                    sj = sj + bt_ref[g * groups + j, jnp.clip(qi - kb, 0, 2)]
                lg_sc[j * tq:(j + 1) * tq, kb * KEY_BLOCK:(kb + 1) * KEY_BLOCK] = sj
                m_parts[j].append(sj)
        m_row = jnp.concatenate(
            [jnp.max(functools.reduce(jnp.maximum, mp), axis=1, keepdims=True) for mp in m_parts],
            axis=0)
        m_b = jnp.broadcast_to(m_row, (groups * tq, KEY_BLOCK))
        l_parts = []
        for kb in range(nkb):
            cols = slice(kb * KEY_BLOCK, (kb + 1) * KEY_BLOCK)
            p = jnp.exp2(lg_sc[:, cols] - m_b)
            pb_sc[:, cols] = p.astype(BF16)
            l_parts.append(p)
        l_row = jnp.sum(_tree_sum(l_parts), axis=1, keepdims=True)
        out = jnp.dot(pb_sc[:, 0:nk], v_ref[0:nk, c0:c0 + HEAD_DIM],
                      preferred_element_type=F32) / l_row
        for j in range(groups):
            hd = (g * groups + j) * HEAD_DIM
            y_ref[:, hd:hd + HEAD_DIM] = out[j * tq:(j + 1) * tq].astype(y_ref.dtype)


def _attn_kernel(*refs, k_top, class_blocks, nq):
    qi = pl.program_id(1)
    for c in range(-(-nq // class_blocks)):
        nkb = min((c + 1) * class_blocks, nq)

        @pl.when(qi // class_blocks == c)
        def _(nkb=nkb):
            _attn_body(*refs, k_top=k_top, nkb=nkb, class_blocks=class_blocks)


def _attention(pb, ps, btab, *, q_width, kv_width, tq, class_blocks):
    bsz, s, _ = pb.shape
    iq_width = IDX_HEADS * IDX_DIM
    k_col = q_width
    v_col = q_width + kv_width
    iq_col = q_width + 2 * kv_width
    nkb_max = s // KEY_BLOCK
    groups = q_width // (KV_HEADS * HEAD_DIM)
    k_top = min(TOPK_MAX, s // 4)
    assert tq == KEY_BLOCK and (s // tq) % class_blocks == 0
    return pl.pallas_call(
        functools.partial(_attn_kernel, k_top=k_top, class_blocks=class_blocks, nq=s // tq),
        grid=(bsz, s // tq),
        in_specs=[pl.BlockSpec((None, tq, q_width), lambda b, i: (b, i, 0)),
                  pl.BlockSpec((None, s, kv_width), lambda b, i: (b, 0, k_col // kv_width)),
                  pl.BlockSpec((None, s, kv_width), lambda b, i: (b, 0, v_col // kv_width)),
                  pl.BlockSpec((None, tq, iq_width), lambda b, i: (b, i, iq_col // iq_width)),
                  pl.BlockSpec((None, s, LANES), lambda b, i: (b, 0, 0)),
                  pl.BlockSpec((None, tq, LANES), lambda b, i: (b, i, 0)),
                  pl.BlockSpec(btab.shape, lambda b, i: (0, 0, 0, 0))],
        out_specs=pl.BlockSpec((None, tq, q_width), lambda b, i: (b, i, 0)),
        out_shape=jax.ShapeDtypeStruct((bsz, s, q_width), BF16),
        scratch_shapes=[
            pltpu.VMEM((nkb_max, tq, KEY_BLOCK), I32),
            pltpu.VMEM((nkb_max, KEY_BLOCK // SUBLANES, SUBLANES, tq), I32),
            pltpu.VMEM((nkb_max, tq, KEY_BLOCK), F32),
            pltpu.VMEM((IDX_HEADS, tq, KEY_BLOCK), F32),
            pltpu.VMEM((groups * tq, s), F32),
            pltpu.VMEM((groups * tq, s), BF16)],
        compiler_params=_params(("parallel", "arbitrary")),
        name="sparse_attn",
    )(pb, pb, pb, pb, ps, ps, btab)


FFN_TM, FFN_TF = 512, 512
MM_TM, MM_TN = 1024, 1024
RGLRU_TS, RGLRU_TC = 256, 512
ATTN_CLASS_BLOCKS = 4


def _pad_cols(w, n):
    return jnp.pad(w, ((0, 0), (0, n - w.shape[1])))


def _layer(x, ffn1_norm, ffn1_w_gate, ffn1_w_up, ffn1_w_down, mix_norm, w_in, conv_w, conv_b,
           rg_w_a, rg_b_a, rg_w_x, rg_b_x, rg_lambda, btab, w_proj_rnn, w_proj_attn, w_out,
           ffn2_norm, ffn2_w_gate, ffn2_w_up, ffn2_w_down, next_norm, *, last):
    bsz, s, d = x.shape
    m = bsz * s
    width = d
    q_width = d
    kv_width = KV_HEADS * HEAD_DIM
    iq_width = IDX_HEADS * IDX_DIM
    tm = min(FFN_TM, m)
    tmm = min(MM_TM, m)

    o_q = 2 * width
    o_ik = o_q + q_width + 2 * kv_width + iq_width
    o_gr = o_ik + IDX_DIM + IDX_HEADS

    bf = lambda w: w.astype(BF16)
    vec = lambda p: p.reshape(1, -1)

    x1, hn = _ffn(x.reshape(m, d), vec(ffn1_norm), bf(ffn1_w_gate), bf(ffn1_w_up), bf(ffn1_w_down),
                  vec(mix_norm), emit_resid=True, tm=tm, tf=FFN_TF)

    pf = _matmul_f32w(hn, w_in, 0, o_q, F32, tm=tmm, tn=MM_TN, name="in_proj_rnn")
    pb = _matmul_f32w(hn, w_in, o_q, o_ik - o_q, BF16, tm=tmm, tn=MM_TN, name="in_proj_attn")
    pg = _matmul(hn, bf(w_in[:, o_gr:]), F32, tm=tmm, tn=MM_TN, name="in_proj_gates")
    ps = _matmul(hn, bf(_pad_cols(w_in[:, o_ik:o_gr], LANES)), F32, tm=tmm, tn=LANES,
                 name="in_proj_idx")

    wax = bf(0.5 * jnp.concatenate([rg_w_a, rg_w_x], axis=-1))
    y_rnn = _rglru(pf.reshape(bsz, s, -1), 0, width, conv_w, vec(conv_b), wax,
                   vec(0.5 * rg_b_a), vec(0.5 * rg_b_x), vec(rg_lambda),
                   width=width, ts=min(RGLRU_TS, s), tc=RGLRU_TC)

    y_attn = _attention(pb.reshape(bsz, s, -1), ps.reshape(bsz, s, -1), btab,
                        q_width=q_width, kv_width=kv_width, tq=KEY_BLOCK,
                        class_blocks=ATTN_CLASS_BLOCKS)

    merged = _merge(y_rnn.reshape(m, width), y_attn.reshape(m, q_width),
                    bf(w_proj_rnn), bf(w_proj_attn), pg, 0, d, tm=tm, tn=MM_TN)
    x2 = _matmul_residual(merged, bf(w_out), x1, tm=tmm, tn=MM_TN)

    outs = _ffn(x2, vec(ffn2_norm), bf(ffn2_w_gate), bf(ffn2_w_up), bf(ffn2_w_down),
                vec(next_norm), emit_resid=not last, tm=tm, tf=FFN_TF)
    return outs[0].reshape(bsz, s, d)


def kernel(x, ffn1_norm, ffn1_w_gate, ffn1_w_up, ffn1_w_down, mix_norm, w_in, conv_w, conv_b,
           rg_w_a, rg_b_a, rg_w_x, rg_b_x, rg_lambda, rel_bias, w_proj_rnn, w_proj_attn, w_out,
           ffn2_norm, ffn2_w_gate, ffn2_w_up, ffn2_w_down, final_norm):
    depth = ffn1_norm.shape[0]
    assert depth == 1, "the fused final norm assumes a single layer"
    btab = _bias_tiles(rel_bias)
    l = 0
    return _layer(x, ffn1_norm[l], ffn1_w_gate[l], ffn1_w_up[l], ffn1_w_down[l], mix_norm[l],
                  w_in[l], conv_w[l], conv_b[l], rg_w_a[l], rg_b_a[l], rg_w_x[l], rg_b_x[l],
                  rg_lambda[l], btab, w_proj_rnn[l], w_proj_attn[l], w_out[l], ffn2_norm[l],
                  ffn2_w_gate[l], ffn2_w_up[l], ffn2_w_down[l], final_norm, last=True)
```

```python
import functools
import math

import jax
import jax.numpy as jnp
from jax import lax
from jax.experimental import pallas as pl
from jax.experimental.pallas import tpu as pltpu

F32 = jnp.float32
BF16 = jnp.bfloat16
I32 = jnp.int32

RMS_EPS = 1e-6
CONV_WIDTH = 4
RG_C = 8.0
RNN_BLOCK = 128
HEAD_DIM = 128
KV_HEADS = 4
IDX_HEADS = 16
IDX_DIM = 64
TOPK_MAX = 256
NUM_BUCKETS = 32
MAX_DISTANCE = 128

LANES = 128
SUBLANES = 8
VMEM_LIMIT_BYTES = 56 * 1024 * 1024

KEY_BLOCK = 128
NEG_LOGIT = -1e30
INT32_MIN = -(2 ** 31)
KEY_NEG_INF = -2139095041
LOG2E = 1.4426950408889634


def _params(semantics):
    return pltpu.CompilerParams(dimension_semantics=semantics,
                                vmem_limit_bytes=VMEM_LIMIT_BYTES)


def _sigmoid(x):
    return 0.5 * (jnp.tanh(0.5 * x) + 1.0)


def _rms(x, g):
    ms = jnp.mean(x * x, axis=-1, keepdims=True)
    return x * lax.rsqrt(ms + RMS_EPS) * g


def _tree_sum(parts):
    while len(parts) > 1:
        parts = [a + b for a, b in zip(parts[::2], parts[1::2])] + (
            [parts[-1]] if len(parts) % 2 else [])
    return parts[0]


def _ffn_kernel(x_ref, g_ref, wg_ref, wu_ref, wd_ref, wgt_ref, wut_ref, wdt_ref, gn_ref, *rest,
                emit_resid, nfull):
    if emit_resid:
        acc_ref, hn_ref, xn_sc = rest
    else:
        acc_ref, xn_sc = rest
        hn_ref = acc_ref
    j = pl.program_id(1)

    @pl.when(j == 0)
    def _():
        xn_sc[...] = _rms(x_ref[...], g_ref[...]).astype(BF16)
        acc_ref[...] = jnp.zeros_like(acc_ref)

    def accumulate(wg, wu, wd):
        xn = xn_sc[...]
        h = jnp.dot(xn, wg, preferred_element_type=F32)
        u = jnp.dot(xn, wu, preferred_element_type=F32)
        a = (h * _sigmoid(h) * u).astype(BF16)
        acc_ref[...] += jnp.dot(a, wd, preferred_element_type=F32)

    @pl.when(j < nfull)
    def _():
        accumulate(wg_ref[...], wu_ref[...], wd_ref[...])

    @pl.when(j == nfull)
    def _():
        accumulate(wgt_ref[...], wut_ref[...], wdt_ref[...])
        y = x_ref[...] + 0.5 * acc_ref[...]
        if emit_resid:
            acc_ref[...] = y
        hn_ref[...] = _rms(y, gn_ref[...]).astype(hn_ref.dtype)


def _ffn(x, g, wg, wu, wd, gn, *, emit_resid, tm, tf):
    m, d = x.shape
    f = wg.shape[1]
    nfull = (f - 1) // tf
    tail = f - nfull * tf
    assert nfull >= 1 and tail % LANES == 0
    wgt, wut, wdt = wg[:, nfull * tf:], wu[:, nfull * tf:], wd[nfull * tf:, :]
    last = nfull - 1
    row = pl.BlockSpec((tm, d), lambda i, j: (i, 0))
    vec = pl.BlockSpec((1, d), lambda i, j: (0, 0))
    out_shape = [jax.ShapeDtypeStruct((m, d), F32)]
    out_specs = [row]
    if emit_resid:
        out_shape.append(jax.ShapeDtypeStruct((m, d), BF16))
        out_specs.append(row)
    return pl.pallas_call(
        functools.partial(_ffn_kernel, emit_resid=emit_resid, nfull=nfull),
        grid=(m // tm, nfull + 1),
        in_specs=[row, vec,
                  pl.BlockSpec((d, tf), lambda i, j: (0, jnp.minimum(j, last))),
                  pl.BlockSpec((d, tf), lambda i, j: (0, jnp.minimum(j, last))),
                  pl.BlockSpec((tf, d), lambda i, j: (jnp.minimum(j, last), 0)),
                  pl.BlockSpec((d, tail), lambda i, j: (0, 0)),
                  pl.BlockSpec((d, tail), lambda i, j: (0, 0)),
                  pl.BlockSpec((tail, d), lambda i, j: (0, 0)),
                  vec],
        out_specs=out_specs,
        out_shape=out_shape,
        scratch_shapes=[pltpu.VMEM((tm, d), BF16)],
        compiler_params=_params(("parallel", "arbitrary")),
        name="ffn_resid" if emit_resid else "ffn_final",
    )(x, g, wg, wu, wd, wgt, wut, wdt, gn)


def _in_proj_kernel(x_ref, wt_ref, o_ref, wb_sc, *, lane_blocks):
    @pl.when(pl.program_id(1) == 0)
    def _():
        wb_sc[...] = wt_ref[...].T.astype(BF16)

    res = jnp.dot(x_ref[...], wb_sc[...], preferred_element_type=F32)
    if lane_blocks:
        for c in range(res.shape[1] // LANES):
            o_ref[c] = res[:, c * LANES:(c + 1) * LANES].astype(o_ref.dtype)
    else:
        o_ref[...] = res.astype(o_ref.dtype)


def _in_proj(x, wt, row0, n, out_dtype, *, tm, tn, name, lane_blocks=False):
    m, k = x.shape
    assert row0 % SUBLANES == 0 and n % tn == 0
    if lane_blocks:
        out_shape = jax.ShapeDtypeStruct((n // LANES, m, LANES), out_dtype)
        out_spec = pl.BlockSpec((tn // LANES, tm, LANES), lambda j, i: (j, i, 0))
    else:
        out_shape = jax.ShapeDtypeStruct((m, n), out_dtype)
        out_spec = pl.BlockSpec((tm, tn), lambda j, i: (i, j))
    return pl.pallas_call(
        functools.partial(_in_proj_kernel, lane_blocks=lane_blocks),
        grid=(n // tn, m // tm),
        in_specs=[pl.BlockSpec((tm, k), lambda j, i: (i, 0)),
                  pl.BlockSpec((pl.Element(tn), pl.Element(k)),
                               lambda j, i: (pl.multiple_of(row0 + j * tn, SUBLANES), 0))],
        out_specs=out_spec,
        out_shape=out_shape,
        scratch_shapes=[pltpu.VMEM((k, tn), BF16)],
        compiler_params=_params(("parallel", "arbitrary")),
        name=name,
    )(x, wt)


def _mm_res_kernel(x_ref, w_ref, r_ref, o_ref):
    o_ref[...] = r_ref[...] + jnp.dot(x_ref[...], w_ref[...], preferred_element_type=F32)


def _matmul_residual(x, w, r, *, tm, tn):
    m, k = x.shape
    n = w.shape[1]
    return pl.pallas_call(
        _mm_res_kernel,
        grid=(n // tn, m // tm),
        in_specs=[pl.BlockSpec((tm, k), lambda j, i: (i, 0)),
                  pl.BlockSpec((k, tn), lambda j, i: (0, j)),
                  pl.BlockSpec((tm, tn), lambda j, i: (i, j))],
        out_specs=pl.BlockSpec((tm, tn), lambda j, i: (i, j)),
        out_shape=jax.ShapeDtypeStruct((m, n), F32),
        compiler_params=_params(("parallel", "parallel")),
        name="out_proj",
    )(x, w, r)


def _merge_kernel(yr_ref, ya_ref, wr_ref, wa_ref, gr_ref, ga_ref, o_ref):
    pr = jnp.dot(yr_ref[...], wr_ref[...], preferred_element_type=F32)
    pa = jnp.dot(ya_ref[...], wa_ref[...], preferred_element_type=F32)
    o_ref[...] = (_sigmoid(gr_ref[...]) * pr + _sigmoid(ga_ref[...]) * pa).astype(o_ref.dtype)


def _merge(y_rnn, y_attn, w_r, w_a, pf, gr_col, ga_col, *, tm, tn):
    m, k = y_rnn.shape
    n = w_r.shape[1]
    act = pl.BlockSpec((tm, k), lambda j, i: (i, 0))
    wsp = pl.BlockSpec((k, tn), lambda j, i: (0, j))
    return pl.pallas_call(
        _merge_kernel,
        grid=(n // tn, m // tm),
        in_specs=[act, act, wsp, wsp,
                  pl.BlockSpec((tm, tn), lambda j, i: (i, gr_col // tn + j)),
                  pl.BlockSpec((tm, tn), lambda j, i: (i, ga_col // tn + j))],
        out_specs=pl.BlockSpec((tm, tn), lambda j, i: (i, j)),
        out_shape=jax.ShapeDtypeStruct((m, n), BF16),
        compiler_params=_params(("parallel", "parallel")),
        name="merge",
    )(y_rnn, y_attn, w_r, w_a, pf, pf)


def _rglru_kernel(rx_ref, rg_ref, cw_ref, cb_ref, wax_ref, ba_ref, bx_ref, lam_ref,
                  y_ref, xs_sc, h_sc):
    ts, tc = rx_ref.shape
    ng = ts // SUBLANES

    @pl.when(pl.program_id(2) == 0)
    def _():
        xs_sc[0:SUBLANES, :] = jnp.zeros((SUBLANES, tc), F32)
        h_sc[...] = jnp.zeros_like(h_sc)

    x = rx_ref[...]
    xs_sc[SUBLANES:SUBLANES + ts, :] = x
    cw = cw_ref[...]
    xc = cb_ref[...]
    for k in range(CONV_WIDTH - 1):
        off = SUBLANES - (CONV_WIDTH - 1) + k
        xc = xc + xs_sc[off:off + ts, :] * cw[k:k + 1, :]
    xc = xc + x * cw[CONV_WIDTH - 1:CONV_WIDTH, :]
    xs_sc[0:SUBLANES, :] = x[ts - SUBLANES:ts, :]

    xcb = xc.astype(BF16)
    r_parts, i_parts = [], []
    for n in range(tc // RNN_BLOCK):
        g = jnp.dot(xcb[:, n * RNN_BLOCK:(n + 1) * RNN_BLOCK], wax_ref[n],
                    preferred_element_type=F32)
        r_parts.append(g[:, :RNN_BLOCK])
        i_parts.append(g[:, RNN_BLOCK:])
    tr = jnp.tanh(jnp.concatenate(r_parts, axis=1) + ba_ref[...])
    ti = jnp.tanh(jnp.concatenate(i_parts, axis=1) + bx_ref[...])

    nl = -lam_ref[...]
    softplus = jnp.maximum(nl, 0.0) + jnp.log1p(jnp.exp(-jnp.abs(nl)))
    log_a = (tr + 1.0) * ((-0.5 * RG_C) * softplus)
    a = jnp.exp(log_a)
    mult = jnp.sqrt(-jnp.tanh(log_a) * (a * a + 1.0))
    u = mult * ((0.5 * ti + 0.5) * xc)

    av = a.reshape(ng, SUBLANES, tc)
    bv = u.reshape(ng, SUBLANES, tc)
    ri = lax.broadcasted_iota(I32, (ng, SUBLANES, tc), 1)
    for d in (1, 2, 4):
        a_prev = jnp.where(ri >= d, pltpu.roll(av, d, axis=1), 1.0)
        b_prev = jnp.where(ri >= d, pltpu.roll(bv, d, axis=1), 0.0)
        bv = av * b_prev + bv
        av = av * a_prev
    carry = h_sc[0:1, :]
    hs = []
    for k in range(ng):
        hk = bv[k] + av[k] * carry
        hs.append(hk)
        carry = hk[SUBLANES - 1:SUBLANES, :]
    h_sc[0:1, :] = carry
    h = jnp.concatenate(hs, axis=0)

    gt = rg_ref[...]
    c1 = math.sqrt(2.0 / math.pi)
    inner = gt * (c1 + (c1 * 0.044715) * (gt * gt))
    y_ref[...] = ((h * (0.5 * gt)) * (1.0 + jnp.tanh(inner))).astype(y_ref.dtype)


def _rglru(pf, rx_col, rg_col, conv_w, conv_b, wax, b_a, b_x, lam, *, width, ts, tc):
    bsz, s, _ = pf.shape
    chan = lambda rows: pl.BlockSpec((rows, tc), lambda b, c, t: (0, c))
    return pl.pallas_call(
        _rglru_kernel,
        grid=(bsz, width // tc, s // ts),
        in_specs=[pl.BlockSpec((None, ts, tc), lambda b, c, t: (b, t, rx_col // tc + c)),
                  pl.BlockSpec((None, ts, tc), lambda b, c, t: (b, t, rg_col // tc + c)),
                  chan(CONV_WIDTH), chan(1),
                  pl.BlockSpec((tc // RNN_BLOCK, RNN_BLOCK, 2 * RNN_BLOCK),
                               lambda b, c, t: (c, 0, 0)),
                  chan(1), chan(1), chan(1)],
        out_specs=pl.BlockSpec((None, ts, tc), lambda b, c, t: (b, t, c)),
        out_shape=jax.ShapeDtypeStruct((bsz, s, width), BF16),
        scratch_shapes=[pltpu.VMEM((SUBLANES + ts, tc), F32),
                        pltpu.VMEM((SUBLANES, tc), F32)],
        compiler_params=_params(("parallel", "parallel", "arbitrary")),
        name="rglru",
    )(pf, pf, conv_w, conv_b, wax, b_a, b_x, lam)


def _bias_kernel(rb_ref, o_ref):
    h = pl.program_id(0)
    far = rb_ref[NUM_BUCKETS - 1, h]
    ii = lax.broadcasted_iota(I32, (KEY_BLOCK, KEY_BLOCK), 0)
    jj = lax.broadcasted_iota(I32, (KEY_BLOCK, KEY_BLOCK), 1)
    max_exact = NUM_BUCKETS // 2
    for d in range(2):
        n = jnp.maximum(ii - jj + KEY_BLOCK * d, 0)
        nf = jnp.maximum(n, 1).astype(F32)
        large = max_exact + (jnp.log(nf / max_exact) / math.log(MAX_DISTANCE / max_exact)
                             * (NUM_BUCKETS - max_exact)).astype(I32)
        large = jnp.minimum(large, NUM_BUCKETS - 1)
        bucket = jnp.where(n < max_exact, n, large)
        acc = jnp.zeros((KEY_BLOCK, KEY_BLOCK), F32)
        for b in range(NUM_BUCKETS):
            acc = jnp.where(bucket == b, rb_ref[b, h], acc)
        o_ref[0, d] = (acc - far) * LOG2E
    o_ref[0, 2] = jnp.zeros((KEY_BLOCK, KEY_BLOCK), F32)


def _bias_tiles(rel_bias):
    heads = rel_bias.shape[1]
    return pl.pallas_call(
        _bias_kernel,
        grid=(heads,),
        in_specs=[pl.BlockSpec(memory_space=pltpu.SMEM)],
        out_specs=pl.BlockSpec((1, 3, KEY_BLOCK, KEY_BLOCK), lambda h: (h, 0, 0, 0)),
        out_shape=jax.ShapeDtypeStruct((heads, 3, KEY_BLOCK, KEY_BLOCK), F32),
        compiler_params=_params(("parallel",)),
        name="bias_tiles",
    )(rel_bias)


SCORE_CHUNK = 4
QK_CHUNK = 2
VALUE_CHUNK = 4


def _attn_body(q_ref, k_ref, v_ref, iq_ref, ik_ref, iw_ref, bt_ref, y_ref,
               keys_sc, keyst_sc, am_sc, wb_sc, lg_sc, pb_sc, *, k_top, nkb, class_blocks):
    heads, tq, _ = q_ref.shape
    groups = heads // KV_HEADS
    nk = nkb * KEY_BLOCK
    qi = pl.program_id(1)
    row = qi * tq + lax.broadcasted_iota(I32, (tq, KEY_BLOCK), 0)
    lane = lax.broadcasted_iota(I32, (tq, KEY_BLOCK), 1)

    w_scale = (IDX_HEADS ** -0.5) * (IDX_DIM ** -0.5)
    iw = iw_ref[...]
    for h in range(IDX_HEADS):
        col = iw[:, IDX_DIM + h:IDX_DIM + h + 1] * w_scale
        wb_sc[h] = jnp.broadcast_to(col, (tq, KEY_BLOCK))
    per_slab = LANES // IDX_DIM
    iq_rows = jnp.concatenate(
        [iq_ref[h // per_slab][:, (h % per_slab) * IDX_DIM:(h % per_slab + 1) * IDX_DIM]
         for h in range(IDX_HEADS)], axis=0)

    for c0 in range(0, nkb, SCORE_CHUNK):
        nb = min(SCORE_CHUNK, nkb - c0)
        ikc = ik_ref[c0 * KEY_BLOCK:(c0 + nb) * KEY_BLOCK, 0:IDX_DIM].astype(BF16)
        dots = lax.dot_general(iq_rows, ikc, (((1,), (1,)), ((), ())),
                               preferred_element_type=F32)
        for sub in range(nb):
            kb = c0 + sub
            acc = jnp.zeros((tq, KEY_BLOCK), F32)
            for h in range(IDX_HEADS):
                d = dots[h * tq:(h + 1) * tq, sub * KEY_BLOCK:(sub + 1) * KEY_BLOCK]
                acc = acc + jnp.maximum(d, 0.0) * wb_sc[h]
            score = jnp.where(kb * KEY_BLOCK + lane <= row, acc, -jnp.inf)
            bits = pltpu.bitcast(score, I32)
            key = bits ^ ((bits >> 31) & 0x7FFFFFFF)
            keys_sc[kb] = key
            keyst_sc[kb] = key.T.reshape(KEY_BLOCK // SUBLANES, SUBLANES, tq)

    scale2 = (HEAD_DIM ** -0.5) * LOG2E
    chunk = QK_CHUNK * KEY_BLOCK
    cpg = nkb // QK_CHUNK
    n_qk = KV_HEADS * cpg
    assert nkb % QK_CHUNK == 0 and n_qk <= 32

    def bit_step(it, thr, with_qk):
        if with_qk:
            g = it // cpg
            start = pl.multiple_of((it % cpg) * chunk, chunk)
            qg = q_ref[pl.ds(g * groups, groups)].reshape(groups * tq, HEAD_DIM)
            lg_sc[it] = lax.dot_general(qg, k_ref[g, pl.ds(start, chunk), :],
                                        (((1,), (1,)), ((), ())),
                                        preferred_element_type=F32) * scale2
        cand = thr + lax.shift_left(jnp.int32(1), 31 - it)
        cand8 = jnp.broadcast_to(cand, (SUBLANES, tq))[None]
        parts = [jnp.sum((keyst_sc[kb] >= cand8).astype(I32), axis=0) for kb in range(nkb)]
        total = jnp.sum(_tree_sum(parts), axis=0, keepdims=True)
        return jnp.where(total >= k_top, cand, thr)

    thr = jnp.full((1, tq), INT32_MIN, I32)
    thr = lax.fori_loop(0, n_qk, functools.partial(bit_step, with_qk=True), thr, unroll=4)
    thr = lax.fori_loop(n_qk, 32, functools.partial(bit_step, with_qk=False), thr)
    thr = jnp.maximum(thr, KEY_NEG_INF + 1)
    thr_b = jnp.broadcast_to(thr, (KEY_BLOCK, tq)).T
    for kb in range(nkb):
        am_sc[kb] = jnp.where(keys_sc[kb] >= thr_b, 0.0, NEG_LOGIT)

    first_near = max(nkb - class_blocks - 1, 0)

    def logit_block(g, kb):
        sub = kb % QK_CHUNK
        return g * cpg + kb // QK_CHUNK, slice(sub * KEY_BLOCK, (sub + 1) * KEY_BLOCK)

    for g in range(KV_HEADS):
        m_parts = [[] for _ in range(groups)]
        for kb in range(nkb):
            am = am_sc[kb]
            ci, cols = logit_block(g, kb)
            for j in range(groups):
                rows = slice(j * tq, (j + 1) * tq)
                sj = lg_sc[ci, rows, cols] + am
                if kb >= first_near:
                    sj = sj + bt_ref[g * groups + j, jnp.clip(qi - kb, 0, 2)]
                lg_sc[ci, rows, cols] = sj
                m_parts[j].append(sj)
        m_row = jnp.concatenate(
            [jnp.max(functools.reduce(jnp.maximum, mp), axis=1, keepdims=True) for mp in m_parts],
            axis=0)
        m_b = jnp.broadcast_to(m_row, (groups * tq, KEY_BLOCK))
        l_parts = []
        acc = jnp.zeros((groups * tq, HEAD_DIM), F32)
        for c0 in range(0, nkb, VALUE_CHUNK):
            nb = min(VALUE_CHUNK, nkb - c0)
            for kb in range(c0, c0 + nb):
                ci, cols = logit_block(g, kb)
                p = jnp.exp2(lg_sc[ci, :, cols] - m_b)
                pb_sc[g, :, kb * KEY_BLOCK:(kb + 1) * KEY_BLOCK] = p.astype(BF16)
                l_parts.append(p)
            rows = slice(c0 * KEY_BLOCK, (c0 + nb) * KEY_BLOCK)
            acc = acc + jnp.dot(pb_sc[g, :, rows], v_ref[g, rows, :], preferred_element_type=F32)
        l_row = jnp.sum(_tree_sum(l_parts), axis=1, keepdims=True)
        out = acc / l_row
        for j in range(groups):
            hd = (g * groups + j) * HEAD_DIM
            y_ref[:, hd:hd + HEAD_DIM] = out[j * tq:(j + 1) * tq].astype(y_ref.dtype)


def _attn_kernel(*refs, k_top, class_blocks, nq):
    qi = pl.program_id(1)
    for c in range(-(-nq // class_blocks)):
        nkb = min((c + 1) * class_blocks, nq)

        @pl.when(qi // class_blocks == c)
        def _(nkb=nkb):
            _attn_body(*refs, k_top=k_top, nkb=nkb, class_blocks=class_blocks)


def _attention(pb, ps, btab, *, bsz, heads, tq, class_blocks):
    s = ps.shape[1]
    nq = s // tq
    iq_slabs = IDX_HEADS * IDX_DIM // LANES
    assert HEAD_DIM == LANES and heads % KV_HEADS == 0 and heads % iq_slabs == 0
    assert pb.shape[0] == heads + 2 * KV_HEADS + iq_slabs
    nkb_max = s // KEY_BLOCK
    groups = heads // KV_HEADS
    k_top = min(TOPK_MAX, s // 4)
    assert tq == KEY_BLOCK and nq % class_blocks == 0
    return pl.pallas_call(
        functools.partial(_attn_kernel, k_top=k_top, class_blocks=class_blocks, nq=nq),
        grid=(bsz, nq),
        in_specs=[pl.BlockSpec((heads, tq, LANES), lambda b, i: (0, b * nq + i, 0)),
                  pl.BlockSpec((KV_HEADS, s, LANES), lambda b, i: (heads // KV_HEADS, b, 0)),
                  pl.BlockSpec((KV_HEADS, s, LANES), lambda b, i: (heads // KV_HEADS + 1, b, 0)),
                  pl.BlockSpec((iq_slabs, tq, LANES),
                               lambda b, i: ((heads + 2 * KV_HEADS) // iq_slabs, b * nq + i, 0)),
                  pl.BlockSpec((None, s, LANES), lambda b, i: (b, 0, 0)),
                  pl.BlockSpec((None, tq, LANES), lambda b, i: (b, i, 0)),
                  pl.BlockSpec(btab.shape, lambda b, i: (0, 0, 0, 0))],
        out_specs=pl.BlockSpec((tq, heads * HEAD_DIM), lambda b, i: (b * nq + i, 0)),
        out_shape=jax.ShapeDtypeStruct((bsz * s, heads * HEAD_DIM), BF16),
        scratch_shapes=[
            pltpu.VMEM((nkb_max, tq, KEY_BLOCK), I32),
            pltpu.VMEM((nkb_max, KEY_BLOCK // SUBLANES, SUBLANES, tq), I32),
            pltpu.VMEM((nkb_max, tq, KEY_BLOCK), F32),
            pltpu.VMEM((IDX_HEADS, tq, KEY_BLOCK), F32),
            pltpu.VMEM((KV_HEADS * nkb_max // QK_CHUNK, groups * tq, QK_CHUNK * KEY_BLOCK), F32),
            pltpu.VMEM((KV_HEADS, groups * tq, s), BF16)],
        compiler_params=_params(("parallel", "arbitrary")),
        name="sparse_attn",
    )(pb, pb, pb, pb, ps, ps, btab)


FFN_TM, FFN_TF = 512, 512
MM_TM, MM_TN = 1024, 1024
RGLRU_TS, RGLRU_TC = 256, 512
ATTN_CLASS_BLOCKS = 4


def _layer(x, ffn1_norm, ffn1_w_gate, ffn1_w_up, ffn1_w_down, mix_norm, w_in, conv_w, conv_b,
           rg_w_a, rg_b_a, rg_w_x, rg_b_x, rg_lambda, btab, w_proj_rnn, w_proj_attn, w_out,
           ffn2_norm, ffn2_w_gate, ffn2_w_up, ffn2_w_down, next_norm, *, last):
    bsz, s, d = x.shape
    m = bsz * s
    width = d
    q_width = d
    kv_width = KV_HEADS * HEAD_DIM
    iq_width = IDX_HEADS * IDX_DIM
    tm = min(FFN_TM, m)
    tmm = min(MM_TM, m)

    o_q = 2 * width
    o_ik = o_q + q_width + 2 * kv_width + iq_width
    o_gr = o_ik + IDX_DIM + IDX_HEADS

    bf = lambda w: w.astype(BF16)
    vec = lambda p: p.reshape(1, -1)

    x1, hn = _ffn(x.reshape(m, d), vec(ffn1_norm), bf(ffn1_w_gate), bf(ffn1_w_up), bf(ffn1_w_down),
                  vec(mix_norm), emit_resid=True, tm=tm, tf=FFN_TF)

    w_t = w_in.T
    pf = _in_proj(hn, w_t, 0, o_q, F32, tm=tmm, tn=MM_TN, name="in_proj_rnn")
    pb = _in_proj(hn, w_t, o_q, o_ik - o_q, BF16, tm=tmm, tn=MM_TN, name="in_proj_attn",
                  lane_blocks=True)
    ps = _in_proj(hn, w_t, o_ik, LANES, F32, tm=tmm, tn=LANES, name="in_proj_idx")
    pg = _in_proj(hn, w_t, o_gr, 2 * d, F32, tm=tmm, tn=MM_TN, name="in_proj_gates")

    wax = bf(0.5 * jnp.concatenate([rg_w_a, rg_w_x], axis=-1))
    y_rnn = _rglru(pf.reshape(bsz, s, -1), 0, width, conv_w, vec(conv_b), wax,
                   vec(0.5 * rg_b_a), vec(0.5 * rg_b_x), vec(rg_lambda),
                   width=width, ts=min(RGLRU_TS, s), tc=RGLRU_TC)

    y_attn = _attention(pb, ps.reshape(bsz, s, LANES), btab, bsz=bsz, heads=q_width // HEAD_DIM,
                        tq=KEY_BLOCK, class_blocks=ATTN_CLASS_BLOCKS)

    merged = _merge(y_rnn.reshape(m, width), y_attn,
                    bf(w_proj_rnn), bf(w_proj_attn), pg, 0, d, tm=tm, tn=MM_TN)
    x2 = _matmul_residual(merged, bf(w_out), x1, tm=tmm, tn=MM_TN)

    outs = _ffn(x2, vec(ffn2_norm), bf(ffn2_w_gate), bf(ffn2_w_up), bf(ffn2_w_down),
                vec(next_norm), emit_resid=not last, tm=tm, tf=FFN_TF)
    return outs[0].reshape(bsz, s, d)


def kernel(x, ffn1_norm, ffn1_w_gate, ffn1_w_up, ffn1_w_down, mix_norm, w_in, conv_w, conv_b,
           rg_w_a, rg_b_a, rg_w_x, rg_b_x, rg_lambda, rel_bias, w_proj_rnn, w_proj_attn, w_out,
           ffn2_norm, ffn2_w_gate, ffn2_w_up, ffn2_w_down, final_norm):
    depth = ffn1_norm.shape[0]
    assert depth == 1, "the fused final norm assumes a single layer"
    btab = _bias_tiles(rel_bias)
    l = 0
    return _layer(x, ffn1_norm[l], ffn1_w_gate[l], ffn1_w_up[l], ffn1_w_down[l], mix_norm[l],
                  w_in[l], conv_w[l], conv_b[l], rg_w_a[l], rg_b_a[l], rg_w_x[l], rg_b_x[l],
                  rg_lambda[l], btab, w_proj_rnn[l], w_proj_attn[l], w_out[l], ffn2_norm[l],
                  ffn2_w_gate[l], ffn2_w_up[l], ffn2_w_down[l], final_norm, last=True)
```

```python
import functools
import math

import jax
import jax.numpy as jnp
from jax import lax
from jax.experimental import pallas as pl
from jax.experimental.pallas import tpu as pltpu

F32 = jnp.float32
BF16 = jnp.bfloat16
I32 = jnp.int32

RMS_EPS = 1e-6
CONV_WIDTH = 4
RG_C = 8.0
RNN_BLOCK = 128
HEAD_DIM = 128
KV_HEADS = 4
IDX_HEADS = 16
IDX_DIM = 64
TOPK_MAX = 256
NUM_BUCKETS = 32
MAX_DISTANCE = 128

LANES = 128
SUBLANES = 8
VMEM_LIMIT_BYTES = 56 * 1024 * 1024

KEY_BLOCK = 128
NEG_LOGIT = -1e30
INT32_MIN = -(2 ** 31)
KEY_NEG_INF = -2139095041
LOG2E = 1.4426950408889634


def _params(semantics):
    return pltpu.CompilerParams(dimension_semantics=semantics,
                                vmem_limit_bytes=VMEM_LIMIT_BYTES)


def _sigmoid(x):
    return 0.5 * (jnp.tanh(0.5 * x) + 1.0)


def _rms(x, g):
    ms = jnp.mean(x * x, axis=-1, keepdims=True)
    return x * lax.rsqrt(ms + RMS_EPS) * g


def _tree_sum(parts):
    while len(parts) > 1:
        parts = [a + b for a, b in zip(parts[::2], parts[1::2])] + (
            [parts[-1]] if len(parts) % 2 else [])
    return parts[0]


def _ffn_kernel(x_ref, g_ref, wg_ref, wu_ref, wd_ref, wgt_ref, wut_ref, wdt_ref, gn_ref, *rest,
                emit_resid, nfull):
    if emit_resid:
        acc_ref, hn_ref, xn_sc = rest
    else:
        acc_ref, xn_sc = rest
        hn_ref = acc_ref
    j = pl.program_id(1)

    @pl.when(j == 0)
    def _():
        xn_sc[...] = _rms(x_ref[...], g_ref[...]).astype(BF16)
        acc_ref[...] = jnp.zeros_like(acc_ref)

    def accumulate(wg, wu, wd):
        xn = xn_sc[...]
        h = jnp.dot(xn, wg, preferred_element_type=F32)
        u = jnp.dot(xn, wu, preferred_element_type=F32)
        a = (h * _sigmoid(h) * u).astype(BF16)
        acc_ref[...] += jnp.dot(a, wd, preferred_element_type=F32)

    @pl.when(j < nfull)
    def _():
        accumulate(wg_ref[...], wu_ref[...], wd_ref[...])

    @pl.when(j == nfull)
    def _():
        accumulate(wgt_ref[...], wut_ref[...], wdt_ref[...])
        y = x_ref[...] + 0.5 * acc_ref[...]
        if emit_resid:
            acc_ref[...] = y
        hn_ref[...] = _rms(y, gn_ref[...]).astype(hn_ref.dtype)


def _ffn(x, g, wg, wu, wd, gn, *, emit_resid, tm, tf):
    m, d = x.shape
    f = wg.shape[1]
    nfull = (f - 1) // tf
    tail = f - nfull * tf
    assert nfull >= 1 and tail % LANES == 0
    wgt, wut, wdt = wg[:, nfull * tf:], wu[:, nfull * tf:], wd[nfull * tf:, :]
    last = nfull - 1
    row = pl.BlockSpec((tm, d), lambda i, j: (i, 0))
    vec = pl.BlockSpec((1, d), lambda i, j: (0, 0))
    out_shape = [jax.ShapeDtypeStruct((m, d), F32)]
    out_specs = [row]
    if emit_resid:
        out_shape.append(jax.ShapeDtypeStruct((m, d), BF16))
        out_specs.append(row)
    return pl.pallas_call(
        functools.partial(_ffn_kernel, emit_resid=emit_resid, nfull=nfull),
        grid=(m // tm, nfull + 1),
        in_specs=[row, vec,
                  pl.BlockSpec((d, tf), lambda i, j: (0, jnp.minimum(j, last))),
                  pl.BlockSpec((d, tf), lambda i, j: (0, jnp.minimum(j, last))),
                  pl.BlockSpec((tf, d), lambda i, j: (jnp.minimum(j, last), 0)),
                  pl.BlockSpec((d, tail), lambda i, j: (0, 0)),
                  pl.BlockSpec((d, tail), lambda i, j: (0, 0)),
                  pl.BlockSpec((tail, d), lambda i, j: (0, 0)),
                  vec],
        out_specs=out_specs,
        out_shape=out_shape,
        scratch_shapes=[pltpu.VMEM((tm, d), BF16)],
        compiler_params=_params(("parallel", "arbitrary")),
        name="ffn_resid" if emit_resid else "ffn_final",
    )(x, g, wg, wu, wd, wgt, wut, wdt, gn)


def _in_proj_kernel(x_ref, wt_ref, o_ref, wb_sc, *, lane_blocks):
    @pl.when(pl.program_id(1) == 0)
    def _():
        wb_sc[...] = wt_ref[...].T.astype(BF16)

    res = jnp.dot(x_ref[...], wb_sc[...], preferred_element_type=F32)
    if lane_blocks:
        for c in range(res.shape[1] // LANES):
            o_ref[c] = res[:, c * LANES:(c + 1) * LANES].astype(o_ref.dtype)
    else:
        o_ref[...] = res.astype(o_ref.dtype)


def _in_proj(x, wt, row0, n, out_dtype, *, tm, tn, name, lane_blocks=False):
    m, k = x.shape
    assert row0 % SUBLANES == 0 and n % tn == 0
    if lane_blocks:
        out_shape = jax.ShapeDtypeStruct((n // LANES, m, LANES), out_dtype)
        out_spec = pl.BlockSpec((tn // LANES, tm, LANES), lambda j, i: (j, i, 0))
    else:
        out_shape = jax.ShapeDtypeStruct((m, n), out_dtype)
        out_spec = pl.BlockSpec((tm, tn), lambda j, i: (i, j))
    return pl.pallas_call(
        functools.partial(_in_proj_kernel, lane_blocks=lane_blocks),
        grid=(n // tn, m // tm),
        in_specs=[pl.BlockSpec((tm, k), lambda j, i: (i, 0)),
                  pl.BlockSpec((pl.Element(tn), pl.Element(k)),
                               lambda j, i: (pl.multiple_of(row0 + j * tn, SUBLANES), 0))],
        out_specs=out_spec,
        out_shape=out_shape,
        scratch_shapes=[pltpu.VMEM((k, tn), BF16)],
        compiler_params=_params(("parallel", "arbitrary")),
        name=name,
    )(x, wt)


def _mm_res_kernel(x_ref, w_ref, r_ref, o_ref):
    o_ref[...] = r_ref[...] + jnp.dot(x_ref[...], w_ref[...], preferred_element_type=F32)


def _matmul_residual(x, w, r, *, tm, tn):
    m, k = x.shape
    n = w.shape[1]
    return pl.pallas_call(
        _mm_res_kernel,
        grid=(n // tn, m // tm),
        in_specs=[pl.BlockSpec((tm, k), lambda j, i: (i, 0)),
                  pl.BlockSpec((k, tn), lambda j, i: (0, j)),
                  pl.BlockSpec((tm, tn), lambda j, i: (i, j))],
        out_specs=pl.BlockSpec((tm, tn), lambda j, i: (i, j)),
        out_shape=jax.ShapeDtypeStruct((m, n), F32),
        compiler_params=_params(("parallel", "parallel")),
        name="out_proj",
    )(x, w, r)


def _merge_kernel(yr_ref, ya_ref, wr_ref, wa_ref, gr_ref, ga_ref, o_ref):
    pr = jnp.dot(yr_ref[...], wr_ref[...], preferred_element_type=F32)
    pa = jnp.dot(ya_ref[...], wa_ref[...], preferred_element_type=F32)
    o_ref[...] = (_sigmoid(gr_ref[...]) * pr + _sigmoid(ga_ref[...]) * pa).astype(o_ref.dtype)


def _merge(y_rnn, y_attn, w_r, w_a, pf, gr_col, ga_col, *, tm, tn):
    m, k = y_rnn.shape
    n = w_r.shape[1]
    act = pl.BlockSpec((tm, k), lambda j, i: (i, 0))
    wsp = pl.BlockSpec((k, tn), lambda j, i: (0, j))
    return pl.pallas_call(
        _merge_kernel,
        grid=(n // tn, m // tm),
        in_specs=[act, act, wsp, wsp,
                  pl.BlockSpec((tm, tn), lambda j, i: (i, gr_col // tn + j)),
                  pl.BlockSpec((tm, tn), lambda j, i: (i, ga_col // tn + j))],
        out_specs=pl.BlockSpec((tm, tn), lambda j, i: (i, j)),
        out_shape=jax.ShapeDtypeStruct((m, n), BF16),
        compiler_params=_params(("parallel", "parallel")),
        name="merge",
    )(y_rnn, y_attn, w_r, w_a, pf, pf)


def _rglru_kernel(rx_ref, rg_ref, cw_ref, cb_ref, wax_ref, ba_ref, bx_ref, lam_ref,
                  y_ref, xs_sc, h_sc):
    ts, tc = rx_ref.shape
    ng = ts // SUBLANES

    @pl.when(pl.program_id(2) == 0)
    def _():
        xs_sc[0:SUBLANES, :] = jnp.zeros((SUBLANES, tc), F32)
        h_sc[...] = jnp.zeros_like(h_sc)

    x = rx_ref[...]
    xs_sc[SUBLANES:SUBLANES + ts, :] = x
    cw = cw_ref[...]
    xc = cb_ref[...]
    for k in range(CONV_WIDTH - 1):
        off = SUBLANES - (CONV_WIDTH - 1) + k
        xc = xc + xs_sc[off:off + ts, :] * cw[k:k + 1, :]
    xc = xc + x * cw[CONV_WIDTH - 1:CONV_WIDTH, :]
    xs_sc[0:SUBLANES, :] = x[ts - SUBLANES:ts, :]

    xcb = xc.astype(BF16)
    r_parts, i_parts = [], []
    for n in range(tc // RNN_BLOCK):
        g = jnp.dot(xcb[:, n * RNN_BLOCK:(n + 1) * RNN_BLOCK], wax_ref[n],
                    preferred_element_type=F32)
        r_parts.append(g[:, :RNN_BLOCK])
        i_parts.append(g[:, RNN_BLOCK:])
    tr = jnp.tanh(jnp.concatenate(r_parts, axis=1) + ba_ref[...])
    ti = jnp.tanh(jnp.concatenate(i_parts, axis=1) + bx_ref[...])

    nl = -lam_ref[...]
    softplus = jnp.maximum(nl, 0.0) + jnp.log1p(jnp.exp(-jnp.abs(nl)))
    log_a = (tr + 1.0) * ((-0.5 * RG_C) * softplus)
    a = jnp.exp(log_a)
    mult = jnp.sqrt(-jnp.tanh(log_a) * (a * a + 1.0))
    u = mult * ((0.5 * ti + 0.5) * xc)

    av = a.reshape(ng, SUBLANES, tc)
    bv = u.reshape(ng, SUBLANES, tc)
    ri = lax.broadcasted_iota(I32, (ng, SUBLANES, tc), 1)
    for d in (1, 2, 4):
        a_prev = jnp.where(ri >= d, pltpu.roll(av, d, axis=1), 1.0)
        b_prev = jnp.where(ri >= d, pltpu.roll(bv, d, axis=1), 0.0)
        bv = av * b_prev + bv
        av = av * a_prev
    carry = h_sc[0:1, :]
    hs = []
    for k in range(ng):
        hk = bv[k] + av[k] * carry
        hs.append(hk)
        carry = hk[SUBLANES - 1:SUBLANES, :]
    h_sc[0:1, :] = carry
    h = jnp.concatenate(hs, axis=0)

    gt = rg_ref[...]
    c1 = math.sqrt(2.0 / math.pi)
    inner = gt * (c1 + (c1 * 0.044715) * (gt * gt))
    y_ref[...] = ((h * (0.5 * gt)) * (1.0 + jnp.tanh(inner))).astype(y_ref.dtype)


def _rglru(pf, rx_col, rg_col, conv_w, conv_b, wax, b_a, b_x, lam, *, width, ts, tc):
    bsz, s, _ = pf.shape
    chan = lambda rows: pl.BlockSpec((rows, tc), lambda b, c, t: (0, c))
    return pl.pallas_call(
        _rglru_kernel,
        grid=(bsz, width // tc, s // ts),
        in_specs=[pl.BlockSpec((None, ts, tc), lambda b, c, t: (b, t, rx_col // tc + c)),
                  pl.BlockSpec((None, ts, tc), lambda b, c, t: (b, t, rg_col // tc + c)),
                  chan(CONV_WIDTH), chan(1),
                  pl.BlockSpec((tc // RNN_BLOCK, RNN_BLOCK, 2 * RNN_BLOCK),
                               lambda b, c, t: (c, 0, 0)),
                  chan(1), chan(1), chan(1)],
        out_specs=pl.BlockSpec((None, ts, tc), lambda b, c, t: (b, t, c)),
        out_shape=jax.ShapeDtypeStruct((bsz, s, width), BF16),
        scratch_shapes=[pltpu.VMEM((SUBLANES + ts, tc), F32),
                        pltpu.VMEM((SUBLANES, tc), F32)],
        compiler_params=_params(("parallel", "parallel", "arbitrary")),
        name="rglru",
    )(pf, pf, conv_w, conv_b, wax, b_a, b_x, lam)


def _bias_kernel(rb_ref, o_ref):
    h = pl.program_id(0)
    far = rb_ref[NUM_BUCKETS - 1, h]
    ii = lax.broadcasted_iota(I32, (KEY_BLOCK, KEY_BLOCK), 0)
    jj = lax.broadcasted_iota(I32, (KEY_BLOCK, KEY_BLOCK), 1)
    max_exact = NUM_BUCKETS // 2
    for d in range(2):
        n = jnp.maximum(ii - jj + KEY_BLOCK * d, 0)
        nf = jnp.maximum(n, 1).astype(F32)
        large = max_exact + (jnp.log(nf / max_exact) / math.log(MAX_DISTANCE / max_exact)
                             * (NUM_BUCKETS - max_exact)).astype(I32)
        large = jnp.minimum(large, NUM_BUCKETS - 1)
        bucket = jnp.where(n < max_exact, n, large)
        acc = jnp.zeros((KEY_BLOCK, KEY_BLOCK), F32)
        for b in range(NUM_BUCKETS):
            acc = jnp.where(bucket == b, rb_ref[b, h], acc)
        o_ref[0, d] = (acc - far) * LOG2E
    o_ref[0, 2] = jnp.zeros((KEY_BLOCK, KEY_BLOCK), F32)


def _bias_tiles(rel_bias):
    heads = rel_bias.shape[1]
    return pl.pallas_call(
        _bias_kernel,
        grid=(heads,),
        in_specs=[pl.BlockSpec(memory_space=pltpu.SMEM)],
        out_specs=pl.BlockSpec((1, 3, KEY_BLOCK, KEY_BLOCK), lambda h: (h, 0, 0, 0)),
        out_shape=jax.ShapeDtypeStruct((heads, 3, KEY_BLOCK, KEY_BLOCK), F32),
        compiler_params=_params(("parallel",)),
        name="bias_tiles",
    )(rel_bias)


SCORE_CHUNK = 4


def _attn_body(q_ref, k_ref, v_ref, iq_ref, ik_ref, iw_ref, bt_ref, y_ref,
               keys_sc, keyst_sc, am_sc, wb_sc, pb_sc, va_sc, *, k_top, nkb, class_blocks):
    heads, tq, _ = q_ref.shape
    groups = heads // KV_HEADS
    nk = nkb * KEY_BLOCK
    qi = pl.program_id(1)
    row = qi * tq + lax.broadcasted_iota(I32, (tq, KEY_BLOCK), 0)
    lane = lax.broadcasted_iota(I32, (tq, KEY_BLOCK), 1)

    w_scale = (IDX_HEADS ** -0.5) * (IDX_DIM ** -0.5)
    iw = iw_ref[...]
    for h in range(IDX_HEADS):
        col = iw[:, IDX_DIM + h:IDX_DIM + h + 1] * w_scale
        wb_sc[h] = jnp.broadcast_to(col, (tq, KEY_BLOCK))
    per_slab = LANES // IDX_DIM
    iq_rows = jnp.concatenate(
        [iq_ref[h // per_slab][:, (h % per_slab) * IDX_DIM:(h % per_slab + 1) * IDX_DIM]
         for h in range(IDX_HEADS)], axis=0)

    for c0 in range(0, nkb, SCORE_CHUNK):
        nb = min(SCORE_CHUNK, nkb - c0)
        ikc = ik_ref[c0 * KEY_BLOCK:(c0 + nb) * KEY_BLOCK, 0:IDX_DIM].astype(BF16)
        dots = lax.dot_general(iq_rows, ikc, (((1,), (1,)), ((), ())),
                               preferred_element_type=F32)
        for sub in range(nb):
            kb = c0 + sub
            acc = jnp.zeros((tq, KEY_BLOCK), F32)
            for h in range(IDX_HEADS):
                d = dots[h * tq:(h + 1) * tq, sub * KEY_BLOCK:(sub + 1) * KEY_BLOCK]
                acc = acc + jnp.maximum(d, 0.0) * wb_sc[h]
            score = jnp.where(kb * KEY_BLOCK + lane <= row, acc, -jnp.inf)
            bits = pltpu.bitcast(score, I32)
            key = bits ^ ((bits >> 31) & 0x7FFFFFFF)
            keys_sc[kb] = key
            keyst_sc[kb] = key.T.reshape(KEY_BLOCK // SUBLANES, SUBLANES, tq)

    def bit_step(it, thr):
        cand = thr + lax.shift_left(jnp.int32(1), 31 - it)
        cand8 = jnp.broadcast_to(cand, (SUBLANES, tq))[None]
        parts = [jnp.sum((keyst_sc[kb] >= cand8).astype(I32), axis=0) for kb in range(nkb)]
        total = jnp.sum(_tree_sum(parts), axis=0, keepdims=True)
        return jnp.where(total >= k_top, cand, thr)

    thr = lax.fori_loop(0, 32, bit_step, jnp.full((1, tq), INT32_MIN, I32))
    thr = jnp.maximum(thr, KEY_NEG_INF + 1)
    thr_b = jnp.broadcast_to(thr, (KEY_BLOCK, tq)).T
    for kb in range(nkb):
        am_sc[kb] = jnp.where(keys_sc[kb] >= thr_b, 0.0, NEG_LOGIT)

    scale2 = (HEAD_DIM ** -0.5) * LOG2E
    first_near = max(nkb - class_blocks - 1, 0)

    def logit(s, g, j, kb):
        sj = (s[j * tq:(j + 1) * tq, kb * KEY_BLOCK:(kb + 1) * KEY_BLOCK] + am_sc[kb])
        if kb >= first_near:
            sj = sj + bt_ref[g * groups + j, jnp.clip(qi - kb, 0, 2)]
        return sj

    def raw_logits(g):
        qg = q_ref[g * groups:(g + 1) * groups].reshape(groups * tq, HEAD_DIM)
        return lax.dot_general(qg, k_ref[g, 0:nk, :], (((1,), (1,)), ((), ())),
                               preferred_element_type=F32) * scale2

    s_next = raw_logits(0)
    for g in range(KV_HEADS):
        s = s_next
        if g + 1 < KV_HEADS:
            s_next = raw_logits(g + 1)
        for j in range(groups):
            m_run = logit(s, g, j, 0)
            for kb in range(1, nkb):
                m_run = jnp.maximum(m_run, logit(s, g, j, kb))
            m_b = jnp.broadcast_to(jnp.max(m_run, axis=1, keepdims=True), (tq, KEY_BLOCK))
            for kb in range(nkb):
                pb_sc[g, j * tq:(j + 1) * tq, kb * KEY_BLOCK:(kb + 1) * KEY_BLOCK] = (
                    jnp.exp2(logit(s, g, j, kb) - m_b).astype(BF16))
        o = jnp.dot(pb_sc[g, :, 0:nk], va_sc[g, 0:nk, :], preferred_element_type=F32)
        out = o[:, 0:HEAD_DIM] / o[:, HEAD_DIM:2 * HEAD_DIM]
        for j in range(groups):
            hd = (g * groups + j) * HEAD_DIM
            y_ref[:, hd:hd + HEAD_DIM] = out[j * tq:(j + 1) * tq].astype(y_ref.dtype)


def _attn_kernel(*refs, k_top, class_blocks, nq):
    qi = pl.program_id(1)
    v_ref, va_sc = refs[2], refs[-1]

    @pl.when(qi == 0)
    def _():
        va_sc[:, :, 0:HEAD_DIM] = v_ref[...]
        va_sc[:, :, HEAD_DIM:2 * HEAD_DIM] = jnp.ones(v_ref.shape, v_ref.dtype)

    for c in range(-(-nq // class_blocks)):
        nkb = min((c + 1) * class_blocks, nq)

        @pl.when(qi // class_blocks == c)
        def _(nkb=nkb):
            _attn_body(*refs, k_top=k_top, nkb=nkb, class_blocks=class_blocks)


def _attention(pb, ps, btab, *, bsz, heads, tq, class_blocks):
    s = ps.shape[1]
    nq = s // tq
    iq_slabs = IDX_HEADS * IDX_DIM // LANES
    assert HEAD_DIM == LANES and heads % KV_HEADS == 0 and heads % iq_slabs == 0
    assert pb.shape[0] == heads + 2 * KV_HEADS + iq_slabs
    nkb_max = s // KEY_BLOCK
    groups = heads // KV_HEADS
    k_top = min(TOPK_MAX, s // 4)
    assert tq == KEY_BLOCK and nq % class_blocks == 0
    return pl.pallas_call(
        functools.partial(_attn_kernel, k_top=k_top, class_blocks=class_blocks, nq=nq),
        grid=(bsz, nq),
        in_specs=[pl.BlockSpec((heads, tq, LANES), lambda b, i: (0, b * nq + i, 0)),
                  pl.BlockSpec((KV_HEADS, s, LANES), lambda b, i: (heads // KV_HEADS, b, 0)),
                  pl.BlockSpec((KV_HEADS, s, LANES), lambda b, i: (heads // KV_HEADS + 1, b, 0)),
                  pl.BlockSpec((iq_slabs, tq, LANES),
                               lambda b, i: ((heads + 2 * KV_HEADS) // iq_slabs, b * nq + i, 0)),
                  pl.BlockSpec((None, s, LANES), lambda b, i: (b, 0, 0)),
                  pl.BlockSpec((None, tq, LANES), lambda b, i: (b, i, 0)),
                  pl.BlockSpec(btab.shape, lambda b, i: (0, 0, 0, 0))],
        out_specs=pl.BlockSpec((tq, heads * HEAD_DIM), lambda b, i: (b * nq + i, 0)),
        out_shape=jax.ShapeDtypeStruct((bsz * s, heads * HEAD_DIM), BF16),
        scratch_shapes=[
            pltpu.VMEM((nkb_max, tq, KEY_BLOCK), I32),
            pltpu.VMEM((nkb_max, KEY_BLOCK // SUBLANES, SUBLANES, tq), I32),
            pltpu.VMEM((nkb_max, tq, KEY_BLOCK), F32),
            pltpu.VMEM((IDX_HEADS, tq, KEY_BLOCK), F32),
            pltpu.VMEM((KV_HEADS, groups * tq, s), BF16),
            pltpu.VMEM((KV_HEADS, s, 2 * HEAD_DIM), BF16)],
        compiler_params=_params(("parallel", "arbitrary")),
        name="sparse_attn",
    )(pb, pb, pb, pb, ps, ps, btab)


FFN_TM, FFN_TF = 512, 512
MM_TM, MM_TN = 1024, 1024
RGLRU_TS, RGLRU_TC = 256, 512
ATTN_CLASS_BLOCKS = 4


def _layer(x, ffn1_norm, ffn1_w_gate, ffn1_w_up, ffn1_w_down, mix_norm, w_in, conv_w, conv_b,
           rg_w_a, rg_b_a, rg_w_x, rg_b_x, rg_lambda, btab, w_proj_rnn, w_proj_attn, w_out,
           ffn2_norm, ffn2_w_gate, ffn2_w_up, ffn2_w_down, next_norm, *, last):
    bsz, s, d = x.shape
    m = bsz * s
    width = d
    q_width = d
    kv_width = KV_HEADS * HEAD_DIM
    iq_width = IDX_HEADS * IDX_DIM
    tm = min(FFN_TM, m)
    tmm = min(MM_TM, m)

    o_q = 2 * width
    o_ik = o_q + q_width + 2 * kv_width + iq_width
    o_gr = o_ik + IDX_DIM + IDX_HEADS

    bf = lambda w: w.astype(BF16)
    vec = lambda p: p.reshape(1, -1)

    x1, hn = _ffn(x.reshape(m, d), vec(ffn1_norm), bf(ffn1_w_gate), bf(ffn1_w_up), bf(ffn1_w_down),
                  vec(mix_norm), emit_resid=True, tm=tm, tf=FFN_TF)

    w_t = w_in.T
    pf = _in_proj(hn, w_t, 0, o_q, F32, tm=tmm, tn=MM_TN, name="in_proj_rnn")
    pb = _in_proj(hn, w_t, o_q, o_ik - o_q, BF16, tm=tmm, tn=MM_TN, name="in_proj_attn",
                  lane_blocks=True)
    ps = _in_proj(hn, w_t, o_ik, LANES, F32, tm=tmm, tn=LANES, name="in_proj_idx")
    pg = _in_proj(hn, w_t, o_gr, 2 * d, F32, tm=tmm, tn=MM_TN, name="in_proj_gates")

    wax = bf(0.5 * jnp.concatenate([rg_w_a, rg_w_x], axis=-1))
    y_rnn = _rglru(pf.reshape(bsz, s, -1), 0, width, conv_w, vec(conv_b), wax,
                   vec(0.5 * rg_b_a), vec(0.5 * rg_b_x), vec(rg_lambda),
                   width=width, ts=min(RGLRU_TS, s), tc=RGLRU_TC)

    y_attn = _attention(pb, ps.reshape(bsz, s, LANES), btab, bsz=bsz, heads=q_width // HEAD_DIM,
                        tq=KEY_BLOCK, class_blocks=ATTN_CLASS_BLOCKS)

    merged = _merge(y_rnn.reshape(m, width), y_attn,
                    bf(w_proj_rnn), bf(w_proj_attn), pg, 0, d, tm=tm, tn=MM_TN)
    x2 = _matmul_residual(merged, bf(w_out), x1, tm=tmm, tn=MM_TN)

    outs = _ffn(x2, vec(ffn2_norm), bf(ffn2_w_gate), bf(ffn2_w_up), bf(ffn2_w_down),
                vec(next_norm), emit_resid=not last, tm=tm, tf=FFN_TF)
    return outs[0].reshape(bsz, s, d)


def kernel(x, ffn1_norm, ffn1_w_gate, ffn1_w_up, ffn1_w_down, mix_norm, w_in, conv_w, conv_b,
           rg_w_a, rg_b_a, rg_w_x, rg_b_x, rg_lambda, rel_bias, w_proj_rnn, w_proj_attn, w_out,
           ffn2_norm, ffn2_w_gate, ffn2_w_up, ffn2_w_down, final_norm):
    depth = ffn1_norm.shape[0]
    assert depth == 1, "the fused final norm assumes a single layer"
    btab = _bias_tiles(rel_bias)
    l = 0
    return _layer(x, ffn1_norm[l], ffn1_w_gate[l], ffn1_w_up[l], ffn1_w_down[l], mix_norm[l],
                  w_in[l], conv_w[l], conv_b[l], rg_w_a[l], rg_b_a[l], rg_w_x[l], rg_b_x[l],
                  rg_lambda[l], btab, w_proj_rnn[l], w_proj_attn[l], w_out[l], ffn2_norm[l],
                  ffn2_w_gate[l], ffn2_w_up[l], ffn2_w_down[l], final_norm, last=True)
```

```python
import functools
import math

import jax
import jax.numpy as jnp
from jax import lax
from jax.experimental import pallas as pl
from jax.experimental.pallas import tpu as pltpu

F32 = jnp.float32
BF16 = jnp.bfloat16
I32 = jnp.int32

RMS_EPS = 1e-6
CONV_WIDTH = 4
RG_C = 8.0
RNN_BLOCK = 128
HEAD_DIM = 128
KV_HEADS = 4
IDX_HEADS = 16
IDX_DIM = 64
TOPK_MAX = 256
NUM_BUCKETS = 32
MAX_DISTANCE = 128

LANES = 128
SUBLANES = 8
VMEM_LIMIT_BYTES = 56 * 1024 * 1024

KEY_BLOCK = 128
NEG_LOGIT = -1e30
INT32_MIN = -(2 ** 31)
KEY_NEG_INF = -2139095041
LOG2E = 1.4426950408889634


def _params(semantics):
    return pltpu.CompilerParams(dimension_semantics=semantics,
                                vmem_limit_bytes=VMEM_LIMIT_BYTES)


def _sigmoid(x):
    return 0.5 * (jnp.tanh(0.5 * x) + 1.0)


def _rms(x, g):
    ms = jnp.mean(x * x, axis=-1, keepdims=True)
    return x * lax.rsqrt(ms + RMS_EPS) * g


def _tree_sum(parts):
    while len(parts) > 1:
        parts = [a + b for a, b in zip(parts[::2], parts[1::2])] + (
            [parts[-1]] if len(parts) % 2 else [])
    return parts[0]


def _ffn_kernel(x_ref, g_ref, wg_ref, wu_ref, wd_ref, wgt_ref, wut_ref, wdt_ref, gn_ref, *rest,
                emit_resid, nfull):
    if emit_resid:
        acc_ref, hn_ref, xn_sc = rest
    else:
        acc_ref, xn_sc = rest
        hn_ref = acc_ref
    j = pl.program_id(1)

    @pl.when(j == 0)
    def _():
        xn_sc[...] = _rms(x_ref[...], g_ref[...]).astype(BF16)
        acc_ref[...] = jnp.zeros_like(acc_ref)

    def accumulate(wg, wu, wd):
        xn = xn_sc[...]
        width = wg.shape[1]
        sub = min(FFN_SUB, width)
        acts = []
        for c in range(0, width, sub):
            h = jnp.dot(xn, wg[:, c:c + sub], preferred_element_type=F32)
            u = jnp.dot(xn, wu[:, c:c + sub], preferred_element_type=F32)
            acts.append((h * _sigmoid(h) * u).astype(BF16))
        a = acts[0] if len(acts) == 1 else jnp.concatenate(acts, axis=1)
        acc_ref[...] += jnp.dot(a, wd[...], preferred_element_type=F32)

    @pl.when(j < nfull)
    def _():
        accumulate(wg_ref, wu_ref, wd_ref)

    @pl.when(j == nfull)
    def _():
        accumulate(wgt_ref, wut_ref, wdt_ref)
        y = x_ref[...] + 0.5 * acc_ref[...]
        if emit_resid:
            acc_ref[...] = y
        hn_ref[...] = _rms(y, gn_ref[...]).astype(hn_ref.dtype)


def _ffn(x, g, wg, wu, wd, gn, *, emit_resid, tm, tf):
    m, d = x.shape
    f = wg.shape[1]
    nfull = (f - 1) // tf
    tail = f - nfull * tf
    assert nfull >= 1 and tail % LANES == 0
    wgt, wut, wdt = wg[:, nfull * tf:], wu[:, nfull * tf:], wd[nfull * tf:, :]
    last = nfull - 1
    row = pl.BlockSpec((tm, d), lambda i, j: (i, 0))
    vec = pl.BlockSpec((1, d), lambda i, j: (0, 0))
    out_shape = [jax.ShapeDtypeStruct((m, d), F32)]
    out_specs = [row]
    if emit_resid:
        out_shape.append(jax.ShapeDtypeStruct((m, d), BF16))
        out_specs.append(row)
    return pl.pallas_call(
        functools.partial(_ffn_kernel, emit_resid=emit_resid, nfull=nfull),
        grid=(m // tm, nfull + 1),
        in_specs=[row, vec,
                  pl.BlockSpec((d, tf), lambda i, j: (0, jnp.minimum(j, last))),
                  pl.BlockSpec((d, tf), lambda i, j: (0, jnp.minimum(j, last))),
                  pl.BlockSpec((tf, d), lambda i, j: (jnp.minimum(j, last), 0)),
                  pl.BlockSpec((d, tail), lambda i, j: (0, 0)),
                  pl.BlockSpec((d, tail), lambda i, j: (0, 0)),
                  pl.BlockSpec((tail, d), lambda i, j: (0, 0)),
                  vec],
        out_specs=out_specs,
        out_shape=out_shape,
        scratch_shapes=[pltpu.VMEM((tm, d), BF16)],
        compiler_params=_params(("parallel", "arbitrary")),
        name="ffn_resid" if emit_resid else "ffn_final",
    )(x, g, wg, wu, wd, wgt, wut, wdt, gn)


def _in_proj_kernel(x_ref, wt_ref, o_ref, wb_sc, *, lane_blocks):
    @pl.when(pl.program_id(1) == 0)
    def _():
        wb_sc[...] = wt_ref[...].T.astype(BF16)

    res = jnp.dot(x_ref[...], wb_sc[...], preferred_element_type=F32)
    if lane_blocks:
        for c in range(res.shape[1] // LANES):
            o_ref[c] = res[:, c * LANES:(c + 1) * LANES].astype(o_ref.dtype)
    else:
        o_ref[...] = res.astype(o_ref.dtype)


def _in_proj(x, wt, row0, n, out_dtype, *, tm, tn, name, lane_blocks=False):
    m, k = x.shape
    assert row0 % SUBLANES == 0 and n % tn == 0
    if lane_blocks:
        out_shape = jax.ShapeDtypeStruct((n // LANES, m, LANES), out_dtype)
        out_spec = pl.BlockSpec((tn // LANES, tm, LANES), lambda j, i: (j, i, 0))
    else:
        out_shape = jax.ShapeDtypeStruct((m, n), out_dtype)
        out_spec = pl.BlockSpec((tm, tn), lambda j, i: (i, j))
    return pl.pallas_call(
        functools.partial(_in_proj_kernel, lane_blocks=lane_blocks),
        grid=(n // tn, m // tm),
        in_specs=[pl.BlockSpec((tm, k), lambda j, i: (i, 0)),
                  pl.BlockSpec((pl.Element(tn), pl.Element(k)),
                               lambda j, i: (pl.multiple_of(row0 + j * tn, SUBLANES), 0))],
        out_specs=out_spec,
        out_shape=out_shape,
        scratch_shapes=[pltpu.VMEM((k, tn), BF16)],
        compiler_params=_params(("parallel", "arbitrary")),
        name=name,
    )(x, wt)


def _mm_res_kernel(x_ref, w_ref, r_ref, o_ref):
    o_ref[...] = r_ref[...] + jnp.dot(x_ref[...], w_ref[...], preferred_element_type=F32)


def _matmul_residual(x, w, r, *, tm, tn):
    m, k = x.shape
    n = w.shape[1]
    return pl.pallas_call(
        _mm_res_kernel,
        grid=(n // tn, m // tm),
        in_specs=[pl.BlockSpec((tm, k), lambda j, i: (i, 0)),
                  pl.BlockSpec((k, tn), lambda j, i: (0, j)),
                  pl.BlockSpec((tm, tn), lambda j, i: (i, j))],
        out_specs=pl.BlockSpec((tm, tn), lambda j, i: (i, j)),
        out_shape=jax.ShapeDtypeStruct((m, n), F32),
        compiler_params=_params(("parallel", "parallel")),
        name="out_proj",
    )(x, w, r)


def _merge_kernel(yr_ref, ya_ref, wr_ref, wa_ref, gr_ref, ga_ref, o_ref):
    pr = jnp.dot(yr_ref[...], wr_ref[...], preferred_element_type=F32)
    pa = jnp.dot(ya_ref[...], wa_ref[...], preferred_element_type=F32)
    o_ref[...] = (_sigmoid(gr_ref[...]) * pr + _sigmoid(ga_ref[...]) * pa).astype(o_ref.dtype)


def _merge(y_rnn, y_attn, w_r, w_a, pf, gr_col, ga_col, *, tm, tn):
    m, k = y_rnn.shape
    n = w_r.shape[1]
    act = pl.BlockSpec((tm, k), lambda j, i: (i, 0))
    wsp = pl.BlockSpec((k, tn), lambda j, i: (0, j))
    return pl.pallas_call(
        _merge_kernel,
        grid=(n // tn, m // tm),
        in_specs=[act, act, wsp, wsp,
                  pl.BlockSpec((tm, tn), lambda j, i: (i, gr_col // tn + j)),
                  pl.BlockSpec((tm, tn), lambda j, i: (i, ga_col // tn + j))],
        out_specs=pl.BlockSpec((tm, tn), lambda j, i: (i, j)),
        out_shape=jax.ShapeDtypeStruct((m, n), BF16),
        compiler_params=_params(("parallel", "parallel")),
        name="merge",
    )(y_rnn, y_attn, w_r, w_a, pf, pf)


def _rglru_kernel(rx_ref, rg_ref, cw_ref, cb_ref, wax_ref, ba_ref, bx_ref, lam_ref,
                  y_ref, xs_sc, h_sc):
    ts, tc = rx_ref.shape
    ng = ts // SUBLANES

    @pl.when(pl.program_id(2) == 0)
    def _():
        xs_sc[0:SUBLANES, :] = jnp.zeros((SUBLANES, tc), F32)
        h_sc[...] = jnp.zeros_like(h_sc)

    x = rx_ref[...]
    xs_sc[SUBLANES:SUBLANES + ts, :] = x
    cw = cw_ref[...]
    xc = cb_ref[...]
    for k in range(CONV_WIDTH - 1):
        off = SUBLANES - (CONV_WIDTH - 1) + k
        xc = xc + xs_sc[off:off + ts, :] * cw[k:k + 1, :]
    xc = xc + x * cw[CONV_WIDTH - 1:CONV_WIDTH, :]
    xs_sc[0:SUBLANES, :] = x[ts - SUBLANES:ts, :]

    xcb = xc.astype(BF16)
    r_parts, i_parts = [], []
    for n in range(tc // RNN_BLOCK):
        g = jnp.dot(xcb[:, n * RNN_BLOCK:(n + 1) * RNN_BLOCK], wax_ref[n],
                    preferred_element_type=F32)
        r_parts.append(g[:, :RNN_BLOCK])
        i_parts.append(g[:, RNN_BLOCK:])
    tr = jnp.tanh(jnp.concatenate(r_parts, axis=1) + ba_ref[...])
    ti = jnp.tanh(jnp.concatenate(i_parts, axis=1) + bx_ref[...])

    nl = -lam_ref[...]
    softplus = jnp.maximum(nl, 0.0) + jnp.log1p(jnp.exp(-jnp.abs(nl)))
    log_a = (tr + 1.0) * ((-0.5 * RG_C) * softplus)
    a = jnp.exp(log_a)
    mult = jnp.sqrt(-jnp.tanh(log_a) * (a * a + 1.0))
    u = mult * ((0.5 * ti + 0.5) * xc)

    av = a.reshape(ng, SUBLANES, tc)
    bv = u.reshape(ng, SUBLANES, tc)
    ri = lax.broadcasted_iota(I32, (ng, SUBLANES, tc), 1)
    for d in (1, 2, 4):
        a_prev = jnp.where(ri >= d, pltpu.roll(av, d, axis=1), 1.0)
        b_prev = jnp.where(ri >= d, pltpu.roll(bv, d, axis=1), 0.0)
        bv = av * b_prev + bv
        av = av * a_prev
    carry = h_sc[0:1, :]
    hs = []
    for k in range(ng):
        hk = bv[k] + av[k] * carry
        hs.append(hk)
        carry = hk[SUBLANES - 1:SUBLANES, :]
    h_sc[0:1, :] = carry
    h = jnp.concatenate(hs, axis=0)

    gt = rg_ref[...]
    c1 = math.sqrt(2.0 / math.pi)
    inner = gt * (c1 + (c1 * 0.044715) * (gt * gt))
    y_ref[...] = ((h * (0.5 * gt)) * (1.0 + jnp.tanh(inner))).astype(y_ref.dtype)


def _rglru(pf, rx_col, rg_col, conv_w, conv_b, wax, b_a, b_x, lam, *, width, ts, tc):
    bsz, s, _ = pf.shape
    chan = lambda rows: pl.BlockSpec((rows, tc), lambda b, c, t: (0, c))
    return pl.pallas_call(
        _rglru_kernel,
        grid=(bsz, width // tc, s // ts),
        in_specs=[pl.BlockSpec((None, ts, tc), lambda b, c, t: (b, t, rx_col // tc + c)),
                  pl.BlockSpec((None, ts, tc), lambda b, c, t: (b, t, rg_col // tc + c)),
                  chan(CONV_WIDTH), chan(1),
                  pl.BlockSpec((tc // RNN_BLOCK, RNN_BLOCK, 2 * RNN_BLOCK),
                               lambda b, c, t: (c, 0, 0)),
                  chan(1), chan(1), chan(1)],
        out_specs=pl.BlockSpec((None, ts, tc), lambda b, c, t: (b, t, c)),
        out_shape=jax.ShapeDtypeStruct((bsz, s, width), BF16),
        scratch_shapes=[pltpu.VMEM((SUBLANES + ts, tc), F32),
                        pltpu.VMEM((SUBLANES, tc), F32)],
        compiler_params=_params(("parallel", "parallel", "arbitrary")),
        name="rglru",
    )(pf, pf, conv_w, conv_b, wax, b_a, b_x, lam)


def _bias_kernel(rb_ref, o_ref):
    h = pl.program_id(0)
    far = rb_ref[NUM_BUCKETS - 1, h]
    ii = lax.broadcasted_iota(I32, (KEY_BLOCK, KEY_BLOCK), 0)
    jj = lax.broadcasted_iota(I32, (KEY_BLOCK, KEY_BLOCK), 1)
    max_exact = NUM_BUCKETS // 2
    for d in range(2):
        n = jnp.maximum(ii - jj + KEY_BLOCK * d, 0)
        nf = jnp.maximum(n, 1).astype(F32)
        large = max_exact + (jnp.log(nf / max_exact) / math.log(MAX_DISTANCE / max_exact)
                             * (NUM_BUCKETS - max_exact)).astype(I32)
        large = jnp.minimum(large, NUM_BUCKETS - 1)
        bucket = jnp.where(n < max_exact, n, large)
        acc = jnp.zeros((KEY_BLOCK, KEY_BLOCK), F32)
        for b in range(NUM_BUCKETS):
            acc = jnp.where(bucket == b, rb_ref[b, h], acc)
        o_ref[0, d] = (acc - far) * LOG2E
    o_ref[0, 2] = jnp.zeros((KEY_BLOCK, KEY_BLOCK), F32)


def _bias_tiles(rel_bias):
    heads = rel_bias.shape[1]
    return pl.pallas_call(
        _bias_kernel,
        grid=(heads,),
        in_specs=[pl.BlockSpec(memory_space=pltpu.SMEM)],
        out_specs=pl.BlockSpec((1, 3, KEY_BLOCK, KEY_BLOCK), lambda h: (h, 0, 0, 0)),
        out_shape=jax.ShapeDtypeStruct((heads, 3, KEY_BLOCK, KEY_BLOCK), F32),
        compiler_params=_params(("parallel",)),
        name="bias_tiles",
    )(rel_bias)


SCORE_CHUNK = 4


def _attn_body(q_ref, k_ref, v_ref, iq_ref, ik_ref, iw_ref, bt_ref, y_ref,
               keys_sc, keyst_sc, am_sc, wb_sc, pb_sc, va_sc, *, k_top, nkb, class_blocks):
    heads, tq, _ = q_ref.shape
    groups = heads // KV_HEADS
    nk = nkb * KEY_BLOCK
    qi = pl.program_id(1)
    row = qi * tq + lax.broadcasted_iota(I32, (tq, KEY_BLOCK), 0)
    lane = lax.broadcasted_iota(I32, (tq, KEY_BLOCK), 1)

    w_scale = (IDX_HEADS ** -0.5) * (IDX_DIM ** -0.5)
    iw = iw_ref[...]
    for h in range(IDX_HEADS):
        col = iw[:, IDX_DIM + h:IDX_DIM + h + 1] * w_scale
        wb_sc[h] = jnp.broadcast_to(col, (tq, KEY_BLOCK))
    per_slab = LANES // IDX_DIM
    iq_rows = jnp.concatenate(
        [iq_ref[h // per_slab][:, (h % per_slab) * IDX_DIM:(h % per_slab + 1) * IDX_DIM]
         for h in range(IDX_HEADS)], axis=0)

    for c0 in range(0, nkb, SCORE_CHUNK):
        nb = min(SCORE_CHUNK, nkb - c0)
        ikc = ik_ref[c0 * KEY_BLOCK:(c0 + nb) * KEY_BLOCK, 0:IDX_DIM].astype(BF16)
        dots = lax.dot_general(iq_rows, ikc, (((1,), (1,)), ((), ())),
                               preferred_element_type=F32)
        for sub in range(nb):
            kb = c0 + sub
            acc = jnp.zeros((tq, KEY_BLOCK), F32)
            for h in range(IDX_HEADS):
                d = dots[h * tq:(h + 1) * tq, sub * KEY_BLOCK:(sub + 1) * KEY_BLOCK]
                acc = acc + jnp.maximum(d, 0.0) * wb_sc[h]
            score = jnp.where(kb * KEY_BLOCK + lane <= row, acc, -jnp.inf)
            bits = pltpu.bitcast(score, I32)
            key = bits ^ ((bits >> 31) & 0x7FFFFFFF)
            keys_sc[kb] = key
            keyst_sc[kb] = key.T.reshape(KEY_BLOCK // SUBLANES, SUBLANES, tq)

    def bit_step(it, thr):
        cand = thr + lax.shift_left(jnp.int32(1), 31 - it)
        cand8 = jnp.broadcast_to(cand, (SUBLANES, tq))[None]
        parts = [jnp.sum((keyst_sc[kb] >= cand8).astype(I32), axis=0) for kb in range(nkb)]
        total = jnp.sum(_tree_sum(parts), axis=0, keepdims=True)
        return jnp.where(total >= k_top, cand, thr)

    thr = lax.fori_loop(0, 32, bit_step, jnp.full((1, tq), INT32_MIN, I32))
    thr = jnp.maximum(thr, KEY_NEG_INF + 1)
    thr_b = jnp.broadcast_to(thr, (KEY_BLOCK, tq)).T
    for kb in range(nkb):
        am_sc[kb] = jnp.where(keys_sc[kb] >= thr_b, 0.0, NEG_LOGIT)

    scale2 = (HEAD_DIM ** -0.5) * LOG2E
    first_near = max(nkb - class_blocks - 1, 0)

    def logit(s, g, j, kb):
        sj = (s[j * tq:(j + 1) * tq, kb * KEY_BLOCK:(kb + 1) * KEY_BLOCK] + am_sc[kb])
        if kb >= first_near:
            sj = sj + bt_ref[g * groups + j, jnp.clip(qi - kb, 0, 2)]
        return sj

    def raw_logits(g):
        qg = q_ref[g * groups:(g + 1) * groups].reshape(groups * tq, HEAD_DIM)
        return lax.dot_general(qg, k_ref[g, 0:nk, :], (((1,), (1,)), ((), ())),
                               preferred_element_type=F32) * scale2

    s_next = raw_logits(0)
    for g in range(KV_HEADS):
        s = s_next
        if g + 1 < KV_HEADS:
            s_next = raw_logits(g + 1)
        for j in range(groups):
            m_run = logit(s, g, j, 0)
            for kb in range(1, nkb):
                m_run = jnp.maximum(m_run, logit(s, g, j, kb))
            m_b = jnp.broadcast_to(jnp.max(m_run, axis=1, keepdims=True), (tq, KEY_BLOCK))
            for kb in range(nkb):
                pb_sc[g, j * tq:(j + 1) * tq, kb * KEY_BLOCK:(kb + 1) * KEY_BLOCK] = (
                    jnp.exp2(logit(s, g, j, kb) - m_b).astype(BF16))
        o = jnp.dot(pb_sc[g, :, 0:nk], va_sc[g, 0:nk, :], preferred_element_type=F32)
        out = o[:, 0:HEAD_DIM] / o[:, HEAD_DIM:2 * HEAD_DIM]
        for j in range(groups):
            hd = (g * groups + j) * HEAD_DIM
            y_ref[:, hd:hd + HEAD_DIM] = out[j * tq:(j + 1) * tq].astype(y_ref.dtype)


def _attn_kernel(*refs, k_top, class_blocks, nq):
    qi = pl.program_id(1)
    v_ref, va_sc = refs[2], refs[-1]

    @pl.when(qi == 0)
    def _():
        va_sc[:, :, 0:HEAD_DIM] = v_ref[...]
        va_sc[:, :, HEAD_DIM:2 * HEAD_DIM] = jnp.ones(v_ref.shape, v_ref.dtype)

    for c in range(-(-nq // class_blocks)):
        nkb = min((c + 1) * class_blocks, nq)

        @pl.when(qi // class_blocks == c)
        def _(nkb=nkb):
            _attn_body(*refs, k_top=k_top, nkb=nkb, class_blocks=class_blocks)


def _attention(pb, ps, btab, *, bsz, heads, tq, class_blocks):
    s = ps.shape[1]
    nq = s // tq
    iq_slabs = IDX_HEADS * IDX_DIM // LANES
    assert HEAD_DIM == LANES and heads % KV_HEADS == 0 and heads % iq_slabs == 0
    assert pb.shape[0] == heads + 2 * KV_HEADS + iq_slabs
    nkb_max = s // KEY_BLOCK
    groups = heads // KV_HEADS
    k_top = min(TOPK_MAX, s // 4)
    assert tq == KEY_BLOCK and nq % class_blocks == 0
    return pl.pallas_call(
        functools.partial(_attn_kernel, k_top=k_top, class_blocks=class_blocks, nq=nq),
        grid=(bsz, nq),
        in_specs=[pl.BlockSpec((heads, tq, LANES), lambda b, i: (0, b * nq + i, 0)),
                  pl.BlockSpec((KV_HEADS, s, LANES), lambda b, i: (heads // KV_HEADS, b, 0)),
                  pl.BlockSpec((KV_HEADS, s, LANES), lambda b, i: (heads // KV_HEADS + 1, b, 0)),
                  pl.BlockSpec((iq_slabs, tq, LANES),
                               lambda b, i: ((heads + 2 * KV_HEADS) // iq_slabs, b * nq + i, 0)),
                  pl.BlockSpec((None, s, LANES), lambda b, i: (b, 0, 0)),
                  pl.BlockSpec((None, tq, LANES), lambda b, i: (b, i, 0)),
                  pl.BlockSpec(btab.shape, lambda b, i: (0, 0, 0, 0))],
        out_specs=pl.BlockSpec((tq, heads * HEAD_DIM), lambda b, i: (b * nq + i, 0)),
        out_shape=jax.ShapeDtypeStruct((bsz * s, heads * HEAD_DIM), BF16),
        scratch_shapes=[
            pltpu.VMEM((nkb_max, tq, KEY_BLOCK), I32),
            pltpu.VMEM((nkb_max, KEY_BLOCK // SUBLANES, SUBLANES, tq), I32),
            pltpu.VMEM((nkb_max, tq, KEY_BLOCK), F32),
            pltpu.VMEM((IDX_HEADS, tq, KEY_BLOCK), F32),
            pltpu.VMEM((KV_HEADS, groups * tq, s), BF16),
            pltpu.VMEM((KV_HEADS, s, 2 * HEAD_DIM), BF16)],
        compiler_params=_params(("parallel", "arbitrary")),
        name="sparse_attn",
    )(pb, pb, pb, pb, ps, ps, btab)


FFN_TM, FFN_TF = 512, 768
FFN_SUB = 256
MM_TM, MM_TN = 1024, 1024
IN_PROJ_TM = 1024
RGLRU_TS, RGLRU_TC = 512, 512
ATTN_CLASS_BLOCKS = 4


def _layer(x, ffn1_norm, ffn1_w_gate, ffn1_w_up, ffn1_w_down, mix_norm, w_in, conv_w, conv_b,
           rg_w_a, rg_b_a, rg_w_x, rg_b_x, rg_lambda, btab, w_proj_rnn, w_proj_attn, w_out,
           ffn2_norm, ffn2_w_gate, ffn2_w_up, ffn2_w_down, next_norm, *, last):
    bsz, s, d = x.shape
    m = bsz * s
    width = d
    q_width = d
    kv_width = KV_HEADS * HEAD_DIM
    iq_width = IDX_HEADS * IDX_DIM
    tm = min(FFN_TM, m)
    tmm = min(MM_TM, m)

    o_q = 2 * width
    o_ik = o_q + q_width + 2 * kv_width + iq_width
    o_gr = o_ik + IDX_DIM + IDX_HEADS

    bf = lambda w: w.astype(BF16)
    vec = lambda p: p.reshape(1, -1)

    x1, hn = _ffn(x.reshape(m, d), vec(ffn1_norm), bf(ffn1_w_gate), bf(ffn1_w_up), bf(ffn1_w_down),
                  vec(mix_norm), emit_resid=True, tm=tm, tf=FFN_TF)

    w_t = w_in.T
    tmi = min(IN_PROJ_TM, m)
    pf = _in_proj(hn, w_t, 0, o_q, F32, tm=tmi, tn=MM_TN, name="in_proj_rnn")
    pb = _in_proj(hn, w_t, o_q, o_ik - o_q, BF16, tm=tmi, tn=MM_TN, name="in_proj_attn",
                  lane_blocks=True)
    ps = _in_proj(hn, w_t, o_ik, LANES, F32, tm=tmi, tn=LANES, name="in_proj_idx")
    pg = _in_proj(hn, w_t, o_gr, 2 * d, F32, tm=tmi, tn=MM_TN, name="in_proj_gates")

    wax = bf(0.5 * jnp.concatenate([rg_w_a, rg_w_x], axis=-1))
    y_rnn = _rglru(pf.reshape(bsz, s, -1), 0, width, conv_w, vec(conv_b), wax,
                   vec(0.5 * rg_b_a), vec(0.5 * rg_b_x), vec(rg_lambda),
                   width=width, ts=min(RGLRU_TS, s), tc=RGLRU_TC)

    y_attn = _attention(pb, ps.reshape(bsz, s, LANES), btab, bsz=bsz, heads=q_width // HEAD_DIM,
                        tq=KEY_BLOCK, class_blocks=ATTN_CLASS_BLOCKS)

    merged = _merge(y_rnn.reshape(m, width), y_attn,
                    bf(w_proj_rnn), bf(w_proj_attn), pg, 0, d, tm=tm, tn=MM_TN)
    x2 = _matmul_residual(merged, bf(w_out), x1, tm=tmm, tn=MM_TN)

    outs = _ffn(x2, vec(ffn2_norm), bf(ffn2_w_gate), bf(ffn2_w_up), bf(ffn2_w_down),
                vec(next_norm), emit_resid=not last, tm=tm, tf=FFN_TF)
    return outs[0].reshape(bsz, s, d)


def kernel(x, ffn1_norm, ffn1_w_gate, ffn1_w_up, ffn1_w_down, mix_norm, w_in, conv_w, conv_b,
           rg_w_a, rg_b_a, rg_w_x, rg_b_x, rg_lambda, rel_bias, w_proj_rnn, w_proj_attn, w_out,
           ffn2_norm, ffn2_w_gate, ffn2_w_up, ffn2_w_down, final_norm):
    depth = ffn1_norm.shape[0]
    assert depth == 1, "the fused final norm assumes a single layer"
    btab = _bias_tiles(rel_bias)
    l = 0
    return _layer(x, ffn1_norm[l], ffn1_w_gate[l], ffn1_w_up[l], ffn1_w_down[l], mix_norm[l],
                  w_in[l], conv_w[l], conv_b[l], rg_w_a[l], rg_b_a[l], rg_w_x[l], rg_b_x[l],
                  rg_lambda[l], btab, w_proj_rnn[l], w_proj_attn[l], w_out[l], ffn2_norm[l],
                  ffn2_w_gate[l], ffn2_w_up[l], ffn2_w_down[l], final_norm, last=True)
```

```python
import functools
import math

import jax
import jax.numpy as jnp
from jax import lax
from jax.experimental import pallas as pl
from jax.experimental.pallas import tpu as pltpu

F32 = jnp.float32
BF16 = jnp.bfloat16
I32 = jnp.int32

RMS_EPS = 1e-6
CONV_WIDTH = 4
RG_C = 8.0
RNN_BLOCK = 128
HEAD_DIM = 128
KV_HEADS = 4
IDX_HEADS = 16
IDX_DIM = 64
TOPK_MAX = 256
NUM_BUCKETS = 32
MAX_DISTANCE = 128

LANES = 128
SUBLANES = 8
VMEM_LIMIT_BYTES = 56 * 1024 * 1024

KEY_BLOCK = 128
NEG_LOGIT = -1e30
INT32_MIN = -(2 ** 31)
KEY_NEG_INF = -2139095041
LOG2E = 1.4426950408889634


def _params(semantics):
    return pltpu.CompilerParams(dimension_semantics=semantics,
                                vmem_limit_bytes=VMEM_LIMIT_BYTES)


def _sigmoid(x):
    return 0.5 * (jnp.tanh(0.5 * x) + 1.0)


def _rms(x, g):
    ms = jnp.mean(x * x, axis=-1, keepdims=True)
    return x * lax.rsqrt(ms + RMS_EPS) * g


def _tree_sum(parts):
    while len(parts) > 1:
        parts = [a + b for a, b in zip(parts[::2], parts[1::2])] + (
            [parts[-1]] if len(parts) % 2 else [])
    return parts[0]


def _ffn_kernel(x_ref, g_ref, wg_ref, wu_ref, wd_ref, wgt_ref, wut_ref, wdt_ref, gn_ref, *rest,
                emit_resid, nfull):
    if emit_resid:
        acc_ref, hn_ref, xn_sc = rest
    else:
        acc_ref, xn_sc = rest
        hn_ref = acc_ref
    j = pl.program_id(1)

    @pl.when(j == 0)
    def _():
        xn_sc[...] = _rms(x_ref[...], g_ref[...]).astype(BF16)
        acc_ref[...] = jnp.zeros_like(acc_ref)

    def accumulate(wg, wu, wd):
        xn = xn_sc[...]
        h = jnp.dot(xn, wg, preferred_element_type=F32)
        u = jnp.dot(xn, wu, preferred_element_type=F32)
        a = (h * _sigmoid(h) * u).astype(BF16)
        acc_ref[...] += jnp.dot(a, wd, preferred_element_type=F32)

    @pl.when(j < nfull)
    def _():
        accumulate(wg_ref[...], wu_ref[...], wd_ref[...])

    @pl.when(j == nfull)
    def _():
        accumulate(wgt_ref[...], wut_ref[...], wdt_ref[...])
        y = x_ref[...] + 0.5 * acc_ref[...]
        if emit_resid:
            acc_ref[...] = y
        hn_ref[...] = _rms(y, gn_ref[...]).astype(hn_ref.dtype)


def _ffn(x, g, wg, wu, wd, gn, *, emit_resid, tm, tf):
    m, d = x.shape
    f = wg.shape[1]
    nfull = (f - 1) // tf
    tail = f - nfull * tf
    assert nfull >= 1 and tail % LANES == 0
    wgt, wut, wdt = wg[:, nfull * tf:], wu[:, nfull * tf:], wd[nfull * tf:, :]
    last = nfull - 1
    row = pl.BlockSpec((tm, d), lambda i, j: (i, 0))
    vec = pl.BlockSpec((1, d), lambda i, j: (0, 0))
    out_shape = [jax.ShapeDtypeStruct((m, d), F32)]
    out_specs = [row]
    if emit_resid:
        out_shape.append(jax.ShapeDtypeStruct((m, d), BF16))
        out_specs.append(row)
    return pl.pallas_call(
        functools.partial(_ffn_kernel, emit_resid=emit_resid, nfull=nfull),
        grid=(m // tm, nfull + 1),
        in_specs=[row, vec,
                  pl.BlockSpec((d, tf), lambda i, j: (0, jnp.minimum(j, last))),
                  pl.BlockSpec((d, tf), lambda i, j: (0, jnp.minimum(j, last))),
                  pl.BlockSpec((tf, d), lambda i, j: (jnp.minimum(j, last), 0)),
                  pl.BlockSpec((d, tail), lambda i, j: (0, 0)),
                  pl.BlockSpec((d, tail), lambda i, j: (0, 0)),
                  pl.BlockSpec((tail, d), lambda i, j: (0, 0)),
                  vec],
        out_specs=out_specs,
        out_shape=out_shape,
        scratch_shapes=[pltpu.VMEM((tm, d), BF16)],
        compiler_params=_params(("parallel", "arbitrary")),
        name="ffn_resid" if emit_resid else "ffn_final",
    )(x, g, wg, wu, wd, wgt, wut, wdt, gn)


def _in_proj_kernel(x_ref, wt_ref, o_ref, wb_sc, *, lane_blocks):
    @pl.when(pl.program_id(1) == 0)
    def _():
        wb_sc[...] = wt_ref[...].T.astype(BF16)

    res = jnp.dot(x_ref[...], wb_sc[...], preferred_element_type=F32)
    if lane_blocks:
        for c in range(res.shape[1] // LANES):
            o_ref[c] = res[:, c * LANES:(c + 1) * LANES].astype(o_ref.dtype)
    else:
        o_ref[...] = res.astype(o_ref.dtype)


def _in_proj(x, wt, row0, n, out_dtype, *, tm, tn, name, lane_blocks=False):
    m, k = x.shape
    assert row0 % SUBLANES == 0 and n % tn == 0
    if lane_blocks:
        out_shape = jax.ShapeDtypeStruct((n // LANES, m, LANES), out_dtype)
        out_spec = pl.BlockSpec((tn // LANES, tm, LANES), lambda j, i: (j, i, 0))
    else:
        out_shape = jax.ShapeDtypeStruct((m, n), out_dtype)
        out_spec = pl.BlockSpec((tm, tn), lambda j, i: (i, j))
    return pl.pallas_call(
        functools.partial(_in_proj_kernel, lane_blocks=lane_blocks),
        grid=(n // tn, m // tm),
        in_specs=[pl.BlockSpec((tm, k), lambda j, i: (i, 0)),
                  pl.BlockSpec((pl.Element(tn), pl.Element(k)),
                               lambda j, i: (pl.multiple_of(row0 + j * tn, SUBLANES), 0))],
        out_specs=out_spec,
        out_shape=out_shape,
        scratch_shapes=[pltpu.VMEM((k, tn), BF16)],
        compiler_params=_params(("parallel", "arbitrary")),
        name=name,
    )(x, wt)


def _mm_res_kernel(x_ref, w_ref, r_ref, o_ref):
    o_ref[...] = r_ref[...] + jnp.dot(x_ref[...], w_ref[...], preferred_element_type=F32)


def _matmul_residual(x, w, r, *, tm, tn):
    m, k = x.shape
    n = w.shape[1]
    return pl.pallas_call(
        _mm_res_kernel,
        grid=(n // tn, m // tm),
        in_specs=[pl.BlockSpec((tm, k), lambda j, i: (i, 0)),
                  pl.BlockSpec((k, tn), lambda j, i: (0, j)),
                  pl.BlockSpec((tm, tn), lambda j, i: (i, j))],
        out_specs=pl.BlockSpec((tm, tn), lambda j, i: (i, j)),
        out_shape=jax.ShapeDtypeStruct((m, n), F32),
        compiler_params=_params(("parallel", "parallel")),
        name="out_proj",
    )(x, w, r)


def _merge_kernel(yr_ref, ya_ref, wr_ref, wa_ref, gr_ref, ga_ref, o_ref):
    pr = jnp.dot(yr_ref[...], wr_ref[...], preferred_element_type=F32)
    pa = jnp.dot(ya_ref[...], wa_ref[...], preferred_element_type=F32)
    o_ref[...] = (_sigmoid(gr_ref[...]) * pr + _sigmoid(ga_ref[...]) * pa).astype(o_ref.dtype)


def _merge(y_rnn, y_attn, w_r, w_a, pf, gr_col, ga_col, *, tm, tn):
    m, k = y_rnn.shape
    n = w_r.shape[1]
    act = pl.BlockSpec((tm, k), lambda j, i: (i, 0))
    wsp = pl.BlockSpec((k, tn), lambda j, i: (0, j))
    return pl.pallas_call(
        _merge_kernel,
        grid=(n // tn, m // tm),
        in_specs=[act, act, wsp, wsp,
                  pl.BlockSpec((tm, tn), lambda j, i: (i, gr_col // tn + j)),
                  pl.BlockSpec((tm, tn), lambda j, i: (i, ga_col // tn + j))],
        out_specs=pl.BlockSpec((tm, tn), lambda j, i: (i, j)),
        out_shape=jax.ShapeDtypeStruct((m, n), BF16),
        compiler_params=_params(("parallel", "parallel")),
        name="merge",
    )(y_rnn, y_attn, w_r, w_a, pf, pf)


def _rglru_kernel(rx_ref, rg_ref, cw_ref, cb_ref, wax_ref, ba_ref, bx_ref, lam_ref,
                  y_ref, xs_sc, h_sc):
    ts, tc = rx_ref.shape
    ng = ts // SUBLANES

    @pl.when(pl.program_id(2) == 0)
    def _():
        xs_sc[0:SUBLANES, :] = jnp.zeros((SUBLANES, tc), F32)
        h_sc[...] = jnp.zeros_like(h_sc)

    x = rx_ref[...]
    xs_sc[SUBLANES:SUBLANES + ts, :] = x
    cw = cw_ref[...]
    xc = cb_ref[...]
    for k in range(CONV_WIDTH - 1):
        off = SUBLANES - (CONV_WIDTH - 1) + k
        xc = xc + xs_sc[off:off + ts, :] * cw[k:k + 1, :]
    xc = xc + x * cw[CONV_WIDTH - 1:CONV_WIDTH, :]
    xs_sc[0:SUBLANES, :] = x[ts - SUBLANES:ts, :]

    xcb = xc.astype(BF16)
    r_parts, i_parts = [], []
    for n in range(tc // RNN_BLOCK):
        g = jnp.dot(xcb[:, n * RNN_BLOCK:(n + 1) * RNN_BLOCK], wax_ref[n],
                    preferred_element_type=F32)
        r_parts.append(g[:, :RNN_BLOCK])
        i_parts.append(g[:, RNN_BLOCK:])
    tr = jnp.tanh(jnp.concatenate(r_parts, axis=1) + ba_ref[...])
    ti = jnp.tanh(jnp.concatenate(i_parts, axis=1) + bx_ref[...])

    nl = -lam_ref[...]
    softplus = jnp.maximum(nl, 0.0) + jnp.log1p(jnp.exp(-jnp.abs(nl)))
    log_a = (tr + 1.0) * ((-0.5 * RG_C) * softplus)
    a = jnp.exp(log_a)
    mult = jnp.sqrt(-jnp.tanh(log_a) * (a * a + 1.0))
    u = mult * ((0.5 * ti + 0.5) * xc)

    av = a.reshape(ng, SUBLANES, tc)
    bv = u.reshape(ng, SUBLANES, tc)
    ri = lax.broadcasted_iota(I32, (ng, SUBLANES, tc), 1)
    for d in (1, 2, 4):
        a_prev = jnp.where(ri >= d, pltpu.roll(av, d, axis=1), 1.0)
        b_prev = jnp.where(ri >= d, pltpu.roll(bv, d, axis=1), 0.0)
        bv = av * b_prev + bv
        av = av * a_prev
    carry = h_sc[0:1, :]
    hs = []
    for k in range(ng):
        hk = bv[k] + av[k] * carry
        hs.append(hk)
        carry = hk[SUBLANES - 1:SUBLANES, :]
    h_sc[0:1, :] = carry
    h = jnp.concatenate(hs, axis=0)

    gt = rg_ref[...]
    c1 = math.sqrt(2.0 / math.pi)
    inner = gt * (c1 + (c1 * 0.044715) * (gt * gt))
    y_ref[...] = ((h * (0.5 * gt)) * (1.0 + jnp.tanh(inner))).astype(y_ref.dtype)


def _rglru(pf, rx_col, rg_col, conv_w, conv_b, wax, b_a, b_x, lam, *, width, ts, tc):
    bsz, s, _ = pf.shape
    chan = lambda rows: pl.BlockSpec((rows, tc), lambda b, c, t: (0, c))
    return pl.pallas_call(
        _rglru_kernel,
        grid=(bsz, width // tc, s // ts),
        in_specs=[pl.BlockSpec((None, ts, tc), lambda b, c, t: (b, t, rx_col // tc + c)),
                  pl.BlockSpec((None, ts, tc), lambda b, c, t: (b, t, rg_col // tc + c)),
                  chan(CONV_WIDTH), chan(1),
                  pl.BlockSpec((tc // RNN_BLOCK, RNN_BLOCK, 2 * RNN_BLOCK),
                               lambda b, c, t: (c, 0, 0)),
                  chan(1), chan(1), chan(1)],
        out_specs=pl.BlockSpec((None, ts, tc), lambda b, c, t: (b, t, c)),
        out_shape=jax.ShapeDtypeStruct((bsz, s, width), BF16),
        scratch_shapes=[pltpu.VMEM((SUBLANES + ts, tc), F32),
                        pltpu.VMEM((SUBLANES, tc), F32)],
        compiler_params=_params(("parallel", "parallel", "arbitrary")),
        name="rglru",
    )(pf, pf, conv_w, conv_b, wax, b_a, b_x, lam)


def _bias_kernel(rb_ref, o_ref):
    h = pl.program_id(0)
    far = rb_ref[NUM_BUCKETS - 1, h]
    ii = lax.broadcasted_iota(I32, (KEY_BLOCK, KEY_BLOCK), 0)
    jj = lax.broadcasted_iota(I32, (KEY_BLOCK, KEY_BLOCK), 1)
    max_exact = NUM_BUCKETS // 2
    for d in range(2):
        n = jnp.maximum(ii - jj + KEY_BLOCK * d, 0)
        nf = jnp.maximum(n, 1).astype(F32)
        large = max_exact + (jnp.log(nf / max_exact) / math.log(MAX_DISTANCE / max_exact)
                             * (NUM_BUCKETS - max_exact)).astype(I32)
        large = jnp.minimum(large, NUM_BUCKETS - 1)
        bucket = jnp.where(n < max_exact, n, large)
        acc = jnp.zeros((KEY_BLOCK, KEY_BLOCK), F32)
        for b in range(NUM_BUCKETS):
            acc = jnp.where(bucket == b, rb_ref[b, h], acc)
        o_ref[0, d] = (acc - far) * LOG2E
    o_ref[0, 2] = jnp.zeros((KEY_BLOCK, KEY_BLOCK), F32)


def _bias_tiles(rel_bias):
    heads = rel_bias.shape[1]
    return pl.pallas_call(
        _bias_kernel,
        grid=(heads,),
        in_specs=[pl.BlockSpec(memory_space=pltpu.SMEM)],
        out_specs=pl.BlockSpec((1, 3, KEY_BLOCK, KEY_BLOCK), lambda h: (h, 0, 0, 0)),
        out_shape=jax.ShapeDtypeStruct((heads, 3, KEY_BLOCK, KEY_BLOCK), F32),
        compiler_params=_params(("parallel",)),
        name="bias_tiles",
    )(rel_bias)


SCORE_CHUNK = 4


def _attn_body(q_ref, k_ref, v_ref, iq_ref, ik_ref, iw_ref, bt_ref, y_ref,
               keys_sc, keyst_sc, am_sc, wb_sc, pb_sc, va_sc, *, k_top, nkb, class_blocks):
    heads, tq, _ = q_ref.shape
    groups = heads // KV_HEADS
    nk = nkb * KEY_BLOCK
    qi = pl.program_id(1)
    row = qi * tq + lax.broadcasted_iota(I32, (tq, KEY_BLOCK), 0)
    lane = lax.broadcasted_iota(I32, (tq, KEY_BLOCK), 1)

    w_scale = (IDX_HEADS ** -0.5) * (IDX_DIM ** -0.5)
    iw = iw_ref[...]
    for h in range(IDX_HEADS):
        col = iw[:, IDX_DIM + h:IDX_DIM + h + 1] * w_scale
        wb_sc[h] = jnp.broadcast_to(col, (tq, KEY_BLOCK))
    per_slab = LANES // IDX_DIM
    iq_rows = jnp.concatenate(
        [iq_ref[h // per_slab][:, (h % per_slab) * IDX_DIM:(h % per_slab + 1) * IDX_DIM]
         for h in range(IDX_HEADS)], axis=0)

    for c0 in range(0, nkb, SCORE_CHUNK):
        nb = min(SCORE_CHUNK, nkb - c0)
        ikc = ik_ref[c0 * KEY_BLOCK:(c0 + nb) * KEY_BLOCK, 0:IDX_DIM].astype(BF16)
        dots = lax.dot_general(iq_rows, ikc, (((1,), (1,)), ((), ())),
                               preferred_element_type=F32)
        for sub in range(nb):
            kb = c0 + sub
            acc = jnp.zeros((tq, KEY_BLOCK), F32)
            for h in range(IDX_HEADS):
                d = dots[h * tq:(h + 1) * tq, sub * KEY_BLOCK:(sub + 1) * KEY_BLOCK]
                acc = acc + jnp.maximum(d, 0.0) * wb_sc[h]
            score = jnp.where(kb * KEY_BLOCK + lane <= row, acc, -jnp.inf)
            bits = pltpu.bitcast(score, I32)
            key = bits ^ ((bits >> 31) & 0x7FFFFFFF)
            keys_sc[kb] = key
            keyst_sc[kb] = key.T.reshape(KEY_BLOCK // SUBLANES, SUBLANES, tq)

    def count_where(pred, ref8):
        parts = [jnp.sum(pred(keyst_sc[kb], ref8).astype(I32), axis=0) for kb in range(nkb)]
        return jnp.sum(_tree_sum(parts), axis=0, keepdims=True)

    def bit_step(it, carry):
        thr, cnt = carry
        cand = thr + lax.shift_left(jnp.int32(1), 31 - it)
        total = count_where(jnp.greater_equal, jnp.broadcast_to(cand, (SUBLANES, tq))[None])
        take = total >= k_top
        return jnp.where(take, cand, thr), jnp.where(take, total, cnt)

    thr, cnt = lax.fori_loop(0, 32, bit_step, (jnp.full((1, tq), INT32_MIN, I32),
                                               jnp.full((1, tq), nk, I32)))
    full_row = thr > KEY_NEG_INF
    thr = jnp.maximum(thr, KEY_NEG_INF + 1)
    has_ties = jnp.max(jnp.where(full_row, cnt, 0)) > k_top

    @pl.when(jnp.logical_not(has_ties))
    def _():
        thr_b = jnp.broadcast_to(thr, (KEY_BLOCK, tq)).T
        for kb in range(nkb):
            am_sc[kb] = jnp.where(keys_sc[kb] >= thr_b, 0.0, NEG_LOGIT)

    @pl.when(has_ties)
    def _():
        thr_t = jnp.broadcast_to(thr, (KEY_BLOCK, tq))
        n_eq = count_where(jnp.equal, jnp.broadcast_to(thr, (SUBLANES, tq))[None])
        keep = jnp.where(full_row, k_top - (cnt - n_eq), nk).astype(F32)
        keep_t = jnp.broadcast_to(keep, (KEY_BLOCK, tq))
        ri = lax.broadcasted_iota(I32, (KEY_BLOCK, KEY_BLOCK), 0)
        ci = lax.broadcasted_iota(I32, (KEY_BLOCK, KEY_BLOCK), 1)
        lower = jnp.where(ri >= ci, 1.0, 0.0).astype(BF16)
        seen = jnp.zeros((1, tq), F32)
        for kb in range(nkb):
            kt = keyst_sc[kb].reshape(KEY_BLOCK, tq)
            eq = jnp.where(kt == thr_t, 1.0, 0.0)
            rank = jnp.dot(lower, eq.astype(BF16), preferred_element_type=F32) + seen
            seen = rank[KEY_BLOCK - 1:KEY_BLOCK, :]
            tie_ok = jnp.where(rank <= keep_t, eq, 0.0)
            sel = jnp.where(kt > thr_t, 1.0, tie_ok)
            am_sc[kb] = jnp.where(sel.T > 0.5, 0.0, NEG_LOGIT)

    scale2 = (HEAD_DIM ** -0.5) * LOG2E
    first_near = max(nkb - class_blocks - 1, 0)

    def logit(s, g, j, kb):
        sj = (s[j * tq:(j + 1) * tq, kb * KEY_BLOCK:(kb + 1) * KEY_BLOCK] + am_sc[kb])
        if kb >= first_near:
            sj = sj + bt_ref[g * groups + j, jnp.clip(qi - kb, 0, 2)]
        return sj

    def raw_logits(g):
        qg = q_ref[g * groups:(g + 1) * groups].reshape(groups * tq, HEAD_DIM)
        return lax.dot_general(qg, k_ref[g, 0:nk, :], (((1,), (1,)), ((), ())),
                               preferred_element_type=F32) * scale2

    s_next = raw_logits(0)
    for g in range(KV_HEADS):
        s = s_next
        if g + 1 < KV_HEADS:
            s_next = raw_logits(g + 1)
        for j in range(groups):
            m_run = logit(s, g, j, 0)
            for kb in range(1, nkb):
                m_run = jnp.maximum(m_run, logit(s, g, j, kb))
            m_b = jnp.broadcast_to(jnp.max(m_run, axis=1, keepdims=True), (tq, KEY_BLOCK))
            for kb in range(nkb):
                pb_sc[g, j * tq:(j + 1) * tq, kb * KEY_BLOCK:(kb + 1) * KEY_BLOCK] = (
                    jnp.exp2(logit(s, g, j, kb) - m_b).astype(BF16))
        o = jnp.dot(pb_sc[g, :, 0:nk], va_sc[g, 0:nk, :], preferred_element_type=F32)
        out = o[:, 0:HEAD_DIM] / o[:, HEAD_DIM:2 * HEAD_DIM]
        for j in range(groups):
            hd = (g * groups + j) * HEAD_DIM
            y_ref[:, hd:hd + HEAD_DIM] = out[j * tq:(j + 1) * tq].astype(y_ref.dtype)


def _attn_kernel(*refs, k_top, class_blocks, nq):
    qi = pl.program_id(1)
    v_ref, va_sc = refs[2], refs[-1]

    @pl.when(qi == 0)
    def _():
        va_sc[:, :, 0:HEAD_DIM] = v_ref[...]
        va_sc[:, :, HEAD_DIM:2 * HEAD_DIM] = jnp.ones(v_ref.shape, v_ref.dtype)

    for c in range(-(-nq // class_blocks)):
        nkb = min((c + 1) * class_blocks, nq)

        @pl.when(qi // class_blocks == c)
        def _(nkb=nkb):
            _attn_body(*refs, k_top=k_top, nkb=nkb, class_blocks=class_blocks)


def _attention(pb, ps, btab, *, bsz, heads, tq, class_blocks):
    s = ps.shape[1]
    nq = s // tq
    iq_slabs = IDX_HEADS * IDX_DIM // LANES
    assert HEAD_DIM == LANES and heads % KV_HEADS == 0 and heads % iq_slabs == 0
    assert pb.shape[0] == heads + 2 * KV_HEADS + iq_slabs
    nkb_max = s // KEY_BLOCK
    groups = heads // KV_HEADS
    k_top = min(TOPK_MAX, s // 4)
    assert tq == KEY_BLOCK and nq % class_blocks == 0
    return pl.pallas_call(
        functools.partial(_attn_kernel, k_top=k_top, class_blocks=class_blocks, nq=nq),
        grid=(bsz, nq),
        in_specs=[pl.BlockSpec((heads, tq, LANES), lambda b, i: (0, b * nq + i, 0)),
                  pl.BlockSpec((KV_HEADS, s, LANES), lambda b, i: (heads // KV_HEADS, b, 0)),
                  pl.BlockSpec((KV_HEADS, s, LANES), lambda b, i: (heads // KV_HEADS + 1, b, 0)),
                  pl.BlockSpec((iq_slabs, tq, LANES),
                               lambda b, i: ((heads + 2 * KV_HEADS) // iq_slabs, b * nq + i, 0)),
                  pl.BlockSpec((None, s, LANES), lambda b, i: (b, 0, 0)),
                  pl.BlockSpec((None, tq, LANES), lambda b, i: (b, i, 0)),
                  pl.BlockSpec(btab.shape, lambda b, i: (0, 0, 0, 0))],
        out_specs=pl.BlockSpec((tq, heads * HEAD_DIM), lambda b, i: (b * nq + i, 0)),
        out_shape=jax.ShapeDtypeStruct((bsz * s, heads * HEAD_DIM), BF16),
        scratch_shapes=[
            pltpu.VMEM((nkb_max, tq, KEY_BLOCK), I32),
            pltpu.VMEM((nkb_max, KEY_BLOCK // SUBLANES, SUBLANES, tq), I32),
            pltpu.VMEM((nkb_max, tq, KEY_BLOCK), F32),
            pltpu.VMEM((IDX_HEADS, tq, KEY_BLOCK), F32),
            pltpu.VMEM((KV_HEADS, groups * tq, s), BF16),
            pltpu.VMEM((KV_HEADS, s, 2 * HEAD_DIM), BF16)],
        compiler_params=_params(("parallel", "arbitrary")),
        name="sparse_attn",
    )(pb, pb, pb, pb, ps, ps, btab)


FFN_TM, FFN_TF = 512, 512
MM_TM, MM_TN = 1024, 1024
RGLRU_TS, RGLRU_TC = 512, 512
ATTN_CLASS_BLOCKS = 4


def _layer(x, ffn1_norm, ffn1_w_gate, ffn1_w_up, ffn1_w_down, mix_norm, w_in, conv_w, conv_b,
           rg_w_a, rg_b_a, rg_w_x, rg_b_x, rg_lambda, btab, w_proj_rnn, w_proj_attn, w_out,
           ffn2_norm, ffn2_w_gate, ffn2_w_up, ffn2_w_down, next_norm, *, last):
    bsz, s, d = x.shape
    m = bsz * s
    width = d
    q_width = d
    kv_width = KV_HEADS * HEAD_DIM
    iq_width = IDX_HEADS * IDX_DIM
    tm = min(FFN_TM, m)
    tmm = min(MM_TM, m)

    o_q = 2 * width
    o_ik = o_q + q_width + 2 * kv_width + iq_width
    o_gr = o_ik + IDX_DIM + IDX_HEADS

    bf = lambda w: w.astype(BF16)
    vec = lambda p: p.reshape(1, -1)

    x1, hn = _ffn(x.reshape(m, d), vec(ffn1_norm), bf(ffn1_w_gate), bf(ffn1_w_up), bf(ffn1_w_down),
                  vec(mix_norm), emit_resid=True, tm=tm, tf=FFN_TF)

    w_t = w_in.T
    tmi = tmm
    pf = _in_proj(hn, w_t, 0, o_q, F32, tm=tmi, tn=MM_TN, name="in_proj_rnn")
    pb = _in_proj(hn, w_t, o_q, o_ik - o_q, BF16, tm=tmi, tn=MM_TN, name="in_proj_attn",
                  lane_blocks=True)
    ps = _in_proj(hn, w_t, o_ik, LANES, F32, tm=tmi, tn=LANES, name="in_proj_idx")
    pg = _in_proj(hn, w_t, o_gr, 2 * d, F32, tm=tmi, tn=MM_TN, name="in_proj_gates")

    wax = bf(0.5 * jnp.concatenate([rg_w_a, rg_w_x], axis=-1))
    y_rnn = _rglru(pf.reshape(bsz, s, -1), 0, width, conv_w, vec(conv_b), wax,
                   vec(0.5 * rg_b_a), vec(0.5 * rg_b_x), vec(rg_lambda),
                   width=width, ts=min(RGLRU_TS, s), tc=RGLRU_TC)

    y_attn = _attention(pb, ps.reshape(bsz, s, LANES), btab, bsz=bsz, heads=q_width // HEAD_DIM,
                        tq=KEY_BLOCK, class_blocks=ATTN_CLASS_BLOCKS)

    merged = _merge(y_rnn.reshape(m, width), y_attn,
                    bf(w_proj_rnn), bf(w_proj_attn), pg, 0, d, tm=tm, tn=MM_TN)
    x2 = _matmul_residual(merged, bf(w_out), x1, tm=tmm, tn=MM_TN)

    outs = _ffn(x2, vec(ffn2_norm), bf(ffn2_w_gate), bf(ffn2_w_up), bf(ffn2_w_down),
                vec(next_norm), emit_resid=not last, tm=tm, tf=FFN_TF)
    return outs[0].reshape(bsz, s, d)


def kernel(x, ffn1_norm, ffn1_w_gate, ffn1_w_up, ffn1_w_down, mix_norm, w_in, conv_w, conv_b,
           rg_w_a, rg_b_a, rg_w_x, rg_b_x, rg_lambda, rel_bias, w_proj_rnn, w_proj_attn, w_out,
           ffn2_norm, ffn2_w_gate, ffn2_w_up, ffn2_w_down, final_norm):
    depth = ffn1_norm.shape[0]
    assert depth == 1, "the fused final norm assumes a single layer"
    btab = _bias_tiles(rel_bias)
    l = 0
    return _layer(x, ffn1_norm[l], ffn1_w_gate[l], ffn1_w_up[l], ffn1_w_down[l], mix_norm[l],
                  w_in[l], conv_w[l], conv_b[l], rg_w_a[l], rg_b_a[l], rg_w_x[l], rg_b_x[l],
                  rg_lambda[l], btab, w_proj_rnn[l], w_proj_attn[l], w_out[l], ffn2_norm[l],
                  ffn2_w_gate[l], ffn2_w_up[l], ffn2_w_down[l], final_norm, last=True)
```

```python
import functools
import math

import jax
import jax.numpy as jnp
from jax import lax
from jax.experimental import pallas as pl
from jax.experimental.pallas import tpu as pltpu

F32 = jnp.float32
BF16 = jnp.bfloat16
I32 = jnp.int32

RMS_EPS = 1e-6
CONV_WIDTH = 4
RG_C = 8.0
RNN_BLOCK = 128
HEAD_DIM = 128
KV_HEADS = 4
IDX_HEADS = 16
IDX_DIM = 64
TOPK_MAX = 256
NUM_BUCKETS = 32
MAX_DISTANCE = 128

LANES = 128
SUBLANES = 8
VMEM_LIMIT_BYTES = 56 * 1024 * 1024

KEY_BLOCK = 128
NEG_LOGIT = -1e30
INT32_MIN = -(2 ** 31)
KEY_NEG_INF = -2139095041
LOG2E = 1.4426950408889634


def _params(semantics):
    return pltpu.CompilerParams(dimension_semantics=semantics,
                                vmem_limit_bytes=VMEM_LIMIT_BYTES)


def _sigmoid(x):
    return 0.5 * (jnp.tanh(0.5 * x) + 1.0)


def _rms(x, g):
    ms = jnp.mean(x * x, axis=-1, keepdims=True)
    return x * lax.rsqrt(ms + RMS_EPS) * g


def _tree_sum(parts):
    while len(parts) > 1:
        parts = [a + b for a, b in zip(parts[::2], parts[1::2])] + (
            [parts[-1]] if len(parts) % 2 else [])
    return parts[0]


def _ffn_kernel(x_ref, g_ref, wg_ref, wu_ref, wd_ref, wgt_ref, wut_ref, wdt_ref, gn_ref, *rest,
                emit_resid, nfull):
    if emit_resid:
        acc_ref, hn_ref, xn_sc = rest
    else:
        acc_ref, xn_sc = rest
        hn_ref = acc_ref
    j = pl.program_id(1)

    @pl.when(j == 0)
    def _():
        xn_sc[...] = _rms(x_ref[...], g_ref[...]).astype(BF16)
        acc_ref[...] = jnp.zeros_like(acc_ref)

    def accumulate(wg, wu, wd):
        xn = xn_sc[...]
        h = jnp.dot(xn, wg, preferred_element_type=F32)
        u = jnp.dot(xn, wu, preferred_element_type=F32)
        a = (h * _sigmoid(h) * u).astype(BF16)
        acc_ref[...] += jnp.dot(a, wd, preferred_element_type=F32)

    @pl.when(j < nfull)
    def _():
        accumulate(wg_ref[...], wu_ref[...], wd_ref[...])

    @pl.when(j == nfull)
    def _():
        accumulate(wgt_ref[...], wut_ref[...], wdt_ref[...])
        y = x_ref[...] + 0.5 * acc_ref[...]
        if emit_resid:
            acc_ref[...] = y
        hn_ref[...] = _rms(y, gn_ref[...]).astype(hn_ref.dtype)


def _ffn(x, g, wg, wu, wd, gn, *, emit_resid, tm, tf):
    m, d = x.shape
    f = wg.shape[1]
    nfull = (f - 1) // tf
    tail = f - nfull * tf
    assert nfull >= 1 and tail % LANES == 0
    wgt, wut, wdt = wg[:, nfull * tf:], wu[:, nfull * tf:], wd[nfull * tf:, :]
    last = nfull - 1
    stream = pl.Buffered(2)
    fixed = pl.Buffered(1)
    row = pl.BlockSpec((tm, d), lambda i, j: (i, 0))
    vec = pl.BlockSpec((1, d), lambda i, j: (0, 0))
    out_shape = [jax.ShapeDtypeStruct((m, d), F32)]
    out_specs = [row]
    if emit_resid:
        out_shape.append(jax.ShapeDtypeStruct((m, d), BF16))
        out_specs.append(row)
    return pl.pallas_call(
        functools.partial(_ffn_kernel, emit_resid=emit_resid, nfull=nfull),
        grid=(m // tm, nfull + 1),
        in_specs=[row, vec,
                  pl.BlockSpec((d, tf), lambda i, j: (0, jnp.minimum(j, last)), pipeline_mode=stream),
                  pl.BlockSpec((d, tf), lambda i, j: (0, jnp.minimum(j, last)), pipeline_mode=stream),
                  pl.BlockSpec((tf, d), lambda i, j: (jnp.minimum(j, last), 0), pipeline_mode=stream),
                  pl.BlockSpec((d, tail), lambda i, j: (0, 0), pipeline_mode=fixed),
                  pl.BlockSpec((d, tail), lambda i, j: (0, 0), pipeline_mode=fixed),
                  pl.BlockSpec((tail, d), lambda i, j: (0, 0), pipeline_mode=fixed),
                  vec],
        out_specs=out_specs,
        out_shape=out_shape,
        scratch_shapes=[pltpu.VMEM((tm, d), BF16)],
        compiler_params=_params(("parallel", "arbitrary")),
        name="ffn_resid" if emit_resid else "ffn_final",
    )(x, g, wg, wu, wd, wgt, wut, wdt, gn)


def _in_proj_kernel(x_ref, wt_ref, o_ref, wb_sc, *, lane_blocks):
    @pl.when(pl.program_id(1) == 0)
    def _():
        wb_sc[...] = wt_ref[...].T.astype(BF16)

    res = jnp.dot(x_ref[...], wb_sc[...], preferred_element_type=F32)
    if lane_blocks:
        for c in range(res.shape[1] // LANES):
            o_ref[c] = res[:, c * LANES:(c + 1) * LANES].astype(o_ref.dtype)
    else:
        o_ref[...] = res.astype(o_ref.dtype)


def _in_proj(x, wt, row0, n, out_dtype, *, tm, tn, name, lane_blocks=False):
    m, k = x.shape
    assert row0 % SUBLANES == 0 and n % tn == 0
    if lane_blocks:
        out_shape = jax.ShapeDtypeStruct((n // LANES, m, LANES), out_dtype)
        out_spec = pl.BlockSpec((tn // LANES, tm, LANES), lambda j, i: (j, i, 0))
    else:
        out_shape = jax.ShapeDtypeStruct((m, n), out_dtype)
        out_spec = pl.BlockSpec((tm, tn), lambda j, i: (i, j))
    return pl.pallas_call(
        functools.partial(_in_proj_kernel, lane_blocks=lane_blocks),
        grid=(n // tn, m // tm),
        in_specs=[pl.BlockSpec((tm, k), lambda j, i: (i, 0)),
                  pl.BlockSpec((pl.Element(tn), pl.Element(k)),
                               lambda j, i: (pl.multiple_of(row0 + j * tn, SUBLANES), 0))],
        out_specs=out_spec,
        out_shape=out_shape,
        scratch_shapes=[pltpu.VMEM((k, tn), BF16)],
        compiler_params=_params(("parallel", "arbitrary")),
        name=name,
    )(x, wt)


def _mm_res_kernel(x_ref, w_ref, r_ref, o_ref):
    o_ref[...] = r_ref[...] + jnp.dot(x_ref[...], w_ref[...], preferred_element_type=F32)


def _matmul_residual(x, w, r, *, tm, tn):
    m, k = x.shape
    n = w.shape[1]
    return pl.pallas_call(
        _mm_res_kernel,
        grid=(n // tn, m // tm),
        in_specs=[pl.BlockSpec((tm, k), lambda j, i: (i, 0)),
                  pl.BlockSpec((k, tn), lambda j, i: (0, j)),
                  pl.BlockSpec((tm, tn), lambda j, i: (i, j))],
        out_specs=pl.BlockSpec((tm, tn), lambda j, i: (i, j)),
        out_shape=jax.ShapeDtypeStruct((m, n), F32),
        compiler_params=_params(("parallel", "parallel")),
        name="out_proj",
    )(x, w, r)


def _merge_kernel(yr_ref, ya_ref, wr_ref, wa_ref, gr_ref, ga_ref, o_ref):
    pr = jnp.dot(yr_ref[...], wr_ref[...], preferred_element_type=F32)
    pa = jnp.dot(ya_ref[...], wa_ref[...], preferred_element_type=F32)
    o_ref[...] = (_sigmoid(gr_ref[...]) * pr + _sigmoid(ga_ref[...]) * pa).astype(o_ref.dtype)


def _merge(y_rnn, y_attn, w_r, w_a, pf, gr_col, ga_col, *, tm, tn):
    m, k = y_rnn.shape
    n = w_r.shape[1]
    act = pl.BlockSpec((tm, k), lambda j, i: (i, 0))
    wsp = pl.BlockSpec((k, tn), lambda j, i: (0, j))
    return pl.pallas_call(
        _merge_kernel,
        grid=(n // tn, m // tm),
        in_specs=[act, act, wsp, wsp,
                  pl.BlockSpec((tm, tn), lambda j, i: (i, gr_col // tn + j)),
                  pl.BlockSpec((tm, tn), lambda j, i: (i, ga_col // tn + j))],
        out_specs=pl.BlockSpec((tm, tn), lambda j, i: (i, j)),
        out_shape=jax.ShapeDtypeStruct((m, n), BF16),
        compiler_params=_params(("parallel", "parallel")),
        name="merge",
    )(y_rnn, y_attn, w_r, w_a, pf, pf)


def _rglru_kernel(rx_ref, rg_ref, cw_ref, cb_ref, wax_ref, ba_ref, bx_ref, lam_ref,
                  y_ref, xs_sc, h_sc):
    ts, tc = rx_ref.shape
    ng = ts // SUBLANES

    @pl.when(pl.program_id(2) == 0)
    def _():
        xs_sc[...] = jnp.zeros_like(xs_sc)
        h_sc[...] = jnp.zeros_like(h_sc)

    ri = lax.broadcasted_iota(I32, (1, SUBLANES, tc), 1)
    x = rx_ref[...]
    x3 = x.reshape(ng, SUBLANES, tc)
    tail = xs_sc[...]
    xs_sc[...] = x[ts - SUBLANES:ts, :]
    cw = cw_ref[...]
    xc3 = cb_ref[...][None] + x3 * cw[CONV_WIDTH - 1:CONV_WIDTH, :][None]
    for d in range(1, CONV_WIDTH):
        rot = pltpu.roll(x3, d, axis=1)
        rot_tail = pltpu.roll(tail, d, axis=0)[None]
        rot_prev = jnp.concatenate([rot_tail, rot[:ng - 1]], axis=0)
        k = CONV_WIDTH - 1 - d
        xc3 = xc3 + jnp.where(ri >= d, rot, rot_prev) * cw[k:k + 1, :][None]
    xc = xc3.reshape(ts, tc)

    xcb = xc.astype(BF16)
    r_parts, i_parts = [], []
    for n in range(tc // RNN_BLOCK):
        g = jnp.dot(xcb[:, n * RNN_BLOCK:(n + 1) * RNN_BLOCK], wax_ref[n],
                    preferred_element_type=F32)
        r_parts.append(g[:, :RNN_BLOCK])
        i_parts.append(g[:, RNN_BLOCK:])
    tr = jnp.tanh(jnp.concatenate(r_parts, axis=1) + ba_ref[...])
    ti = jnp.tanh(jnp.concatenate(i_parts, axis=1) + bx_ref[...])

    nl = -lam_ref[...]
    softplus = jnp.maximum(nl, 0.0) + jnp.log1p(jnp.exp(-jnp.abs(nl)))
    log_a = (tr + 1.0) * ((-0.5 * RG_C) * softplus)
    a = jnp.exp(log_a)
    mult = jnp.sqrt(-jnp.tanh(log_a) * (a * a + 1.0))
    u = mult * ((0.5 * ti + 0.5) * xc)

    av = a.reshape(ng, SUBLANES, tc)
    bv = u.reshape(ng, SUBLANES, tc)
    for d in (1, 2, 4):
        a_prev = jnp.where(ri >= d, pltpu.roll(av, d, axis=1), 1.0)
        b_prev = jnp.where(ri >= d, pltpu.roll(bv, d, axis=1), 0.0)
        bv = av * b_prev + bv
        av = av * a_prev
    carry = h_sc[0:1, :]
    hs = []
    for k in range(ng):
        hk = bv[k] + av[k] * carry
        hs.append(hk)
        carry = hk[SUBLANES - 1:SUBLANES, :]
    h_sc[0:1, :] = carry
    h = jnp.concatenate(hs, axis=0)

    gt = rg_ref[...]
    c1 = math.sqrt(2.0 / math.pi)
    inner = gt * (c1 + (c1 * 0.044715) * (gt * gt))
    y_ref[...] = ((h * (0.5 * gt)) * (1.0 + jnp.tanh(inner))).astype(y_ref.dtype)


def _rglru(pf, rx_col, rg_col, conv_w, conv_b, wax, b_a, b_x, lam, *, width, ts, tc):
    bsz, s, _ = pf.shape
    chan = lambda rows: pl.BlockSpec((rows, tc), lambda b, c, t: (0, c))
    return pl.pallas_call(
        _rglru_kernel,
        grid=(bsz, width // tc, s // ts),
        in_specs=[pl.BlockSpec((None, ts, tc), lambda b, c, t: (b, t, rx_col // tc + c)),
                  pl.BlockSpec((None, ts, tc), lambda b, c, t: (b, t, rg_col // tc + c)),
                  chan(CONV_WIDTH), chan(1),
                  pl.BlockSpec((tc // RNN_BLOCK, RNN_BLOCK, 2 * RNN_BLOCK),
                               lambda b, c, t: (c, 0, 0)),
                  chan(1), chan(1), chan(1)],
        out_specs=pl.BlockSpec((None, ts, tc), lambda b, c, t: (b, t, c)),
        out_shape=jax.ShapeDtypeStruct((bsz, s, width), BF16),
        scratch_shapes=[pltpu.VMEM((SUBLANES, tc), F32),
                        pltpu.VMEM((SUBLANES, tc), F32)],
        compiler_params=_params(("parallel", "parallel", "arbitrary")),
        name="rglru",
    )(pf, pf, conv_w, conv_b, wax, b_a, b_x, lam)


def _bias_kernel(rb_ref, o_ref):
    h = pl.program_id(0)
    far = rb_ref[NUM_BUCKETS - 1, h]
    ii = lax.broadcasted_iota(I32, (KEY_BLOCK, KEY_BLOCK), 0)
    jj = lax.broadcasted_iota(I32, (KEY_BLOCK, KEY_BLOCK), 1)
    max_exact = NUM_BUCKETS // 2
    for d in range(2):
        n = jnp.maximum(ii - jj + KEY_BLOCK * d, 0)
        nf = jnp.maximum(n, 1).astype(F32)
        large = max_exact + (jnp.log(nf / max_exact) / math.log(MAX_DISTANCE / max_exact)
                             * (NUM_BUCKETS - max_exact)).astype(I32)
        large = jnp.minimum(large, NUM_BUCKETS - 1)
        bucket = jnp.where(n < max_exact, n, large)
        acc = jnp.zeros((KEY_BLOCK, KEY_BLOCK), F32)
        for b in range(NUM_BUCKETS):
            acc = jnp.where(bucket == b, rb_ref[b, h], acc)
        o_ref[0, d] = (acc - far) * LOG2E
    o_ref[0, 2] = jnp.zeros((KEY_BLOCK, KEY_BLOCK), F32)


def _bias_tiles(rel_bias):
    heads = rel_bias.shape[1]
    return pl.pallas_call(
        _bias_kernel,
        grid=(heads,),
        in_specs=[pl.BlockSpec(memory_space=pltpu.SMEM)],
        out_specs=pl.BlockSpec((1, 3, KEY_BLOCK, KEY_BLOCK), lambda h: (h, 0, 0, 0)),
        out_shape=jax.ShapeDtypeStruct((heads, 3, KEY_BLOCK, KEY_BLOCK), F32),
        compiler_params=_params(("parallel",)),
        name="bias_tiles",
    )(rel_bias)


SCORE_CHUNK = 4


def _attn_body(q_ref, k_ref, v_ref, iq_ref, ik_ref, iw_ref, bt_ref, y_ref,
               keys_sc, keyst_sc, am_sc, wb_sc, pb_sc, va_sc, *, k_top, nkb, class_blocks):
    heads, tq, _ = q_ref.shape
    groups = heads // KV_HEADS
    nk = nkb * KEY_BLOCK
    qi = pl.program_id(1)
    row = qi * tq + lax.broadcasted_iota(I32, (tq, KEY_BLOCK), 0)
    lane = lax.broadcasted_iota(I32, (tq, KEY_BLOCK), 1)

    w_scale = (IDX_HEADS ** -0.5) * (IDX_DIM ** -0.5)
    iw = iw_ref[...]
    for h in range(IDX_HEADS):
        col = iw[:, IDX_DIM + h:IDX_DIM + h + 1] * w_scale
        wb_sc[h] = jnp.broadcast_to(col, (tq, KEY_BLOCK))
    per_slab = LANES // IDX_DIM
    iq_rows = jnp.concatenate(
        [iq_ref[h // per_slab][:, (h % per_slab) * IDX_DIM:(h % per_slab + 1) * IDX_DIM]
         for h in range(IDX_HEADS)], axis=0)

    for c0 in range(0, nkb, SCORE_CHUNK):
        nb = min(SCORE_CHUNK, nkb - c0)
        ikc = ik_ref[c0 * KEY_BLOCK:(c0 + nb) * KEY_BLOCK, 0:IDX_DIM].astype(BF16)
        dots = lax.dot_general(iq_rows, ikc, (((1,), (1,)), ((), ())),
                               preferred_element_type=F32)
        for sub in range(nb):
            kb = c0 + sub
            acc = jnp.zeros((tq, KEY_BLOCK), F32)
            for h in range(IDX_HEADS):
                d = dots[h * tq:(h + 1) * tq, sub * KEY_BLOCK:(sub + 1) * KEY_BLOCK]
                acc = acc + jnp.maximum(d, 0.0) * wb_sc[h]
            score = jnp.where(kb * KEY_BLOCK + lane <= row, acc, -jnp.inf)
            bits = pltpu.bitcast(score, I32)
            key = bits ^ ((bits >> 31) & 0x7FFFFFFF)
            keys_sc[kb] = key
            keyst_sc[kb] = key.T.reshape(KEY_BLOCK // SUBLANES, SUBLANES, tq)

    def count_where(pred, ref8):
        parts = [jnp.sum(pred(keyst_sc[kb], ref8).astype(I32), axis=0) for kb in range(nkb)]
        return jnp.sum(_tree_sum(parts), axis=0, keepdims=True)

    def bit_step(it, carry):
        thr, cnt = carry
        cand = thr + lax.shift_left(jnp.int32(1), 31 - it)
        total = count_where(jnp.greater_equal, jnp.broadcast_to(cand, (SUBLANES, tq))[None])
        take = total >= k_top
        return jnp.where(take, cand, thr), jnp.where(take, total, cnt)

    thr, cnt = lax.fori_loop(0, 32, bit_step, (jnp.full((1, tq), INT32_MIN, I32),
                                               jnp.full((1, tq), nk, I32)))
    full_row = thr > KEY_NEG_INF
    thr = jnp.maximum(thr, KEY_NEG_INF + 1)
    has_ties = jnp.max(jnp.where(full_row, cnt, 0)) > k_top

    thr_b = jnp.broadcast_to(thr, (KEY_BLOCK, tq)).T
    for kb in range(nkb):
        am_sc[kb] = jnp.where(keys_sc[kb] >= thr_b, 0.0, NEG_LOGIT)

    @pl.when(has_ties)
    def _():
        thr_t = jnp.broadcast_to(thr, (KEY_BLOCK, tq))
        n_eq = count_where(jnp.equal, jnp.broadcast_to(thr, (SUBLANES, tq))[None])
        keep = jnp.where(full_row, k_top - (cnt - n_eq), nk).astype(F32)
        keep_t = jnp.broadcast_to(keep, (KEY_BLOCK, tq))
        ri = lax.broadcasted_iota(I32, (KEY_BLOCK, KEY_BLOCK), 0)
        ci = lax.broadcasted_iota(I32, (KEY_BLOCK, KEY_BLOCK), 1)
        lower = jnp.where(ri >= ci, 1.0, 0.0).astype(BF16)
        seen = jnp.zeros((1, tq), F32)
        for kb in range(nkb):
            kt = keyst_sc[kb].reshape(KEY_BLOCK, tq)
            eq = jnp.where(kt == thr_t, 1.0, 0.0)
            rank = jnp.dot(lower, eq.astype(BF16), preferred_element_type=F32) + seen
            seen = rank[KEY_BLOCK - 1:KEY_BLOCK, :]
            tie_ok = jnp.where(rank <= keep_t, eq, 0.0)
            sel = jnp.where(kt > thr_t, 1.0, tie_ok)
            am_sc[kb] = jnp.where(sel.T > 0.5, 0.0, NEG_LOGIT)

    first_near = max(nkb - class_blocks - 1, 0)

    def logit(s, g, j, kb):
        sj = (s[j * tq:(j + 1) * tq, kb * KEY_BLOCK:(kb + 1) * KEY_BLOCK] + am_sc[kb])
        if kb >= first_near:
            sj = sj + bt_ref[g * groups + j, jnp.clip(qi - kb, 0, 2)]
        return sj

    scale2 = (HEAD_DIM ** -0.5) * LOG2E

    def raw_logits(g):
        qg = q_ref[g * groups:(g + 1) * groups].reshape(groups * tq, HEAD_DIM)
        return lax.dot_general(qg, k_ref[g, 0:nk, :], (((1,), (1,)), ((), ())),
                               preferred_element_type=F32) * scale2

    s_next = raw_logits(0)
    for g in range(KV_HEADS):
        s = s_next
        if g + 1 < KV_HEADS:
            s_next = raw_logits(g + 1)
        for j in range(groups):
            m_run = logit(s, g, j, 0)
            for kb in range(1, nkb):
                m_run = jnp.maximum(m_run, logit(s, g, j, kb))
            m_b = jnp.broadcast_to(jnp.max(m_run, axis=1, keepdims=True), (tq, KEY_BLOCK))
            for kb in range(nkb):
                pb_sc[g, j * tq:(j + 1) * tq, kb * KEY_BLOCK:(kb + 1) * KEY_BLOCK] = (
                    jnp.exp2(logit(s, g, j, kb) - m_b).astype(BF16))
        o = jnp.dot(pb_sc[g, :, 0:nk], va_sc[g, 0:nk, :], preferred_element_type=F32)
        out = o[:, 0:HEAD_DIM] / o[:, HEAD_DIM:2 * HEAD_DIM]
        for j in range(groups):
            hd = (g * groups + j) * HEAD_DIM
            y_ref[:, hd:hd + HEAD_DIM] = out[j * tq:(j + 1) * tq].astype(y_ref.dtype)


def _attn_kernel(*refs, k_top, class_blocks, nq):
    qi = pl.program_id(1)
    v_ref, va_sc = refs[2], refs[-1]

    @pl.when(qi == 0)
    def _():
        va_sc[:, :, 0:HEAD_DIM] = v_ref[...]
        va_sc[:, :, HEAD_DIM:2 * HEAD_DIM] = jnp.ones(v_ref.shape, v_ref.dtype)

    for c in range(-(-nq // class_blocks)):
        nkb = min((c + 1) * class_blocks, nq)

        @pl.when(qi // class_blocks == c)
        def _(nkb=nkb):
            _attn_body(*refs, k_top=k_top, nkb=nkb, class_blocks=class_blocks)


def _attention(pb, ps, btab, *, bsz, heads, tq, class_blocks):
    s = ps.shape[1]
    nq = s // tq
    iq_slabs = IDX_HEADS * IDX_DIM // LANES
    assert HEAD_DIM == LANES and heads % KV_HEADS == 0 and heads % iq_slabs == 0
    assert pb.shape[0] == heads + 2 * KV_HEADS + iq_slabs
    nkb_max = s // KEY_BLOCK
    groups = heads // KV_HEADS
    k_top = min(TOPK_MAX, s // 4)
    assert tq == KEY_BLOCK and nq % class_blocks == 0
    return pl.pallas_call(
        functools.partial(_attn_kernel, k_top=k_top, class_blocks=class_blocks, nq=nq),
        grid=(bsz, nq),
        in_specs=[pl.BlockSpec((heads, tq, LANES), lambda b, i: (0, b * nq + i, 0)),
                  pl.BlockSpec((KV_HEADS, s, LANES), lambda b, i: (heads // KV_HEADS, b, 0)),
                  pl.BlockSpec((KV_HEADS, s, LANES), lambda b, i: (heads // KV_HEADS + 1, b, 0)),
                  pl.BlockSpec((iq_slabs, tq, LANES),
                               lambda b, i: ((heads + 2 * KV_HEADS) // iq_slabs, b * nq + i, 0)),
                  pl.BlockSpec((None, s, LANES), lambda b, i: (b, 0, 0)),
                  pl.BlockSpec((None, tq, LANES), lambda b, i: (b, i, 0)),
                  pl.BlockSpec(btab.shape, lambda b, i: (0, 0, 0, 0))],
        out_specs=pl.BlockSpec((tq, heads * HEAD_DIM), lambda b, i: (b * nq + i, 0)),
        out_shape=jax.ShapeDtypeStruct((bsz * s, heads * HEAD_DIM), BF16),
        scratch_shapes=[
            pltpu.VMEM((nkb_max, tq, KEY_BLOCK), I32),
            pltpu.VMEM((nkb_max, KEY_BLOCK // SUBLANES, SUBLANES, tq), I32),
            pltpu.VMEM((nkb_max, tq, KEY_BLOCK), F32),
            pltpu.VMEM((IDX_HEADS, tq, KEY_BLOCK), F32),
            pltpu.VMEM((KV_HEADS, groups * tq, s), BF16),
            pltpu.VMEM((KV_HEADS, s, 2 * HEAD_DIM), BF16)],
        compiler_params=_params(("parallel", "arbitrary")),
        name="sparse_attn",
    )(pb, pb, pb, pb, ps, ps, btab)


FFN_TM, FFN_TF = 512, 512
MM_TM, MM_TN = 1024, 1024
RGLRU_TS, RGLRU_TC = 512, 512
ATTN_CLASS_BLOCKS = 4


def _layer(x, ffn1_norm, ffn1_w_gate, ffn1_w_up, ffn1_w_down, mix_norm, w_in, conv_w, conv_b,
           rg_w_a, rg_b_a, rg_w_x, rg_b_x, rg_lambda, btab, w_proj_rnn, w_proj_attn, w_out,
           ffn2_norm, ffn2_w_gate, ffn2_w_up, ffn2_w_down, next_norm, *, last):
    bsz, s, d = x.shape
    m = bsz * s
    width = d
    q_width = d
    kv_width = KV_HEADS * HEAD_DIM
    iq_width = IDX_HEADS * IDX_DIM
    tm = min(FFN_TM, m)
    tmm = min(MM_TM, m)

    o_q = 2 * width
    o_ik = o_q + q_width + 2 * kv_width + iq_width
    o_gr = o_ik + IDX_DIM + IDX_HEADS

    bf = lambda w: w.astype(BF16)
    vec = lambda p: p.reshape(1, -1)

    x1, hn = _ffn(x.reshape(m, d), vec(ffn1_norm), bf(ffn1_w_gate), bf(ffn1_w_up), bf(ffn1_w_down),
                  vec(mix_norm), emit_resid=True, tm=tm, tf=FFN_TF)

    w_t = w_in.T
    tmi = tmm
    pf = _in_proj(hn, w_t, 0, o_q, F32, tm=tmi, tn=MM_TN, name="in_proj_rnn")
    pb = _in_proj(hn, w_t, o_q, o_ik - o_q, BF16, tm=tmi, tn=MM_TN, name="in_proj_attn",
                  lane_blocks=True)
    ps = _in_proj(hn, w_t, o_ik, LANES, F32, tm=tmi, tn=LANES, name="in_proj_idx")
    pg = _in_proj(hn, w_t, o_gr, 2 * d, F32, tm=tmi, tn=MM_TN, name="in_proj_gates")

    wax = bf(0.5 * jnp.concatenate([rg_w_a, rg_w_x], axis=-1))
    y_rnn = _rglru(pf.reshape(bsz, s, -1), 0, width, conv_w, vec(conv_b), wax,
                   vec(0.5 * rg_b_a), vec(0.5 * rg_b_x), vec(rg_lambda),
                   width=width, ts=min(RGLRU_TS, s), tc=RGLRU_TC)

    y_attn = _attention(pb, ps.reshape(bsz, s, LANES), btab, bsz=bsz, heads=q_width // HEAD_DIM,
                        tq=KEY_BLOCK, class_blocks=ATTN_CLASS_BLOCKS)

    merged = _merge(y_rnn.reshape(m, width), y_attn,
                    bf(w_proj_rnn), bf(w_proj_attn), pg, 0, d, tm=tm, tn=MM_TN)
    x2 = _matmul_residual(merged, bf(w_out), x1, tm=tmm, tn=MM_TN)

    outs = _ffn(x2, vec(ffn2_norm), bf(ffn2_w_gate), bf(ffn2_w_up), bf(ffn2_w_down),
                vec(next_norm), emit_resid=not last, tm=tm, tf=FFN_TF)
    return outs[0].reshape(bsz, s, d)


def kernel(x, ffn1_norm, ffn1_w_gate, ffn1_w_up, ffn1_w_down, mix_norm, w_in, conv_w, conv_b,
           rg_w_a, rg_b_a, rg_w_x, rg_b_x, rg_lambda, rel_bias, w_proj_rnn, w_proj_attn, w_out,
           ffn2_norm, ffn2_w_gate, ffn2_w_up, ffn2_w_down, final_norm):
    depth = ffn1_norm.shape[0]
    assert depth == 1, "the fused final norm assumes a single layer"
    btab = _bias_tiles(rel_bias)
    l = 0
    return _layer(x, ffn1_norm[l], ffn1_w_gate[l], ffn1_w_up[l], ffn1_w_down[l], mix_norm[l],
                  w_in[l], conv_w[l], conv_b[l], rg_w_a[l], rg_b_a[l], rg_w_x[l], rg_b_x[l],
                  rg_lambda[l], btab, w_proj_rnn[l], w_proj_attn[l], w_out[l], ffn2_norm[l],
                  ffn2_w_gate[l], ffn2_w_up[l], ffn2_w_down[l], final_norm, last=True)
```

```python
import functools
import math

import jax
import jax.numpy as jnp
from jax import lax
from jax.experimental import pallas as pl
from jax.experimental.pallas import tpu as pltpu

F32 = jnp.float32
BF16 = jnp.bfloat16
I32 = jnp.int32

RMS_EPS = 1e-6
CONV_WIDTH = 4
RG_C = 8.0
RNN_BLOCK = 128
HEAD_DIM = 128
KV_HEADS = 4
IDX_HEADS = 16
IDX_DIM = 64
TOPK_MAX = 256
NUM_BUCKETS = 32
MAX_DISTANCE = 128

LANES = 128
SUBLANES = 8
VMEM_LIMIT_BYTES = 56 * 1024 * 1024

KEY_BLOCK = 128
NEG_LOGIT = -1e30
INT32_MIN = -(2 ** 31)
KEY_NEG_INF = -2139095041
LOG2E = 1.4426950408889634


def _params(semantics):
    return pltpu.CompilerParams(dimension_semantics=semantics,
                                vmem_limit_bytes=VMEM_LIMIT_BYTES)


def _sigmoid(x):
    return 0.5 * (jnp.tanh(0.5 * x) + 1.0)


def _rms(x, g):
    ms = jnp.mean(x * x, axis=-1, keepdims=True)
    return x * lax.rsqrt(ms + RMS_EPS) * g


def _tree_sum(parts):
    while len(parts) > 1:
        parts = [a + b for a, b in zip(parts[::2], parts[1::2])] + (
            [parts[-1]] if len(parts) % 2 else [])
    return parts[0]


def _ffn_kernel(x_ref, g_ref, wg_ref, wu_ref, wd_ref, wgt_ref, wut_ref, wdt_ref, gn_ref, *rest,
                emit_resid, nfull):
    if emit_resid:
        acc_ref, hn_ref, xn_sc = rest
    else:
        acc_ref, xn_sc = rest
        hn_ref = acc_ref
    j = pl.program_id(1)

    @pl.when(j == 0)
    def _():
        xn_sc[...] = _rms(x_ref[...], g_ref[...]).astype(BF16)
        acc_ref[...] = jnp.zeros_like(acc_ref)

    def accumulate(wg, wu, wd):
        xn = xn_sc[...]
        h = jnp.dot(xn, wg, preferred_element_type=F32)
        u = jnp.dot(xn, wu, preferred_element_type=F32)
        a = (h * _sigmoid(h) * u).astype(BF16)
        acc_ref[...] += jnp.dot(a, wd, preferred_element_type=F32)

    @pl.when(j < nfull)
    def _():
        accumulate(wg_ref[...], wu_ref[...], wd_ref[...])

    @pl.when(j == nfull)
    def _():
        accumulate(wgt_ref[...], wut_ref[...], wdt_ref[...])
        y = x_ref[...] + 0.5 * acc_ref[...]
        if emit_resid:
            acc_ref[...] = y
        hn_ref[...] = _rms(y, gn_ref[...]).astype(hn_ref.dtype)


def _ffn(x, g, wg, wu, wd, gn, *, emit_resid, tm, tf):
    m, d = x.shape
    f = wg.shape[1]
    nfull = (f - 1) // tf
    tail = f - nfull * tf
    assert nfull >= 1 and tail % LANES == 0
    wgt, wut, wdt = wg[:, nfull * tf:], wu[:, nfull * tf:], wd[nfull * tf:, :]

    def tile_major(w):
        return w[:, :nfull * tf].reshape(d, nfull, tf).transpose(1, 0, 2)

    wg, wu = tile_major(wg), tile_major(wu)
    last = nfull - 1
    stream = pl.Buffered(2)
    fixed = pl.Buffered(1)
    row = pl.BlockSpec((tm, d), lambda i, j: (i, 0))
    vec = pl.BlockSpec((1, d), lambda i, j: (0, 0))
    out_shape = [jax.ShapeDtypeStruct((m, d), F32)]
    out_specs = [row]
    if emit_resid:
        out_shape.append(jax.ShapeDtypeStruct((m, d), BF16))
        out_specs.append(row)
    return pl.pallas_call(
        functools.partial(_ffn_kernel, emit_resid=emit_resid, nfull=nfull),
        grid=(m // tm, nfull + 1),
        in_specs=[row, vec,
                  pl.BlockSpec((None, d, tf), lambda i, j: (jnp.minimum(j, last), 0, 0),
                               pipeline_mode=stream),
                  pl.BlockSpec((None, d, tf), lambda i, j: (jnp.minimum(j, last), 0, 0),
                               pipeline_mode=stream),
                  pl.BlockSpec((tf, d), lambda i, j: (jnp.minimum(j, last), 0), pipeline_mode=stream),
                  pl.BlockSpec((d, tail), lambda i, j: (0, 0), pipeline_mode=fixed),
                  pl.BlockSpec((d, tail), lambda i, j: (0, 0), pipeline_mode=fixed),
                  pl.BlockSpec((tail, d), lambda i, j: (0, 0), pipeline_mode=fixed),
                  vec],
        out_specs=out_specs,
        out_shape=out_shape,
        scratch_shapes=[pltpu.VMEM((tm, d), BF16)],
        compiler_params=_params(("parallel", "arbitrary")),
        name="ffn_resid" if emit_resid else "ffn_final",
    )(x, g, wg, wu, wd, wgt, wut, wdt, gn)


def _in_proj_kernel(x_ref, wt_ref, o_ref, wb_sc, *, lane_blocks):
    @pl.when(pl.program_id(1) == 0)
    def _():
        wb_sc[...] = wt_ref[...].T.astype(BF16)

    res = jnp.dot(x_ref[...], wb_sc[...], preferred_element_type=F32)
    if lane_blocks:
        for c in range(res.shape[1] // LANES):
            o_ref[c] = res[:, c * LANES:(c + 1) * LANES].astype(o_ref.dtype)
    else:
        o_ref[...] = res.astype(o_ref.dtype)


def _in_proj(x, wt, row0, n, out_dtype, *, tm, tn, name, lane_blocks=False):
    m, k = x.shape
    assert row0 % SUBLANES == 0 and n % tn == 0
    if lane_blocks:
        out_shape = jax.ShapeDtypeStruct((n // LANES, m, LANES), out_dtype)
        out_spec = pl.BlockSpec((tn // LANES, tm, LANES), lambda j, i: (j, i, 0))
    else:
        out_shape = jax.ShapeDtypeStruct((m, n), out_dtype)
        out_spec = pl.BlockSpec((tm, tn), lambda j, i: (i, j))
    return pl.pallas_call(
        functools.partial(_in_proj_kernel, lane_blocks=lane_blocks),
        grid=(n // tn, m // tm),
        in_specs=[pl.BlockSpec((tm, k), lambda j, i: (i, 0)),
                  pl.BlockSpec((pl.Element(tn), pl.Element(k)),
                               lambda j, i: (pl.multiple_of(row0 + j * tn, SUBLANES), 0))],
        out_specs=out_spec,
        out_shape=out_shape,
        scratch_shapes=[pltpu.VMEM((k, tn), BF16)],
        compiler_params=_params(("parallel", "arbitrary")),
        name=name,
    )(x, wt)


def _mm_res_kernel(x_ref, w_ref, r_ref, o_ref):
    o_ref[...] = r_ref[...] + jnp.dot(x_ref[...], w_ref[...], preferred_element_type=F32)


def _matmul_residual(x, w, r, *, tm, tn):
    m, k = x.shape
    n = w.shape[1]
    return pl.pallas_call(
        _mm_res_kernel,
        grid=(n // tn, m // tm),
        in_specs=[pl.BlockSpec((tm, k), lambda j, i: (i, 0)),
                  pl.BlockSpec((k, tn), lambda j, i: (0, j)),
                  pl.BlockSpec((tm, tn), lambda j, i: (i, j))],
        out_specs=pl.BlockSpec((tm, tn), lambda j, i: (i, j)),
        out_shape=jax.ShapeDtypeStruct((m, n), F32),
        compiler_params=_params(("parallel", "parallel")),
        name="out_proj",
    )(x, w, r)


def _merge_kernel(yr_ref, ya_ref, wr_ref, wa_ref, gr_ref, ga_ref, o_ref):
    pr = jnp.dot(yr_ref[...], wr_ref[...], preferred_element_type=F32)
    pa = jnp.dot(ya_ref[...], wa_ref[...], preferred_element_type=F32)
    o_ref[...] = (_sigmoid(gr_ref[...]) * pr + _sigmoid(ga_ref[...]) * pa).astype(o_ref.dtype)


def _merge(y_rnn, y_attn, w_r, w_a, pf, gr_col, ga_col, *, tm, tn):
    m, k = y_rnn.shape
    n = w_r.shape[1]
    act = pl.BlockSpec((tm, k), lambda j, i: (i, 0))
    wsp = pl.BlockSpec((k, tn), lambda j, i: (0, j))
    return pl.pallas_call(
        _merge_kernel,
        grid=(n // tn, m // tm),
        in_specs=[act, act, wsp, wsp,
                  pl.BlockSpec((tm, tn), lambda j, i: (i, gr_col // tn + j)),
                  pl.BlockSpec((tm, tn), lambda j, i: (i, ga_col // tn + j))],
        out_specs=pl.BlockSpec((tm, tn), lambda j, i: (i, j)),
        out_shape=jax.ShapeDtypeStruct((m, n), BF16),
        compiler_params=_params(("parallel", "parallel")),
        name="merge",
    )(y_rnn, y_attn, w_r, w_a, pf, pf)


def _rglru_kernel(rx_ref, rg_ref, cw_ref, cb_ref, wax_ref, ba_ref, bx_ref, lam_ref,
                  y_ref, xs_sc, h_sc):
    ts, tc = rx_ref.shape
    ng = ts // SUBLANES

    @pl.when(pl.program_id(2) == 0)
    def _():
        xs_sc[...] = jnp.zeros_like(xs_sc)
        h_sc[...] = jnp.zeros_like(h_sc)

    ri = lax.broadcasted_iota(I32, (1, SUBLANES, tc), 1)
    x = rx_ref[...]
    x3 = x.reshape(ng, SUBLANES, tc)
    tail = xs_sc[...]
    xs_sc[...] = x[ts - SUBLANES:ts, :]
    cw = cw_ref[...]
    xc3 = cb_ref[...][None] + x3 * cw[CONV_WIDTH - 1:CONV_WIDTH, :][None]
    for d in range(1, CONV_WIDTH):
        rot = pltpu.roll(x3, d, axis=1)
        rot_tail = pltpu.roll(tail, d, axis=0)[None]
        rot_prev = jnp.concatenate([rot_tail, rot[:ng - 1]], axis=0)
        k = CONV_WIDTH - 1 - d
        xc3 = xc3 + jnp.where(ri >= d, rot, rot_prev) * cw[k:k + 1, :][None]
    xc = xc3.reshape(ts, tc)

    xcb = xc.astype(BF16)
    r_parts, i_parts = [], []
    for n in range(tc // RNN_BLOCK):
        g = jnp.dot(xcb[:, n * RNN_BLOCK:(n + 1) * RNN_BLOCK], wax_ref[n],
                    preferred_element_type=F32)
        r_parts.append(g[:, :RNN_BLOCK])
        i_parts.append(g[:, RNN_BLOCK:])
    tr = jnp.tanh(jnp.concatenate(r_parts, axis=1) + ba_ref[...])
    ti = jnp.tanh(jnp.concatenate(i_parts, axis=1) + bx_ref[...])

    nl = -lam_ref[...]
    softplus = jnp.maximum(nl, 0.0) + jnp.log1p(jnp.exp(-jnp.abs(nl)))
    log_a = (tr + 1.0) * ((-0.5 * RG_C) * softplus)
    a = jnp.exp(log_a)
    mult = jnp.sqrt(-jnp.tanh(log_a) * (a * a + 1.0))
    u = mult * ((0.5 * ti + 0.5) * xc)

    av = a.reshape(ng, SUBLANES, tc)
    bv = u.reshape(ng, SUBLANES, tc)
    for d in (1, 2, 4):
        a_prev = jnp.where(ri >= d, pltpu.roll(av, d, axis=1), 1.0)
        b_prev = jnp.where(ri >= d, pltpu.roll(bv, d, axis=1), 0.0)
        bv = av * b_prev + bv
        av = av * a_prev
    carry = h_sc[0:1, :]
    hs = []
    for k in range(ng):
        hk = bv[k] + av[k] * carry
        hs.append(hk)
        carry = hk[SUBLANES - 1:SUBLANES, :]
    h_sc[0:1, :] = carry
    h = jnp.concatenate(hs, axis=0)

    gt = rg_ref[...]
    c1 = math.sqrt(2.0 / math.pi)
    inner = gt * (c1 + (c1 * 0.044715) * (gt * gt))
    y_ref[...] = ((h * (0.5 * gt)) * (1.0 + jnp.tanh(inner))).astype(y_ref.dtype)


def _rglru(pf, rx_col, rg_col, conv_w, conv_b, wax, b_a, b_x, lam, *, width, ts, tc):
    bsz, s, _ = pf.shape
    chan = lambda rows: pl.BlockSpec((rows, tc), lambda b, c, t: (0, c))
    return pl.pallas_call(
        _rglru_kernel,
        grid=(bsz, width // tc, s // ts),
        in_specs=[pl.BlockSpec((None, ts, tc), lambda b, c, t: (b, t, rx_col // tc + c)),
                  pl.BlockSpec((None, ts, tc), lambda b, c, t: (b, t, rg_col // tc + c)),
                  chan(CONV_WIDTH), chan(1),
                  pl.BlockSpec((tc // RNN_BLOCK, RNN_BLOCK, 2 * RNN_BLOCK),
                               lambda b, c, t: (c, 0, 0)),
                  chan(1), chan(1), chan(1)],
        out_specs=pl.BlockSpec((None, ts, tc), lambda b, c, t: (b, t, c)),
        out_shape=jax.ShapeDtypeStruct((bsz, s, width), BF16),
        scratch_shapes=[pltpu.VMEM((SUBLANES, tc), F32),
                        pltpu.VMEM((SUBLANES, tc), F32)],
        compiler_params=_params(("parallel", "parallel", "arbitrary")),
        name="rglru",
    )(pf, pf, conv_w, conv_b, wax, b_a, b_x, lam)


def _bias_kernel(rb_ref, o_ref):
    h = pl.program_id(0)
    far = rb_ref[NUM_BUCKETS - 1, h]
    ii = lax.broadcasted_iota(I32, (KEY_BLOCK, KEY_BLOCK), 0)
    jj = lax.broadcasted_iota(I32, (KEY_BLOCK, KEY_BLOCK), 1)
    max_exact = NUM_BUCKETS // 2
    for d in range(2):
        n = jnp.maximum(ii - jj + KEY_BLOCK * d, 0)
        nf = jnp.maximum(n, 1).astype(F32)
        large = max_exact + (jnp.log(nf / max_exact) / math.log(MAX_DISTANCE / max_exact)
                             * (NUM_BUCKETS - max_exact)).astype(I32)
        large = jnp.minimum(large, NUM_BUCKETS - 1)
        bucket = jnp.where(n < max_exact, n, large)
        acc = jnp.zeros((KEY_BLOCK, KEY_BLOCK), F32)
        for b in range(NUM_BUCKETS):
            acc = jnp.where(bucket == b, rb_ref[b, h], acc)
        o_ref[0, d] = (acc - far) * LOG2E
    o_ref[0, 2] = jnp.zeros((KEY_BLOCK, KEY_BLOCK), F32)


def _bias_tiles(rel_bias):
    heads = rel_bias.shape[1]
    return pl.pallas_call(
        _bias_kernel,
        grid=(heads,),
        in_specs=[pl.BlockSpec(memory_space=pltpu.SMEM)],
        out_specs=pl.BlockSpec((1, 3, KEY_BLOCK, KEY_BLOCK), lambda h: (h, 0, 0, 0)),
        out_shape=jax.ShapeDtypeStruct((heads, 3, KEY_BLOCK, KEY_BLOCK), F32),
        compiler_params=_params(("parallel",)),
        name="bias_tiles",
    )(rel_bias)


SCORE_CHUNK = 4


def _attn_body(q_ref, k_ref, v_ref, iq_ref, ik_ref, iw_ref, bt_ref, y_ref,
               keys_sc, keyst_sc, am_sc, wb_sc, pb_sc, va_sc, *, k_top, nkb, class_blocks):
    heads, tq, _ = q_ref.shape
    groups = heads // KV_HEADS
    nk = nkb * KEY_BLOCK
    qi = pl.program_id(1)
    row = qi * tq + lax.broadcasted_iota(I32, (tq, KEY_BLOCK), 0)
    lane = lax.broadcasted_iota(I32, (tq, KEY_BLOCK), 1)

    w_scale = (IDX_HEADS ** -0.5) * (IDX_DIM ** -0.5)
    iw = iw_ref[...]
    for h in range(IDX_HEADS):
        col = iw[:, IDX_DIM + h:IDX_DIM + h + 1] * w_scale
        wb_sc[h] = jnp.broadcast_to(col, (tq, KEY_BLOCK))
    per_slab = LANES // IDX_DIM
    iq_rows = jnp.concatenate(
        [iq_ref[h // per_slab][:, (h % per_slab) * IDX_DIM:(h % per_slab + 1) * IDX_DIM]
         for h in range(IDX_HEADS)], axis=0)

    for c0 in range(0, nkb, SCORE_CHUNK):
        nb = min(SCORE_CHUNK, nkb - c0)
        ikc = ik_ref[c0 * KEY_BLOCK:(c0 + nb) * KEY_BLOCK, 0:IDX_DIM].astype(BF16)
        dots = lax.dot_general(iq_rows, ikc, (((1,), (1,)), ((), ())),
                               preferred_element_type=F32)
        for sub in range(nb):
            kb = c0 + sub
            acc = jnp.zeros((tq, KEY_BLOCK), F32)
            for h in range(IDX_HEADS):
                d = dots[h * tq:(h + 1) * tq, sub * KEY_BLOCK:(sub + 1) * KEY_BLOCK]
                acc = acc + jnp.maximum(d, 0.0) * wb_sc[h]
            score = jnp.where(kb * KEY_BLOCK + lane <= row, acc, -jnp.inf)
            bits = pltpu.bitcast(score, I32)
            key = bits ^ ((bits >> 31) & 0x7FFFFFFF)
            keys_sc[kb] = key
            keyst_sc[kb] = key.T.reshape(KEY_BLOCK // SUBLANES, SUBLANES, tq)

    def count_where(pred, ref8):
        parts = [jnp.sum(pred(keyst_sc[kb], ref8).astype(I32), axis=0) for kb in range(nkb)]
        return jnp.sum(_tree_sum(parts), axis=0, keepdims=True)

    def bit_step(it, carry):
        thr, cnt = carry
        cand = thr + lax.shift_left(jnp.int32(1), 31 - it)
        total = count_where(jnp.greater_equal, jnp.broadcast_to(cand, (SUBLANES, tq))[None])
        take = total >= k_top
        return jnp.where(take, cand, thr), jnp.where(take, total, cnt)

    thr, cnt = lax.fori_loop(0, 32, bit_step, (jnp.full((1, tq), INT32_MIN, I32),
                                               jnp.full((1, tq), nk, I32)))
    full_row = thr > KEY_NEG_INF
    thr = jnp.maximum(thr, KEY_NEG_INF + 1)
    has_ties = jnp.max(jnp.where(full_row, cnt, 0)) > k_top

    thr_b = jnp.broadcast_to(thr, (KEY_BLOCK, tq)).T
    for kb in range(nkb):
        am_sc[kb] = jnp.where(keys_sc[kb] >= thr_b, 0.0, NEG_LOGIT)

    @pl.when(has_ties)
    def _():
        thr_t = jnp.broadcast_to(thr, (KEY_BLOCK, tq))
        n_eq = count_where(jnp.equal, jnp.broadcast_to(thr, (SUBLANES, tq))[None])
        keep = jnp.where(full_row, k_top - (cnt - n_eq), nk).astype(F32)
        keep_t = jnp.broadcast_to(keep, (KEY_BLOCK, tq))
        ri = lax.broadcasted_iota(I32, (KEY_BLOCK, KEY_BLOCK), 0)
        ci = lax.broadcasted_iota(I32, (KEY_BLOCK, KEY_BLOCK), 1)
        lower = jnp.where(ri >= ci, 1.0, 0.0).astype(BF16)
        seen = jnp.zeros((1, tq), F32)
        for kb in range(nkb):
            kt = keyst_sc[kb].reshape(KEY_BLOCK, tq)
            eq = jnp.where(kt == thr_t, 1.0, 0.0)
            rank = jnp.dot(lower, eq.astype(BF16), preferred_element_type=F32) + seen
            seen = rank[KEY_BLOCK - 1:KEY_BLOCK, :]
            tie_ok = jnp.where(rank <= keep_t, eq, 0.0)
            sel = jnp.where(kt > thr_t, 1.0, tie_ok)
            am_sc[kb] = jnp.where(sel.T > 0.5, 0.0, NEG_LOGIT)

    first_near = max(nkb - class_blocks - 1, 0)

    def logit(s, g, j, kb):
        sj = (s[j * tq:(j + 1) * tq, kb * KEY_BLOCK:(kb + 1) * KEY_BLOCK] + am_sc[kb])
        if kb >= first_near:
            sj = sj + bt_ref[g * groups + j, jnp.clip(qi - kb, 0, 2)]
        return sj

    scale2 = (HEAD_DIM ** -0.5) * LOG2E

    def raw_logits(g):
        qg = q_ref[g * groups:(g + 1) * groups].reshape(groups * tq, HEAD_DIM)
        return lax.dot_general(qg, k_ref[g, 0:nk, :], (((1,), (1,)), ((), ())),
                               preferred_element_type=F32) * scale2

    s_next = raw_logits(0)
    for g in range(KV_HEADS):
        s = s_next
        if g + 1 < KV_HEADS:
            s_next = raw_logits(g + 1)
        for j in range(groups):
            m_run = logit(s, g, j, 0)
            for kb in range(1, nkb):
                m_run = jnp.maximum(m_run, logit(s, g, j, kb))
            m_b = jnp.broadcast_to(jnp.max(m_run, axis=1, keepdims=True), (tq, KEY_BLOCK))
            for kb in range(nkb):
                pb_sc[g, j * tq:(j + 1) * tq, kb * KEY_BLOCK:(kb + 1) * KEY_BLOCK] = (
                    jnp.exp2(logit(s, g, j, kb) - m_b).astype(BF16))
        o = jnp.dot(pb_sc[g, :, 0:nk], va_sc[g, 0:nk, :], preferred_element_type=F32)
        out = o[:, 0:HEAD_DIM] / o[:, HEAD_DIM:2 * HEAD_DIM]
        for j in range(groups):
            hd = (g * groups + j) * HEAD_DIM
            y_ref[:, hd:hd + HEAD_DIM] = out[j * tq:(j + 1) * tq].astype(y_ref.dtype)


def _attn_kernel(*refs, k_top, class_blocks, nq):
    qi = pl.program_id(1)
    v_ref, va_sc = refs[2], refs[-1]

    @pl.when(qi == 0)
    def _():
        va_sc[:, :, 0:HEAD_DIM] = v_ref[...]
        va_sc[:, :, HEAD_DIM:2 * HEAD_DIM] = jnp.ones(v_ref.shape, v_ref.dtype)

    for c in range(-(-nq // class_blocks)):
        nkb = min((c + 1) * class_blocks, nq)

        @pl.when(qi // class_blocks == c)
        def _(nkb=nkb):
            _attn_body(*refs, k_top=k_top, nkb=nkb, class_blocks=class_blocks)


def _attention(pb, ps, btab, *, bsz, heads, tq, class_blocks):
    s = ps.shape[1]
    nq = s // tq
    iq_slabs = IDX_HEADS * IDX_DIM // LANES
    assert HEAD_DIM == LANES and heads % KV_HEADS == 0 and heads % iq_slabs == 0
    assert pb.shape[0] == heads + 2 * KV_HEADS + iq_slabs
    nkb_max = s // KEY_BLOCK
    groups = heads // KV_HEADS
    k_top = min(TOPK_MAX, s // 4)
    assert tq == KEY_BLOCK and nq % class_blocks == 0
    return pl.pallas_call(
        functools.partial(_attn_kernel, k_top=k_top, class_blocks=class_blocks, nq=nq),
        grid=(bsz, nq),
        in_specs=[pl.BlockSpec((heads, tq, LANES), lambda b, i: (0, b * nq + i, 0)),
                  pl.BlockSpec((KV_HEADS, s, LANES), lambda b, i: (heads // KV_HEADS, b, 0)),
                  pl.BlockSpec((KV_HEADS, s, LANES), lambda b, i: (heads // KV_HEADS + 1, b, 0)),
                  pl.BlockSpec((iq_slabs, tq, LANES),
                               lambda b, i: ((heads + 2 * KV_HEADS) // iq_slabs, b * nq + i, 0)),
                  pl.BlockSpec((None, s, LANES), lambda b, i: (b, 0, 0)),
                  pl.BlockSpec((None, tq, LANES), lambda b, i: (b, i, 0)),
                  pl.BlockSpec(btab.shape, lambda b, i: (0, 0, 0, 0))],
        out_specs=pl.BlockSpec((tq, heads * HEAD_DIM), lambda b, i: (b * nq + i, 0)),
        out_shape=jax.ShapeDtypeStruct((bsz * s, heads * HEAD_DIM), BF16),
        scratch_shapes=[
            pltpu.VMEM((nkb_max, tq, KEY_BLOCK), I32),
            pltpu.VMEM((nkb_max, KEY_BLOCK // SUBLANES, SUBLANES, tq), I32),
            pltpu.VMEM((nkb_max, tq, KEY_BLOCK), F32),
            pltpu.VMEM((IDX_HEADS, tq, KEY_BLOCK), F32),
            pltpu.VMEM((KV_HEADS, groups * tq, s), BF16),
            pltpu.VMEM((KV_HEADS, s, 2 * HEAD_DIM), BF16)],
        compiler_params=_params(("parallel", "arbitrary")),
        name="sparse_attn",
    )(pb, pb, pb, pb, ps, ps, btab)


FFN_TM, FFN_TF = 512, 512
MM_TM, MM_TN = 1024, 1024
RGLRU_TS, RGLRU_TC = 512, 512
ATTN_CLASS_BLOCKS = 4


def _layer(x, ffn1_norm, ffn1_w_gate, ffn1_w_up, ffn1_w_down, mix_norm, w_in, conv_w, conv_b,
           rg_w_a, rg_b_a, rg_w_x, rg_b_x, rg_lambda, btab, w_proj_rnn, w_proj_attn, w_out,
           ffn2_norm, ffn2_w_gate, ffn2_w_up, ffn2_w_down, next_norm, *, last):
    bsz, s, d = x.shape
    m = bsz * s
    width = d
    q_width = d
    kv_width = KV_HEADS * HEAD_DIM
    iq_width = IDX_HEADS * IDX_DIM
    tm = min(FFN_TM, m)
    tmm = min(MM_TM, m)

    o_q = 2 * width
    o_ik = o_q + q_width + 2 * kv_width + iq_width
    o_gr = o_ik + IDX_DIM + IDX_HEADS

    bf = lambda w: w.astype(BF16)
    vec = lambda p: p.reshape(1, -1)

    x1, hn = _ffn(x.reshape(m, d), vec(ffn1_norm), bf(ffn1_w_gate), bf(ffn1_w_up), bf(ffn1_w_down),
                  vec(mix_norm), emit_resid=True, tm=tm, tf=FFN_TF)

    w_t = w_in.T
    tmi = tmm
    pf = _in_proj(hn, w_t, 0, o_q, F32, tm=tmi, tn=MM_TN, name="in_proj_rnn")
    pb = _in_proj(hn, w_t, o_q, o_ik - o_q, BF16, tm=tmi, tn=MM_TN, name="in_proj_attn",
                  lane_blocks=True)
    ps = _in_proj(hn, w_t, o_ik, LANES, F32, tm=tmi, tn=LANES, name="in_proj_idx")
    pg = _in_proj(hn, w_t, o_gr, 2 * d, F32, tm=tmi, tn=MM_TN, name="in_proj_gates")

    wax = bf(0.5 * jnp.concatenate([rg_w_a, rg_w_x], axis=-1))
    y_rnn = _rglru(pf.reshape(bsz, s, -1), 0, width, conv_w, vec(conv_b), wax,
                   vec(0.5 * rg_b_a), vec(0.5 * rg_b_x), vec(rg_lambda),
                   width=width, ts=min(RGLRU_TS, s), tc=RGLRU_TC)

    y_attn = _attention(pb, ps.reshape(bsz, s, LANES), btab, bsz=bsz, heads=q_width // HEAD_DIM,
                        tq=KEY_BLOCK, class_blocks=ATTN_CLASS_BLOCKS)

    merged = _merge(y_rnn.reshape(m, width), y_attn,
                    bf(w_proj_rnn), bf(w_proj_attn), pg, 0, d, tm=tm, tn=MM_TN)
    x2 = _matmul_residual(merged, bf(w_out), x1, tm=tmm, tn=MM_TN)

    outs = _ffn(x2, vec(ffn2_norm), bf(ffn2_w_gate), bf(ffn2_w_up), bf(ffn2_w_down),
                vec(next_norm), emit_resid=not last, tm=tm, tf=FFN_TF)
    return outs[0].reshape(bsz, s, d)


def kernel(x, ffn1_norm, ffn1_w_gate, ffn1_w_up, ffn1_w_down, mix_norm, w_in, conv_w, conv_b,
           rg_w_a, rg_b_a, rg_w_x, rg_b_x, rg_lambda, rel_bias, w_proj_rnn, w_proj_attn, w_out,
           ffn2_norm, ffn2_w_gate, ffn2_w_up, ffn2_w_down, final_norm):
    depth = ffn1_norm.shape[0]
    assert depth == 1, "the fused final norm assumes a single layer"
    btab = _bias_tiles(rel_bias)
    l = 0
    return _layer(x, ffn1_norm[l], ffn1_w_gate[l], ffn1_w_up[l], ffn1_w_down[l], mix_norm[l],
                  w_in[l], conv_w[l], conv_b[l], rg_w_a[l], rg_b_a[l], rg_w_x[l], rg_b_x[l],
                  rg_lambda[l], btab, w_proj_rnn[l], w_proj_attn[l], w_out[l], ffn2_norm[l],
                  ffn2_w_gate[l], ffn2_w_up[l], ffn2_w_down[l], final_norm, last=True)
```

```python
import functools
import math

import jax
import jax.numpy as jnp
from jax import lax
from jax.experimental import pallas as pl
from jax.experimental.pallas import tpu as pltpu

F32 = jnp.float32
BF16 = jnp.bfloat16
I32 = jnp.int32

RMS_EPS = 1e-6
CONV_WIDTH = 4
RG_C = 8.0
RNN_BLOCK = 128
HEAD_DIM = 128
KV_HEADS = 4
IDX_HEADS = 16
IDX_DIM = 64
TOPK_MAX = 256
NUM_BUCKETS = 32
MAX_DISTANCE = 128

LANES = 128
SUBLANES = 8
VMEM_LIMIT_BYTES = 56 * 1024 * 1024

KEY_BLOCK = 128
NEG_LOGIT = -1e30
INT32_MIN = -(2 ** 31)
KEY_NEG_INF = -2139095041
LOG2E = 1.4426950408889634


def _params(semantics):
    return pltpu.CompilerParams(dimension_semantics=semantics,
                                vmem_limit_bytes=VMEM_LIMIT_BYTES)


def _sigmoid(x):
    return 0.5 * (jnp.tanh(0.5 * x) + 1.0)


def _rms(x, g):
    ms = jnp.mean(x * x, axis=-1, keepdims=True)
    return x * lax.rsqrt(ms + RMS_EPS) * g


def _tree_sum(parts):
    while len(parts) > 1:
        parts = [a + b for a, b in zip(parts[::2], parts[1::2])] + (
            [parts[-1]] if len(parts) % 2 else [])
    return parts[0]


def _ffn_kernel(x_ref, g_ref, wg_ref, wu_ref, wd_ref, wgt_ref, wut_ref, wdt_ref, gn_ref, *rest,
                emit_resid, nfull):
    if emit_resid:
        acc_ref, hn_ref, xn_sc = rest
    else:
        acc_ref, xn_sc = rest
        hn_ref = acc_ref
    j = pl.program_id(1)

    @pl.when(j == 0)
    def _():
        xn_sc[...] = _rms(x_ref[...], g_ref[...]).astype(BF16)
        acc_ref[...] = jnp.zeros_like(acc_ref)

    def accumulate(wg, wu, wd):
        xn = xn_sc[...]
        h = jnp.dot(xn, wg, preferred_element_type=F32)
        u = jnp.dot(xn, wu, preferred_element_type=F32)
        a = (h * _sigmoid(h) * u).astype(BF16)
        acc_ref[...] += jnp.dot(a, wd.astype(BF16), preferred_element_type=F32)

    @pl.when(j < nfull)
    def _():
        accumulate(wg_ref[...], wu_ref[...], wd_ref[...])

    @pl.when(j == nfull)
    def _():
        accumulate(wgt_ref[...], wut_ref[...], wdt_ref[...])
        y = x_ref[...] + 0.5 * acc_ref[...]
        if emit_resid:
            acc_ref[...] = y
        hn_ref[...] = _rms(y, gn_ref[...]).astype(hn_ref.dtype)


def _ffn(x, g, wg, wu, wd, gn, *, emit_resid, tm, tf):
    m, d = x.shape
    f = wg.shape[1]
    nfull = (f - 1) // tf
    tail = f - nfull * tf
    assert nfull >= 1 and tail % LANES == 0
    wgt, wut, wdt = wg[:, nfull * tf:], wu[:, nfull * tf:], wd[nfull * tf:, :]
    last = nfull - 1
    stream = pl.Buffered(2)
    fixed = pl.Buffered(1)
    row = pl.BlockSpec((tm, d), lambda i, j: (i, 0))
    vec = pl.BlockSpec((1, d), lambda i, j: (0, 0))
    out_shape = [jax.ShapeDtypeStruct((m, d), F32)]
    out_specs = [row]
    if emit_resid:
        out_shape.append(jax.ShapeDtypeStruct((m, d), BF16))
        out_specs.append(row)
    return pl.pallas_call(
        functools.partial(_ffn_kernel, emit_resid=emit_resid, nfull=nfull),
        grid=(m // tm, nfull + 1),
        in_specs=[row, vec,
                  pl.BlockSpec((d, tf), lambda i, j: (0, jnp.minimum(j, last)), pipeline_mode=stream),
                  pl.BlockSpec((d, tf), lambda i, j: (0, jnp.minimum(j, last)), pipeline_mode=stream),
                  pl.BlockSpec((tf, d), lambda i, j: (jnp.minimum(j, last), 0), pipeline_mode=stream),
                  pl.BlockSpec((d, tail), lambda i, j: (0, 0), pipeline_mode=fixed),
                  pl.BlockSpec((d, tail), lambda i, j: (0, 0), pipeline_mode=fixed),
                  pl.BlockSpec((tail, d), lambda i, j: (0, 0), pipeline_mode=fixed),
                  vec],
        out_specs=out_specs,
        out_shape=out_shape,
        scratch_shapes=[pltpu.VMEM((tm, d), BF16)],
        compiler_params=_params(("parallel", "arbitrary")),
        name="ffn_resid" if emit_resid else "ffn_final",
    )(x, g, wg, wu, wd, wgt, wut, wdt, gn)


def _in_proj_kernel(x_ref, wt_ref, o_ref, wb_sc, *, lane_blocks):
    @pl.when(pl.program_id(1) == 0)
    def _():
        wb_sc[...] = wt_ref[...].T.astype(BF16)

    res = jnp.dot(x_ref[...], wb_sc[...], preferred_element_type=F32)
    if lane_blocks:
        for c in range(res.shape[1] // LANES):
            o_ref[c] = res[:, c * LANES:(c + 1) * LANES].astype(o_ref.dtype)
    else:
        o_ref[...] = res.astype(o_ref.dtype)


def _in_proj(x, wt, row0, n, out_dtype, *, tm, tn, name, lane_blocks=False):
    m, k = x.shape
    assert row0 % SUBLANES == 0 and n % tn == 0
    if lane_blocks:
        out_shape = jax.ShapeDtypeStruct((n // LANES, m, LANES), out_dtype)
        out_spec = pl.BlockSpec((tn // LANES, tm, LANES), lambda j, i: (j, i, 0))
    else:
        out_shape = jax.ShapeDtypeStruct((m, n), out_dtype)
        out_spec = pl.BlockSpec((tm, tn), lambda j, i: (i, j))
    return pl.pallas_call(
        functools.partial(_in_proj_kernel, lane_blocks=lane_blocks),
        grid=(n // tn, m // tm),
        in_specs=[pl.BlockSpec((tm, k), lambda j, i: (i, 0)),
                  pl.BlockSpec((pl.Element(tn), pl.Element(k)),
                               lambda j, i: (pl.multiple_of(row0 + j * tn, SUBLANES), 0))],
        out_specs=out_spec,
        out_shape=out_shape,
        scratch_shapes=[pltpu.VMEM((k, tn), BF16)],
        compiler_params=_params(("parallel", "arbitrary")),
        name=name,
    )(x, wt)


def _mm_res_kernel(x_ref, w_ref, r_ref, o_ref, wb_sc):
    @pl.when(pl.program_id(1) == 0)
    def _():
        wb_sc[...] = w_ref[...].astype(BF16)

    o_ref[...] = r_ref[...] + jnp.dot(x_ref[...], wb_sc[...], preferred_element_type=F32)


def _matmul_residual(x, w, r, *, tm, tn):
    m, k = x.shape
    n = w.shape[1]
    return pl.pallas_call(
        _mm_res_kernel,
        grid=(n // tn, m // tm),
        in_specs=[pl.BlockSpec((tm, k), lambda j, i: (i, 0)),
                  pl.BlockSpec((k, tn), lambda j, i: (0, j)),
                  pl.BlockSpec((tm, tn), lambda j, i: (i, j))],
        out_specs=pl.BlockSpec((tm, tn), lambda j, i: (i, j)),
        out_shape=jax.ShapeDtypeStruct((m, n), F32),
        scratch_shapes=[pltpu.VMEM((k, tn), BF16)],
        compiler_params=_params(("parallel", "arbitrary")),
        name="out_proj",
    )(x, w, r)


def _merge_kernel(yr_ref, ya_ref, wr_ref, wa_ref, gr_ref, ga_ref, o_ref):
    pr = jnp.dot(yr_ref[...], wr_ref[...], preferred_element_type=F32)
    pa = jnp.dot(ya_ref[...], wa_ref[...], preferred_element_type=F32)
    o_ref[...] = (_sigmoid(gr_ref[...]) * pr + _sigmoid(ga_ref[...]) * pa).astype(o_ref.dtype)


def _merge(y_rnn, y_attn, w_r, w_a, pf, gr_col, ga_col, *, tm, tn):
    m, k = y_rnn.shape
    n = w_r.shape[1]
    act = pl.BlockSpec((tm, k), lambda j, i: (i, 0))
    wsp = pl.BlockSpec((k, tn), lambda j, i: (0, j))
    return pl.pallas_call(
        _merge_kernel,
        grid=(n // tn, m // tm),
        in_specs=[act, act, wsp, wsp,
                  pl.BlockSpec((tm, tn), lambda j, i: (i, gr_col // tn + j)),
                  pl.BlockSpec((tm, tn), lambda j, i: (i, ga_col // tn + j))],
        out_specs=pl.BlockSpec((tm, tn), lambda j, i: (i, j)),
        out_shape=jax.ShapeDtypeStruct((m, n), BF16),
        compiler_params=_params(("parallel", "parallel")),
        name="merge",
    )(y_rnn, y_attn, w_r, w_a, pf, pf)


def _rglru_kernel(rx_ref, rg_ref, cw_ref, cb_ref, wax_ref, ba_ref, bx_ref, lam_ref,
                  y_ref, xs_sc, h_sc):
    ts, tc = rx_ref.shape
    ng = ts // SUBLANES

    @pl.when(pl.program_id(2) == 0)
    def _():
        xs_sc[...] = jnp.zeros_like(xs_sc)
        h_sc[...] = jnp.zeros_like(h_sc)

    ri = lax.broadcasted_iota(I32, (1, SUBLANES, tc), 1)
    x = rx_ref[...]
    x3 = x.reshape(ng, SUBLANES, tc)
    tail = xs_sc[...]
    xs_sc[...] = x[ts - SUBLANES:ts, :]
    cw = cw_ref[...]
    xc3 = cb_ref[...][None] + x3 * cw[CONV_WIDTH - 1:CONV_WIDTH, :][None]
    for d in range(1, CONV_WIDTH):
        rot = pltpu.roll(x3, d, axis=1)
        rot_tail = pltpu.roll(tail, d, axis=0)[None]
        rot_prev = jnp.concatenate([rot_tail, rot[:ng - 1]], axis=0)
        k = CONV_WIDTH - 1 - d
        xc3 = xc3 + jnp.where(ri >= d, rot, rot_prev) * cw[k:k + 1, :][None]
    xc = xc3.reshape(ts, tc)

    xcb = xc.astype(BF16)
    r_parts, i_parts = [], []
    for n in range(tc // RNN_BLOCK):
        g = jnp.dot(xcb[:, n * RNN_BLOCK:(n + 1) * RNN_BLOCK], wax_ref[n],
                    preferred_element_type=F32)
        r_parts.append(g[:, :RNN_BLOCK])
        i_parts.append(g[:, RNN_BLOCK:])
    tr = jnp.tanh(jnp.concatenate(r_parts, axis=1) + ba_ref[...])
    ti = jnp.tanh(jnp.concatenate(i_parts, axis=1) + bx_ref[...])

    nl = -lam_ref[...]
    softplus = jnp.maximum(nl, 0.0) + jnp.log1p(jnp.exp(-jnp.abs(nl)))
    log_a = (tr + 1.0) * ((-0.5 * RG_C) * softplus)
    a = jnp.exp(log_a)
    mult = jnp.sqrt(-jnp.tanh(log_a) * (a * a + 1.0))
    u = mult * ((0.5 * ti + 0.5) * xc)

    av = a.reshape(ng, SUBLANES, tc)
    bv = u.reshape(ng, SUBLANES, tc)
    for d in (1, 2, 4):
        a_prev = jnp.where(ri >= d, pltpu.roll(av, d, axis=1), 1.0)
        b_prev = jnp.where(ri >= d, pltpu.roll(bv, d, axis=1), 0.0)
        bv = av * b_prev + bv
        av = av * a_prev
    carry = h_sc[0:1, :]
    hs = []
    for k in range(ng):
        hk = bv[k] + av[k] * carry
        hs.append(hk)
        carry = hk[SUBLANES - 1:SUBLANES, :]
    h_sc[0:1, :] = carry
    h = jnp.concatenate(hs, axis=0)

    gt = rg_ref[...]
    c1 = math.sqrt(2.0 / math.pi)
    inner = gt * (c1 + (c1 * 0.044715) * (gt * gt))
    y_ref[...] = ((h * (0.5 * gt)) * (1.0 + jnp.tanh(inner))).astype(y_ref.dtype)


def _rglru(pf, rx_col, rg_col, conv_w, conv_b, wax, b_a, b_x, lam, *, width, ts, tc):
    bsz, s, _ = pf.shape
    chan = lambda rows: pl.BlockSpec((rows, tc), lambda b, c, t: (0, c))
    return pl.pallas_call(
        _rglru_kernel,
        grid=(bsz, width // tc, s // ts),
        in_specs=[pl.BlockSpec((None, ts, tc), lambda b, c, t: (b, t, rx_col // tc + c)),
                  pl.BlockSpec((None, ts, tc), lambda b, c, t: (b, t, rg_col // tc + c)),
                  chan(CONV_WIDTH), chan(1),
                  pl.BlockSpec((tc // RNN_BLOCK, RNN_BLOCK, 2 * RNN_BLOCK),
                               lambda b, c, t: (c, 0, 0)),
                  chan(1), chan(1), chan(1)],
        out_specs=pl.BlockSpec((None, ts, tc), lambda b, c, t: (b, t, c)),
        out_shape=jax.ShapeDtypeStruct((bsz, s, width), BF16),
        scratch_shapes=[pltpu.VMEM((SUBLANES, tc), F32),
                        pltpu.VMEM((SUBLANES, tc), F32)],
        compiler_params=_params(("parallel", "parallel", "arbitrary")),
        name="rglru",
    )(pf, pf, conv_w, conv_b, wax, b_a, b_x, lam)


def _bias_kernel(rb_ref, o_ref):
    h = pl.program_id(0)
    far = rb_ref[NUM_BUCKETS - 1, h]
    ii = lax.broadcasted_iota(I32, (KEY_BLOCK, KEY_BLOCK), 0)
    jj = lax.broadcasted_iota(I32, (KEY_BLOCK, KEY_BLOCK), 1)
    max_exact = NUM_BUCKETS // 2
    for d in range(2):
        n = jnp.maximum(ii - jj + KEY_BLOCK * d, 0)
        nf = jnp.maximum(n, 1).astype(F32)
        large = max_exact + (jnp.log(nf / max_exact) / math.log(MAX_DISTANCE / max_exact)
                             * (NUM_BUCKETS - max_exact)).astype(I32)
        large = jnp.minimum(large, NUM_BUCKETS - 1)
        bucket = jnp.where(n < max_exact, n, large)
        acc = jnp.zeros((KEY_BLOCK, KEY_BLOCK), F32)
        for b in range(NUM_BUCKETS):
            acc = jnp.where(bucket == b, rb_ref[b, h], acc)
        o_ref[0, d] = (acc - far) * LOG2E
    o_ref[0, 2] = jnp.zeros((KEY_BLOCK, KEY_BLOCK), F32)


def _bias_tiles(rel_bias):
    heads = rel_bias.shape[1]
    return pl.pallas_call(
        _bias_kernel,
        grid=(heads,),
        in_specs=[pl.BlockSpec(memory_space=pltpu.SMEM)],
        out_specs=pl.BlockSpec((1, 3, KEY_BLOCK, KEY_BLOCK), lambda h: (h, 0, 0, 0)),
        out_shape=jax.ShapeDtypeStruct((heads, 3, KEY_BLOCK, KEY_BLOCK), F32),
        compiler_params=_params(("parallel",)),
        name="bias_tiles",
    )(rel_bias)


SCORE_CHUNK = 4


def _attn_body(q_ref, k_ref, v_ref, iq_ref, ik_ref, iw_ref, bt_ref, y_ref,
               keys_sc, keyst_sc, am_sc, wb_sc, pb_sc, va_sc, *, k_top, nkb, class_blocks):
    heads, tq, _ = q_ref.shape
    groups = heads // KV_HEADS
    nk = nkb * KEY_BLOCK
    qi = pl.program_id(1)
    row = qi * tq + lax.broadcasted_iota(I32, (tq, KEY_BLOCK), 0)
    lane = lax.broadcasted_iota(I32, (tq, KEY_BLOCK), 1)

    w_scale = (IDX_HEADS ** -0.5) * (IDX_DIM ** -0.5)
    iw = iw_ref[...]
    for h in range(IDX_HEADS):
        col = iw[:, IDX_DIM + h:IDX_DIM + h + 1] * w_scale
        wb_sc[h] = jnp.broadcast_to(col, (tq, KEY_BLOCK))
    per_slab = LANES // IDX_DIM
    iq_rows = jnp.concatenate(
        [iq_ref[h // per_slab][:, (h % per_slab) * IDX_DIM:(h % per_slab + 1) * IDX_DIM]
         for h in range(IDX_HEADS)], axis=0)

    for c0 in range(0, nkb, SCORE_CHUNK):
        nb = min(SCORE_CHUNK, nkb - c0)
        ikc = ik_ref[c0 * KEY_BLOCK:(c0 + nb) * KEY_BLOCK, 0:IDX_DIM].astype(BF16)
        dots = lax.dot_general(iq_rows, ikc, (((1,), (1,)), ((), ())),
                               preferred_element_type=F32)
        for sub in range(nb):
            kb = c0 + sub
            acc = jnp.zeros((tq, KEY_BLOCK), F32)
            for h in range(IDX_HEADS):
                d = dots[h * tq:(h + 1) * tq, sub * KEY_BLOCK:(sub + 1) * KEY_BLOCK]
                acc = acc + jnp.maximum(d, 0.0) * wb_sc[h]
            score = jnp.where(kb * KEY_BLOCK + lane <= row, acc, -jnp.inf)
            bits = pltpu.bitcast(score, I32)
            key = bits ^ ((bits >> 31) & 0x7FFFFFFF)
            keys_sc[kb] = key
            keyst_sc[kb] = key.T.reshape(KEY_BLOCK // SUBLANES, SUBLANES, tq)

    def count_where(pred, ref8):
        parts = [jnp.sum(pred(keyst_sc[kb], ref8).astype(I32), axis=0) for kb in range(nkb)]
        return jnp.sum(_tree_sum(parts), axis=0, keepdims=True)

    def bit_step(it, carry):
        thr, cnt = carry
        cand = thr + lax.shift_left(jnp.int32(1), 31 - it)
        total = count_where(jnp.greater_equal, jnp.broadcast_to(cand, (SUBLANES, tq))[None])
        take = total >= k_top
        return jnp.where(take, cand, thr), jnp.where(take, total, cnt)

    thr, cnt = lax.fori_loop(0, 32, bit_step, (jnp.full((1, tq), INT32_MIN, I32),
                                               jnp.full((1, tq), nk, I32)))
    full_row = thr > KEY_NEG_INF
    thr = jnp.maximum(thr, KEY_NEG_INF + 1)
    has_ties = jnp.max(jnp.where(full_row, cnt, 0)) > k_top

    thr_b = jnp.broadcast_to(thr, (KEY_BLOCK, tq)).T
    for kb in range(nkb):
        am_sc[kb] = jnp.where(keys_sc[kb] >= thr_b, 0.0, NEG_LOGIT)

    @pl.when(has_ties)
    def _():
        thr_t = jnp.broadcast_to(thr, (KEY_BLOCK, tq))
        n_eq = count_where(jnp.equal, jnp.broadcast_to(thr, (SUBLANES, tq))[None])
        keep = jnp.where(full_row, k_top - (cnt - n_eq), nk).astype(F32)
        keep_t = jnp.broadcast_to(keep, (KEY_BLOCK, tq))
        ri = lax.broadcasted_iota(I32, (KEY_BLOCK, KEY_BLOCK), 0)
        ci = lax.broadcasted_iota(I32, (KEY_BLOCK, KEY_BLOCK), 1)
        lower = jnp.where(ri >= ci, 1.0, 0.0).astype(BF16)
        seen = jnp.zeros((1, tq), F32)
        for kb in range(nkb):
            kt = keyst_sc[kb].reshape(KEY_BLOCK, tq)
            eq = jnp.where(kt == thr_t, 1.0, 0.0)
            rank = jnp.dot(lower, eq.astype(BF16), preferred_element_type=F32) + seen
            seen = rank[KEY_BLOCK - 1:KEY_BLOCK, :]
            tie_ok = jnp.where(rank <= keep_t, eq, 0.0)
            sel = jnp.where(kt > thr_t, 1.0, tie_ok)
            am_sc[kb] = jnp.where(sel.T > 0.5, 0.0, NEG_LOGIT)

    first_near = max(nkb - class_blocks - 1, 0)

    def logit(s, g, j, kb):
        sj = (s[j * tq:(j + 1) * tq, kb * KEY_BLOCK:(kb + 1) * KEY_BLOCK] + am_sc[kb])
        if kb >= first_near:
            sj = sj + bt_ref[g * groups + j, jnp.clip(qi - kb, 0, 2)]
        return sj

    scale2 = (HEAD_DIM ** -0.5) * LOG2E

    def raw_logits(g):
        qg = q_ref[g * groups:(g + 1) * groups].reshape(groups * tq, HEAD_DIM)
        return lax.dot_general(qg, k_ref[g, 0:nk, :], (((1,), (1,)), ((), ())),
                               preferred_element_type=F32) * scale2

    s_next = raw_logits(0)
    for g in range(KV_HEADS):
        s = s_next
        if g + 1 < KV_HEADS:
            s_next = raw_logits(g + 1)
        for j in range(groups):
            m_run = logit(s, g, j, 0)
            for kb in range(1, nkb):
                m_run = jnp.maximum(m_run, logit(s, g, j, kb))
            m_b = jnp.broadcast_to(jnp.max(m_run, axis=1, keepdims=True), (tq, KEY_BLOCK))
            for kb in range(nkb):
                pb_sc[g, j * tq:(j + 1) * tq, kb * KEY_BLOCK:(kb + 1) * KEY_BLOCK] = (
                    jnp.exp2(logit(s, g, j, kb) - m_b).astype(BF16))
        o = jnp.dot(pb_sc[g, :, 0:nk], va_sc[g, 0:nk, :], preferred_element_type=F32)
        out = o[:, 0:HEAD_DIM] / o[:, HEAD_DIM:2 * HEAD_DIM]
        for j in range(groups):
            hd = (g * groups + j) * HEAD_DIM
            y_ref[:, hd:hd + HEAD_DIM] = out[j * tq:(j + 1) * tq].astype(y_ref.dtype)


def _attn_kernel(*refs, k_top, class_blocks, nq):
    qi = pl.program_id(1)
    v_ref, va_sc = refs[2], refs[-1]

    @pl.when(qi == 0)
    def _():
        va_sc[:, :, 0:HEAD_DIM] = v_ref[...]
        va_sc[:, :, HEAD_DIM:2 * HEAD_DIM] = jnp.ones(v_ref.shape, v_ref.dtype)

    for c in range(-(-nq // class_blocks)):
        nkb = min((c + 1) * class_blocks, nq)

        @pl.when(qi // class_blocks == c)
        def _(nkb=nkb):
            _attn_body(*refs, k_top=k_top, nkb=nkb, class_blocks=class_blocks)


def _attention(pb, ps, btab, *, bsz, heads, tq, class_blocks):
    s = ps.shape[1]
    nq = s // tq
    iq_slabs = IDX_HEADS * IDX_DIM // LANES
    assert HEAD_DIM == LANES and heads % KV_HEADS == 0 and heads % iq_slabs == 0
    assert pb.shape[0] == heads + 2 * KV_HEADS + iq_slabs
    nkb_max = s // KEY_BLOCK
    groups = heads // KV_HEADS
    k_top = min(TOPK_MAX, s // 4)
    assert tq == KEY_BLOCK and nq % class_blocks == 0
    return pl.pallas_call(
        functools.partial(_attn_kernel, k_top=k_top, class_blocks=class_blocks, nq=nq),
        grid=(bsz, nq),
        in_specs=[pl.BlockSpec((heads, tq, LANES), lambda b, i: (0, b * nq + i, 0)),
                  pl.BlockSpec((KV_HEADS, s, LANES), lambda b, i: (heads // KV_HEADS, b, 0)),
                  pl.BlockSpec((KV_HEADS, s, LANES), lambda b, i: (heads // KV_HEADS + 1, b, 0)),
                  pl.BlockSpec((iq_slabs, tq, LANES),
                               lambda b, i: ((heads + 2 * KV_HEADS) // iq_slabs, b * nq + i, 0)),
                  pl.BlockSpec((None, s, LANES), lambda b, i: (b, 0, 0)),
                  pl.BlockSpec((None, tq, LANES), lambda b, i: (b, i, 0)),
                  pl.BlockSpec(btab.shape, lambda b, i: (0, 0, 0, 0))],
        out_specs=pl.BlockSpec((tq, heads * HEAD_DIM), lambda b, i: (b * nq + i, 0)),
        out_shape=jax.ShapeDtypeStruct((bsz * s, heads * HEAD_DIM), BF16),
        scratch_shapes=[
            pltpu.VMEM((nkb_max, tq, KEY_BLOCK), I32),
            pltpu.VMEM((nkb_max, KEY_BLOCK // SUBLANES, SUBLANES, tq), I32),
            pltpu.VMEM((nkb_max, tq, KEY_BLOCK), F32),
            pltpu.VMEM((IDX_HEADS, tq, KEY_BLOCK), F32),
            pltpu.VMEM((KV_HEADS, groups * tq, s), BF16),
            pltpu.VMEM((KV_HEADS, s, 2 * HEAD_DIM), BF16)],
        compiler_params=_params(("parallel", "arbitrary")),
        name="sparse_attn",
    )(pb, pb, pb, pb, ps, ps, btab)


FFN_TM, FFN_TF = 512, 512
MM_TM, MM_TN = 1024, 1024
RGLRU_TS, RGLRU_TC = 512, 512
ATTN_CLASS_BLOCKS = 4


def _layer(x, ffn1_norm, ffn1_w_gate, ffn1_w_up, ffn1_w_down, mix_norm, w_in, conv_w, conv_b,
           rg_w_a, rg_b_a, rg_w_x, rg_b_x, rg_lambda, btab, w_proj_rnn, w_proj_attn, w_out,
           ffn2_norm, ffn2_w_gate, ffn2_w_up, ffn2_w_down, next_norm, *, last):
    bsz, s, d = x.shape
    m = bsz * s
    width = d
    q_width = d
    kv_width = KV_HEADS * HEAD_DIM
    iq_width = IDX_HEADS * IDX_DIM
    tm = min(FFN_TM, m)
    tmm = min(MM_TM, m)

    o_q = 2 * width
    o_ik = o_q + q_width + 2 * kv_width + iq_width
    o_gr = o_ik + IDX_DIM + IDX_HEADS

    bf = lambda w: w.astype(BF16)
    vec = lambda p: p.reshape(1, -1)

    x1, hn = _ffn(x.reshape(m, d), vec(ffn1_norm), bf(ffn1_w_gate), bf(ffn1_w_up), ffn1_w_down,
                  vec(mix_norm), emit_resid=True, tm=tm, tf=FFN_TF)

    w_t = w_in.T
    tmi = tmm
    pf = _in_proj(hn, w_t, 0, o_q, F32, tm=tmi, tn=MM_TN, name="in_proj_rnn")
    pb = _in_proj(hn, w_t, o_q, o_ik - o_q, BF16, tm=tmi, tn=MM_TN, name="in_proj_attn",
                  lane_blocks=True)
    ps = _in_proj(hn, w_t, o_ik, LANES, F32, tm=tmi, tn=LANES, name="in_proj_idx")
    pg = _in_proj(hn, w_t, o_gr, 2 * d, F32, tm=tmi, tn=MM_TN, name="in_proj_gates")

    wax = bf(0.5 * jnp.concatenate([rg_w_a, rg_w_x], axis=-1))
    y_rnn = _rglru(pf.reshape(bsz, s, -1), 0, width, conv_w, vec(conv_b), wax,
                   vec(0.5 * rg_b_a), vec(0.5 * rg_b_x), vec(rg_lambda),
                   width=width, ts=min(RGLRU_TS, s), tc=RGLRU_TC)

    y_attn = _attention(pb, ps.reshape(bsz, s, LANES), btab, bsz=bsz, heads=q_width // HEAD_DIM,
                        tq=KEY_BLOCK, class_blocks=ATTN_CLASS_BLOCKS)

    merged = _merge(y_rnn.reshape(m, width), y_attn,
                    bf(w_proj_rnn), bf(w_proj_attn), pg, 0, d, tm=tm, tn=MM_TN)
    x2 = _matmul_residual(merged, w_out, x1, tm=tmm, tn=MM_TN)

    outs = _ffn(x2, vec(ffn2_norm), bf(ffn2_w_gate), bf(ffn2_w_up), ffn2_w_down,
                vec(next_norm), emit_resid=not last, tm=tm, tf=FFN_TF)
    return outs[0].reshape(bsz, s, d)


def kernel(x, ffn1_norm, ffn1_w_gate, ffn1_w_up, ffn1_w_down, mix_norm, w_in, conv_w, conv_b,
           rg_w_a, rg_b_a, rg_w_x, rg_b_x, rg_lambda, rel_bias, w_proj_rnn, w_proj_attn, w_out,
           ffn2_norm, ffn2_w_gate, ffn2_w_up, ffn2_w_down, final_norm):
    depth = ffn1_norm.shape[0]
    assert depth == 1, "the fused final norm assumes a single layer"
    btab = _bias_tiles(rel_bias)
    l = 0
    return _layer(x, ffn1_norm[l], ffn1_w_gate[l], ffn1_w_up[l], ffn1_w_down[l], mix_norm[l],
                  w_in[l], conv_w[l], conv_b[l], rg_w_a[l], rg_b_a[l], rg_w_x[l], rg_b_x[l],
                  rg_lambda[l], btab, w_proj_rnn[l], w_proj_attn[l], w_out[l], ffn2_norm[l],
                  ffn2_w_gate[l], ffn2_w_up[l], ffn2_w_down[l], final_norm, last=True)
```

```python
import functools
import math

import jax
import jax.numpy as jnp
from jax import lax
from jax.experimental import pallas as pl
from jax.experimental.pallas import tpu as pltpu

F32 = jnp.float32
BF16 = jnp.bfloat16
I32 = jnp.int32

RMS_EPS = 1e-6
CONV_WIDTH = 4
RG_C = 8.0
RNN_BLOCK = 128
HEAD_DIM = 128
KV_HEADS = 4
IDX_HEADS = 16
IDX_DIM = 64
TOPK_MAX = 256
NUM_BUCKETS = 32
MAX_DISTANCE = 128

LANES = 128
SUBLANES = 8
VMEM_LIMIT_BYTES = 58 * 1024 * 1024

KEY_BLOCK = 128
NEG_LOGIT = -1e30
INT32_MIN = -(2 ** 31)
KEY_NEG_INF = -2139095041
LOG2E = 1.4426950408889634


def _params(semantics):
    return pltpu.CompilerParams(dimension_semantics=semantics,
                                vmem_limit_bytes=VMEM_LIMIT_BYTES)


def _sigmoid(x):
    return 0.5 * (jnp.tanh(0.5 * x) + 1.0)


def _rms(x, g):
    ms = jnp.mean(x * x, axis=-1, keepdims=True)
    return x * lax.rsqrt(ms + RMS_EPS) * g


def _tree_sum(parts):
    while len(parts) > 1:
        parts = [a + b for a, b in zip(parts[::2], parts[1::2])] + (
            [parts[-1]] if len(parts) % 2 else [])
    return parts[0]


def _ffn_kernel(x_ref, g_ref, wg_ref, wu_ref, wd_ref, wgt_ref, wut_ref, wdt_ref, gn_ref, *rest,
                emit_resid, nfull):
    if emit_resid:
        acc_ref, hn_ref, xn_sc = rest
    else:
        acc_ref, xn_sc = rest
        hn_ref = acc_ref
    j = pl.program_id(1)

    @pl.when(j == 0)
    def _():
        xn_sc[...] = _rms(x_ref[...], g_ref[...]).astype(BF16)
        acc_ref[...] = jnp.zeros_like(acc_ref)

    def accumulate(wg, wu, wd):
        xn = xn_sc[...]
        h = jnp.dot(xn, wg, preferred_element_type=F32)
        u = jnp.dot(xn, wu, preferred_element_type=F32)
        a = (h * _sigmoid(h) * u).astype(BF16)
        acc_ref[...] += jnp.dot(a, wd, preferred_element_type=F32)

    @pl.when(j < nfull)
    def _():
        accumulate(wg_ref[...], wu_ref[...], wd_ref[...])

    @pl.when(j == nfull)
    def _():
        accumulate(wgt_ref[...], wut_ref[...], wdt_ref[...])
        y = x_ref[...] + 0.5 * acc_ref[...]
        if emit_resid:
            acc_ref[...] = y
        hn_ref[...] = _rms(y, gn_ref[...]).astype(hn_ref.dtype)


def _ffn(x, g, wg, wu, wd, gn, *, emit_resid, tm, tf):
    m, d = x.shape
    f = wg.shape[1]
    nfull = (f - 1) // tf
    tail = f - nfull * tf
    assert nfull >= 1 and tail % LANES == 0
    wgt, wut, wdt = wg[:, nfull * tf:], wu[:, nfull * tf:], wd[nfull * tf:, :]
    last = nfull - 1
    stream = pl.Buffered(2)
    fixed = pl.Buffered(1)
    row = pl.BlockSpec((tm, d), lambda i, j: (i, 0))
    vec = pl.BlockSpec((1, d), lambda i, j: (0, 0))
    out_shape = [jax.ShapeDtypeStruct((m, d), F32)]
    out_specs = [row]
    if emit_resid:
        out_shape.append(jax.ShapeDtypeStruct((m, d), BF16))
        out_specs.append(row)
    return pl.pallas_call(
        functools.partial(_ffn_kernel, emit_resid=emit_resid, nfull=nfull),
        grid=(m // tm, nfull + 1),
        in_specs=[row, vec,
                  pl.BlockSpec((d, tf), lambda i, j: (0, jnp.minimum(j, last)), pipeline_mode=stream),
                  pl.BlockSpec((d, tf), lambda i, j: (0, jnp.minimum(j, last)), pipeline_mode=stream),
                  pl.BlockSpec((tf, d), lambda i, j: (jnp.minimum(j, last), 0), pipeline_mode=stream),
                  pl.BlockSpec((d, tail), lambda i, j: (0, 0), pipeline_mode=fixed),
                  pl.BlockSpec((d, tail), lambda i, j: (0, 0), pipeline_mode=fixed),
                  pl.BlockSpec((tail, d), lambda i, j: (0, 0), pipeline_mode=fixed),
                  vec],
        out_specs=out_specs,
        out_shape=out_shape,
        scratch_shapes=[pltpu.VMEM((tm, d), BF16)],
        compiler_params=_params(("parallel", "arbitrary")),
        name="ffn_resid" if emit_resid else "ffn_final",
    )(x, g, wg, wu, wd, wgt, wut, wdt, gn)


def _in_proj_kernel(x_ref, wt_ref, o_ref, wb_sc, *, lane_blocks):
    @pl.when(pl.program_id(1) == 0)
    def _():
        wb_sc[...] = wt_ref[...].T.astype(BF16)

    res = jnp.dot(x_ref[...], wb_sc[...], preferred_element_type=F32)
    if lane_blocks:
        for c in range(res.shape[1] // LANES):
            o_ref[c] = res[:, c * LANES:(c + 1) * LANES].astype(o_ref.dtype)
    else:
        o_ref[...] = res.astype(o_ref.dtype)


def _in_proj(x, wt, row0, n, out_dtype, *, tm, tn, name, lane_blocks=False):
    m, k = x.shape
    assert row0 % SUBLANES == 0 and n % tn == 0
    if lane_blocks:
        out_shape = jax.ShapeDtypeStruct((n // LANES, m, LANES), out_dtype)
        out_spec = pl.BlockSpec((tn // LANES, tm, LANES), lambda j, i: (j, i, 0))
    else:
        out_shape = jax.ShapeDtypeStruct((m, n), out_dtype)
        out_spec = pl.BlockSpec((tm, tn), lambda j, i: (i, j))
    return pl.pallas_call(
        functools.partial(_in_proj_kernel, lane_blocks=lane_blocks),
        grid=(n // tn, m // tm),
        in_specs=[pl.BlockSpec((tm, k), lambda j, i: (i, 0)),
                  pl.BlockSpec((pl.Element(tn), pl.Element(k)),
                               lambda j, i: (pl.multiple_of(row0 + j * tn, SUBLANES), 0))],
        out_specs=out_spec,
        out_shape=out_shape,
        scratch_shapes=[pltpu.VMEM((k, tn), BF16)],
        compiler_params=_params(("parallel", "arbitrary")),
        name=name,
    )(x, wt)


def _mm_res_kernel(x_ref, w_ref, r_ref, o_ref):
    o_ref[...] = r_ref[...] + jnp.dot(x_ref[...], w_ref[...], preferred_element_type=F32)


def _matmul_residual(x, w, r, *, tm, tn):
    m, k = x.shape
    n = w.shape[1]
    return pl.pallas_call(
        _mm_res_kernel,
        grid=(n // tn, m // tm),
        in_specs=[pl.BlockSpec((tm, k), lambda j, i: (i, 0)),
                  pl.BlockSpec((k, tn), lambda j, i: (0, j)),
                  pl.BlockSpec((tm, tn), lambda j, i: (i, j))],
        out_specs=pl.BlockSpec((tm, tn), lambda j, i: (i, j)),
        out_shape=jax.ShapeDtypeStruct((m, n), F32),
        compiler_params=_params(("parallel", "parallel")),
        name="out_proj",
    )(x, w, r)


def _merge_kernel(yr_ref, ya_ref, wr_ref, wa_ref, gr_ref, ga_ref, o_ref):
    pr = jnp.dot(yr_ref[...], wr_ref[...], preferred_element_type=F32)
    pa = jnp.dot(ya_ref[...], wa_ref[...], preferred_element_type=F32)
    o_ref[...] = (_sigmoid(gr_ref[...]) * pr + _sigmoid(ga_ref[...]) * pa).astype(o_ref.dtype)


def _merge(y_rnn, y_attn, w_r, w_a, pf, gr_col, ga_col, *, tm, tn):
    m, k = y_rnn.shape
    n = w_r.shape[1]
    act = pl.BlockSpec((tm, k), lambda j, i: (i, 0))
    wsp = pl.BlockSpec((k, tn), lambda j, i: (0, j))
    return pl.pallas_call(
        _merge_kernel,
        grid=(n // tn, m // tm),
        in_specs=[act, act, wsp, wsp,
                  pl.BlockSpec((tm, tn), lambda j, i: (i, gr_col // tn + j)),
                  pl.BlockSpec((tm, tn), lambda j, i: (i, ga_col // tn + j))],
        out_specs=pl.BlockSpec((tm, tn), lambda j, i: (i, j)),
        out_shape=jax.ShapeDtypeStruct((m, n), BF16),
        compiler_params=_params(("parallel", "parallel")),
        name="merge",
    )(y_rnn, y_attn, w_r, w_a, pf, pf)


def _rglru_kernel(rx_ref, rg_ref, cw_ref, cb_ref, wax_ref, ba_ref, bx_ref, lam_ref,
                  y_ref, xs_sc, h_sc):
    ts, tc = rx_ref.shape
    ng = ts // SUBLANES

    @pl.when(pl.program_id(2) == 0)
    def _():
        xs_sc[...] = jnp.zeros_like(xs_sc)
        h_sc[...] = jnp.zeros_like(h_sc)

    ri = lax.broadcasted_iota(I32, (1, SUBLANES, tc), 1)
    x = rx_ref[...]
    x3 = x.reshape(ng, SUBLANES, tc)
    tail = xs_sc[...]
    xs_sc[...] = x[ts - SUBLANES:ts, :]
    cw = cw_ref[...]
    xc3 = cb_ref[...][None] + x3 * cw[CONV_WIDTH - 1:CONV_WIDTH, :][None]
    for d in range(1, CONV_WIDTH):
        rot = pltpu.roll(x3, d, axis=1)
        rot_tail = pltpu.roll(tail, d, axis=0)[None]
        rot_prev = jnp.concatenate([rot_tail, rot[:ng - 1]], axis=0)
        k = CONV_WIDTH - 1 - d
        xc3 = xc3 + jnp.where(ri >= d, rot, rot_prev) * cw[k:k + 1, :][None]
    xc = xc3.reshape(ts, tc)

    xcb = xc.astype(BF16)
    r_parts, i_parts = [], []
    for n in range(tc // RNN_BLOCK):
        g = jnp.dot(xcb[:, n * RNN_BLOCK:(n + 1) * RNN_BLOCK], wax_ref[n],
                    preferred_element_type=F32)
        r_parts.append(g[:, :RNN_BLOCK])
        i_parts.append(g[:, RNN_BLOCK:])
    tr = jnp.tanh(jnp.concatenate(r_parts, axis=1) + ba_ref[...])
    ti = jnp.tanh(jnp.concatenate(i_parts, axis=1) + bx_ref[...])

    nl = -lam_ref[...]
    softplus = jnp.maximum(nl, 0.0) + jnp.log1p(jnp.exp(-jnp.abs(nl)))
    log_a = (tr + 1.0) * ((-0.5 * RG_C) * softplus)
    a = jnp.exp(log_a)
    mult = jnp.sqrt(-jnp.tanh(log_a) * (a * a + 1.0))
    u = mult * ((0.5 * ti + 0.5) * xc)

    av = a.reshape(ng, SUBLANES, tc)
    bv = u.reshape(ng, SUBLANES, tc)
    for d in (1, 2, 4):
        a_prev = jnp.where(ri >= d, pltpu.roll(av, d, axis=1), 1.0)
        b_prev = jnp.where(ri >= d, pltpu.roll(bv, d, axis=1), 0.0)
        bv = av * b_prev + bv
        av = av * a_prev
    carry = h_sc[0:1, :]
    hs = []
    for k in range(ng):
        hk = bv[k] + av[k] * carry
        hs.append(hk)
        carry = hk[SUBLANES - 1:SUBLANES, :]
    h_sc[0:1, :] = carry
    h = jnp.concatenate(hs, axis=0)

    gt = rg_ref[...]
    c1 = math.sqrt(2.0 / math.pi)
    inner = gt * (c1 + (c1 * 0.044715) * (gt * gt))
    y_ref[...] = ((h * (0.5 * gt)) * (1.0 + jnp.tanh(inner))).astype(y_ref.dtype)


def _rglru(pf, rx_col, rg_col, conv_w, conv_b, wax, b_a, b_x, lam, *, width, ts, tc):
    bsz, s, _ = pf.shape
    chan = lambda rows: pl.BlockSpec((rows, tc), lambda b, c, t: (0, c))
    return pl.pallas_call(
        _rglru_kernel,
        grid=(bsz, width // tc, s // ts),
        in_specs=[pl.BlockSpec((None, ts, tc), lambda b, c, t: (b, t, rx_col // tc + c)),
                  pl.BlockSpec((None, ts, tc), lambda b, c, t: (b, t, rg_col // tc + c)),
                  chan(CONV_WIDTH), chan(1),
                  pl.BlockSpec((tc // RNN_BLOCK, RNN_BLOCK, 2 * RNN_BLOCK),
                               lambda b, c, t: (c, 0, 0)),
                  chan(1), chan(1), chan(1)],
        out_specs=pl.BlockSpec((None, ts, tc), lambda b, c, t: (b, t, c)),
        out_shape=jax.ShapeDtypeStruct((bsz, s, width), BF16),
        scratch_shapes=[pltpu.VMEM((SUBLANES, tc), F32),
                        pltpu.VMEM((SUBLANES, tc), F32)],
        compiler_params=_params(("parallel", "parallel", "arbitrary")),
        name="rglru",
    )(pf, pf, conv_w, conv_b, wax, b_a, b_x, lam)


def _bias_kernel(rb_ref, o_ref):
    h = pl.program_id(0)
    far = rb_ref[NUM_BUCKETS - 1, h]
    ii = lax.broadcasted_iota(I32, (KEY_BLOCK, KEY_BLOCK), 0)
    jj = lax.broadcasted_iota(I32, (KEY_BLOCK, KEY_BLOCK), 1)
    max_exact = NUM_BUCKETS // 2
    for d in range(2):
        n = jnp.maximum(ii - jj + KEY_BLOCK * d, 0)
        nf = jnp.maximum(n, 1).astype(F32)
        large = max_exact + (jnp.log(nf / max_exact) / math.log(MAX_DISTANCE / max_exact)
                             * (NUM_BUCKETS - max_exact)).astype(I32)
        large = jnp.minimum(large, NUM_BUCKETS - 1)
        bucket = jnp.where(n < max_exact, n, large)
        acc = jnp.zeros((KEY_BLOCK, KEY_BLOCK), F32)
        for b in range(NUM_BUCKETS):
            acc = jnp.where(bucket == b, rb_ref[b, h], acc)
        o_ref[0, d] = (acc - far) * LOG2E
    o_ref[0, 2] = jnp.zeros((KEY_BLOCK, KEY_BLOCK), F32)


def _bias_tiles(rel_bias):
    heads = rel_bias.shape[1]
    return pl.pallas_call(
        _bias_kernel,
        grid=(heads,),
        in_specs=[pl.BlockSpec(memory_space=pltpu.SMEM)],
        out_specs=pl.BlockSpec((1, 3, KEY_BLOCK, KEY_BLOCK), lambda h: (h, 0, 0, 0)),
        out_shape=jax.ShapeDtypeStruct((heads, 3, KEY_BLOCK, KEY_BLOCK), F32),
        compiler_params=_params(("parallel",)),
        name="bias_tiles",
    )(rel_bias)


SCORE_CHUNK = 4


def _attn_body(q_ref, k_ref, v_ref, iq_ref, ik_ref, iw_ref, bt_ref, y_ref,
               keys_sc, keyst_sc, am_sc, wb_sc, pb_sc, va_sc, *, k_top, nkb, class_blocks):
    heads, tq, _ = q_ref.shape
    groups = heads // KV_HEADS
    nk = nkb * KEY_BLOCK
    qi = pl.program_id(1)
    row = qi * tq + lax.broadcasted_iota(I32, (tq, KEY_BLOCK), 0)
    lane = lax.broadcasted_iota(I32, (tq, KEY_BLOCK), 1)

    w_scale = (IDX_HEADS ** -0.5) * (IDX_DIM ** -0.5)
    iw = iw_ref[...]
    for h in range(IDX_HEADS):
        col = iw[:, IDX_DIM + h:IDX_DIM + h + 1] * w_scale
        wb_sc[h] = jnp.broadcast_to(col, (tq, KEY_BLOCK))
    per_slab = LANES // IDX_DIM
    iq_rows = jnp.concatenate(
        [iq_ref[h // per_slab][:, (h % per_slab) * IDX_DIM:(h % per_slab + 1) * IDX_DIM]
         for h in range(IDX_HEADS)], axis=0)

    for c0 in range(0, nkb, SCORE_CHUNK):
        nb = min(SCORE_CHUNK, nkb - c0)
        ikc = ik_ref[c0 * KEY_BLOCK:(c0 + nb) * KEY_BLOCK, 0:IDX_DIM].astype(BF16)
        dots = lax.dot_general(iq_rows, ikc, (((1,), (1,)), ((), ())),
                               preferred_element_type=F32)
        for sub in range(nb):
            kb = c0 + sub
            acc = jnp.zeros((tq, KEY_BLOCK), F32)
            for h in range(IDX_HEADS):
                d = dots[h * tq:(h + 1) * tq, sub * KEY_BLOCK:(sub + 1) * KEY_BLOCK]
                acc = acc + jnp.maximum(d, 0.0) * wb_sc[h]
            score = jnp.where(kb * KEY_BLOCK + lane <= row, acc, -jnp.inf)
            bits = pltpu.bitcast(score, I32)
            key = bits ^ ((bits >> 31) & 0x7FFFFFFF)
            keys_sc[kb] = key
            keyst_sc[kb] = key.T.reshape(KEY_BLOCK // SUBLANES, SUBLANES, tq)

    def count_where(pred, ref8):
        parts = [jnp.sum(pred(keyst_sc[kb], ref8).astype(I32), axis=0) for kb in range(nkb)]
        return jnp.sum(_tree_sum(parts), axis=0, keepdims=True)

    def bit_step(it, carry):
        thr, cnt = carry
        cand = thr + lax.shift_left(jnp.int32(1), 31 - it)
        total = count_where(jnp.greater_equal, jnp.broadcast_to(cand, (SUBLANES, tq))[None])
        take = total >= k_top
        return jnp.where(take, cand, thr), jnp.where(take, total, cnt)

    thr, cnt = lax.fori_loop(0, 32, bit_step, (jnp.full((1, tq), INT32_MIN, I32),
                                               jnp.full((1, tq), nk, I32)))
    full_row = thr > KEY_NEG_INF
    thr = jnp.maximum(thr, KEY_NEG_INF + 1)
    has_ties = jnp.max(jnp.where(full_row, cnt, 0)) > k_top

    thr_b = jnp.broadcast_to(thr, (KEY_BLOCK, tq)).T
    for kb in range(nkb):
        am_sc[kb] = jnp.where(keys_sc[kb] >= thr_b, 0.0, NEG_LOGIT)

    @pl.when(has_ties)
    def _():
        thr_t = jnp.broadcast_to(thr, (KEY_BLOCK, tq))
        n_eq = count_where(jnp.equal, jnp.broadcast_to(thr, (SUBLANES, tq))[None])
        keep = jnp.where(full_row, k_top - (cnt - n_eq), nk).astype(F32)
        keep_t = jnp.broadcast_to(keep, (KEY_BLOCK, tq))
        ri = lax.broadcasted_iota(I32, (KEY_BLOCK, KEY_BLOCK), 0)
        ci = lax.broadcasted_iota(I32, (KEY_BLOCK, KEY_BLOCK), 1)
        lower = jnp.where(ri >= ci, 1.0, 0.0).astype(BF16)
        seen = jnp.zeros((1, tq), F32)
        for kb in range(nkb):
            kt = keyst_sc[kb].reshape(KEY_BLOCK, tq)
            eq = jnp.where(kt == thr_t, 1.0, 0.0)
            rank = jnp.dot(lower, eq.astype(BF16), preferred_element_type=F32) + seen
            seen = rank[KEY_BLOCK - 1:KEY_BLOCK, :]
            tie_ok = jnp.where(rank <= keep_t, eq, 0.0)
            sel = jnp.where(kt > thr_t, 1.0, tie_ok)
            am_sc[kb] = jnp.where(sel.T > 0.5, 0.0, NEG_LOGIT)

    first_near = max(nkb - class_blocks - 1, 0)

    def logit(s, g, j, kb):
        sj = (s[j * tq:(j + 1) * tq, kb * KEY_BLOCK:(kb + 1) * KEY_BLOCK] + am_sc[kb])
        if kb >= first_near:
            sj = sj + bt_ref[g * groups + j, jnp.clip(qi - kb, 0, 2)]
        return sj

    scale2 = (HEAD_DIM ** -0.5) * LOG2E

    def raw_logits(g):
        qg = q_ref[g * groups:(g + 1) * groups].reshape(groups * tq, HEAD_DIM)
        return lax.dot_general(qg, k_ref[g, 0:nk, :], (((1,), (1,)), ((), ())),
                               preferred_element_type=F32) * scale2

    s_next = raw_logits(0)
    for g in range(KV_HEADS):
        s = s_next
        if g + 1 < KV_HEADS:
            s_next = raw_logits(g + 1)
        for j in range(groups):
            m_run = logit(s, g, j, 0)
            for kb in range(1, nkb):
                m_run = jnp.maximum(m_run, logit(s, g, j, kb))
            m_b = jnp.broadcast_to(jnp.max(m_run, axis=1, keepdims=True), (tq, KEY_BLOCK))
            for kb in range(nkb):
                pb_sc[g, j * tq:(j + 1) * tq, kb * KEY_BLOCK:(kb + 1) * KEY_BLOCK] = (
                    jnp.exp2(logit(s, g, j, kb) - m_b).astype(BF16))
        o = jnp.dot(pb_sc[g, :, 0:nk], va_sc[g, 0:nk, :], preferred_element_type=F32)
        out = o[:, 0:HEAD_DIM] / o[:, HEAD_DIM:2 * HEAD_DIM]
        for j in range(groups):
            hd = (g * groups + j) * HEAD_DIM
            y_ref[:, hd:hd + HEAD_DIM] = out[j * tq:(j + 1) * tq].astype(y_ref.dtype)


def _attn_kernel(*refs, k_top, class_blocks, nq):
    qi = pl.program_id(1)
    v_ref, va_sc = refs[2], refs[-1]

    @pl.when(qi == 0)
    def _():
        va_sc[:, :, 0:HEAD_DIM] = v_ref[...]
        va_sc[:, :, HEAD_DIM:2 * HEAD_DIM] = jnp.ones(v_ref.shape, v_ref.dtype)

    for c in range(-(-nq // class_blocks)):
        nkb = min((c + 1) * class_blocks, nq)

        @pl.when(qi // class_blocks == c)
        def _(nkb=nkb):
            _attn_body(*refs, k_top=k_top, nkb=nkb, class_blocks=class_blocks)


def _attention(pb, ps, btab, *, bsz, heads, tq, class_blocks):
    s = ps.shape[1]
    nq = s // tq
    iq_slabs = IDX_HEADS * IDX_DIM // LANES
    assert HEAD_DIM == LANES and heads % KV_HEADS == 0 and heads % iq_slabs == 0
    assert pb.shape[0] == heads + 2 * KV_HEADS + iq_slabs
    nkb_max = s // KEY_BLOCK
    groups = heads // KV_HEADS
    k_top = min(TOPK_MAX, s // 4)
    assert tq == KEY_BLOCK and nq % class_blocks == 0
    return pl.pallas_call(
        functools.partial(_attn_kernel, k_top=k_top, class_blocks=class_blocks, nq=nq),
        grid=(bsz, nq),
        in_specs=[pl.BlockSpec((heads, tq, LANES), lambda b, i: (0, b * nq + i, 0)),
                  pl.BlockSpec((KV_HEADS, s, LANES), lambda b, i: (heads // KV_HEADS, b, 0)),
                  pl.BlockSpec((KV_HEADS, s, LANES), lambda b, i: (heads // KV_HEADS + 1, b, 0)),
                  pl.BlockSpec((iq_slabs, tq, LANES),
                               lambda b, i: ((heads + 2 * KV_HEADS) // iq_slabs, b * nq + i, 0)),
                  pl.BlockSpec((None, s, LANES), lambda b, i: (b, 0, 0)),
                  pl.BlockSpec((None, tq, LANES), lambda b, i: (b, i, 0)),
                  pl.BlockSpec(btab.shape, lambda b, i: (0, 0, 0, 0))],
        out_specs=pl.BlockSpec((tq, heads * HEAD_DIM), lambda b, i: (b * nq + i, 0)),
        out_shape=jax.ShapeDtypeStruct((bsz * s, heads * HEAD_DIM), BF16),
        scratch_shapes=[
            pltpu.VMEM((nkb_max, tq, KEY_BLOCK), I32),
            pltpu.VMEM((nkb_max, KEY_BLOCK // SUBLANES, SUBLANES, tq), I32),
            pltpu.VMEM((nkb_max, tq, KEY_BLOCK), F32),
            pltpu.VMEM((IDX_HEADS, tq, KEY_BLOCK), F32),
            pltpu.VMEM((KV_HEADS, groups * tq, s), BF16),
            pltpu.VMEM((KV_HEADS, s, 2 * HEAD_DIM), BF16)],
        compiler_params=_params(("parallel", "arbitrary")),
        name="sparse_attn",
    )(pb, pb, pb, pb, ps, ps, btab)


FFN_TM, FFN_TF = 1024, 256
MM_TM, MM_TN = 1024, 1024
MERGE_TM = 512
RGLRU_TS, RGLRU_TC = 512, 512
ATTN_CLASS_BLOCKS = 4


def _layer(x, ffn1_norm, ffn1_w_gate, ffn1_w_up, ffn1_w_down, mix_norm, w_in, conv_w, conv_b,
           rg_w_a, rg_b_a, rg_w_x, rg_b_x, rg_lambda, btab, w_proj_rnn, w_proj_attn, w_out,
           ffn2_norm, ffn2_w_gate, ffn2_w_up, ffn2_w_down, next_norm, *, last):
    bsz, s, d = x.shape
    m = bsz * s
    width = d
    q_width = d
    kv_width = KV_HEADS * HEAD_DIM
    iq_width = IDX_HEADS * IDX_DIM
    tm = min(FFN_TM, m)
    tmm = min(MM_TM, m)

    o_q = 2 * width
    o_ik = o_q + q_width + 2 * kv_width + iq_width
    o_gr = o_ik + IDX_DIM + IDX_HEADS

    bf = lambda w: w.astype(BF16)
    vec = lambda p: p.reshape(1, -1)

    x1, hn = _ffn(x.reshape(m, d), vec(ffn1_norm), bf(ffn1_w_gate), bf(ffn1_w_up), bf(ffn1_w_down),
                  vec(mix_norm), emit_resid=True, tm=tm, tf=FFN_TF)

    w_t = w_in.T
    tmi = tmm
    pf = _in_proj(hn, w_t, 0, o_q, F32, tm=tmi, tn=MM_TN, name="in_proj_rnn")
    pb = _in_proj(hn, w_t, o_q, o_ik - o_q, BF16, tm=tmi, tn=MM_TN, name="in_proj_attn",
                  lane_blocks=True)
    ps = _in_proj(hn, w_t, o_ik, LANES, F32, tm=tmi, tn=LANES, name="in_proj_idx")
    pg = _in_proj(hn, w_t, o_gr, 2 * d, F32, tm=tmi, tn=MM_TN, name="in_proj_gates")

    wax = bf(0.5 * jnp.concatenate([rg_w_a, rg_w_x], axis=-1))
    y_rnn = _rglru(pf.reshape(bsz, s, -1), 0, width, conv_w, vec(conv_b), wax,
                   vec(0.5 * rg_b_a), vec(0.5 * rg_b_x), vec(rg_lambda),
                   width=width, ts=min(RGLRU_TS, s), tc=RGLRU_TC)

    y_attn = _attention(pb, ps.reshape(bsz, s, LANES), btab, bsz=bsz, heads=q_width // HEAD_DIM,
                        tq=KEY_BLOCK, class_blocks=ATTN_CLASS_BLOCKS)

    merged = _merge(y_rnn.reshape(m, width), y_attn,
                    bf(w_proj_rnn), bf(w_proj_attn), pg, 0, d, tm=min(MERGE_TM, m), tn=MM_TN)
    x2 = _matmul_residual(merged, bf(w_out), x1, tm=tmm, tn=MM_TN)

    outs = _ffn(x2, vec(ffn2_norm), bf(ffn2_w_gate), bf(ffn2_w_up), bf(ffn2_w_down),
                vec(next_norm), emit_resid=not last, tm=tm, tf=FFN_TF)
    return outs[0].reshape(bsz, s, d)


def kernel(x, ffn1_norm, ffn1_w_gate, ffn1_w_up, ffn1_w_down, mix_norm, w_in, conv_w, conv_b,
           rg_w_a, rg_b_a, rg_w_x, rg_b_x, rg_lambda, rel_bias, w_proj_rnn, w_proj_attn, w_out,
           ffn2_norm, ffn2_w_gate, ffn2_w_up, ffn2_w_down, final_norm):
    depth = ffn1_norm.shape[0]
    assert depth == 1, "the fused final norm assumes a single layer"
    btab = _bias_tiles(rel_bias)
    l = 0
    return _layer(x, ffn1_norm[l], ffn1_w_gate[l], ffn1_w_up[l], ffn1_w_down[l], mix_norm[l],
                  w_in[l], conv_w[l], conv_b[l], rg_w_a[l], rg_b_a[l], rg_w_x[l], rg_b_x[l],
                  rg_lambda[l], btab, w_proj_rnn[l], w_proj_attn[l], w_out[l], ffn2_norm[l],
                  ffn2_w_gate[l], ffn2_w_up[l], ffn2_w_down[l], final_norm, last=True)
```

```python
import functools
import math

import jax
import jax.numpy as jnp
from jax import lax
from jax.experimental import pallas as pl
from jax.experimental.pallas import tpu as pltpu

F32 = jnp.float32
BF16 = jnp.bfloat16
I32 = jnp.int32

RMS_EPS = 1e-6
CONV_WIDTH = 4
RG_C = 8.0
RNN_BLOCK = 128
HEAD_DIM = 128
KV_HEADS = 4
IDX_HEADS = 16
IDX_DIM = 64
TOPK_MAX = 256
NUM_BUCKETS = 32
MAX_DISTANCE = 128

LANES = 128
SUBLANES = 8
VMEM_LIMIT_BYTES = 56 * 1024 * 1024

KEY_BLOCK = 128
NEG_LOGIT = -1e30
INT32_MIN = -(2 ** 31)
KEY_NEG_INF = -2139095041
LOG2E = 1.4426950408889634


def _params(semantics):
    return pltpu.CompilerParams(dimension_semantics=semantics,
                                vmem_limit_bytes=VMEM_LIMIT_BYTES)


def _sigmoid(x):
    return 0.5 * (jnp.tanh(0.5 * x) + 1.0)


def _rms(x, g):
    ms = jnp.mean(x * x, axis=-1, keepdims=True)
    return x * lax.rsqrt(ms + RMS_EPS) * g


def _tree_sum(parts):
    while len(parts) > 1:
        parts = [a + b for a, b in zip(parts[::2], parts[1::2])] + (
            [parts[-1]] if len(parts) % 2 else [])
    return parts[0]


def _ffn_kernel(x_ref, g_ref, wg_ref, wu_ref, wd_ref, wgt_ref, wut_ref, wdt_ref, gn_ref, *rest,
                emit_resid, nfull):
    if emit_resid:
        acc_ref, hn_ref, xn_sc = rest
    else:
        acc_ref, xn_sc = rest
        hn_ref = acc_ref
    j = pl.program_id(1)

    @pl.when(j == 0)
    def _():
        xn_sc[...] = _rms(x_ref[...], g_ref[...]).astype(BF16)
        acc_ref[...] = jnp.zeros_like(acc_ref)

    def accumulate(wg, wu, wd):
        xn = xn_sc[...]
        h = jnp.dot(xn, wg, preferred_element_type=F32)
        u = jnp.dot(xn, wu, preferred_element_type=F32)
        a = (h * _sigmoid(h) * u).astype(BF16)
        acc_ref[...] += jnp.dot(a, wd, preferred_element_type=F32)

    @pl.when(j < nfull)
    def _():
        accumulate(wg_ref[...], wu_ref[...], wd_ref[...])

    @pl.when(j == nfull)
    def _():
        accumulate(wgt_ref[...], wut_ref[...], wdt_ref[...])
        y = x_ref[...] + 0.5 * acc_ref[...]
        if emit_resid:
            acc_ref[...] = y
        hn_ref[...] = _rms(y, gn_ref[...]).astype(hn_ref.dtype)


def _ffn(x, g, wg, wu, wd, gn, *, emit_resid, tm, tf):
    m, d = x.shape
    f = wg.shape[1]
    nfull = (f - 1) // tf
    tail = f - nfull * tf
    assert nfull >= 1 and tail % LANES == 0
    wgt, wut, wdt = wg[:, nfull * tf:], wu[:, nfull * tf:], wd[nfull * tf:, :]
    last = nfull - 1
    fixed = pl.Buffered(1)
    row = pl.BlockSpec((tm, d), lambda i, j: (i, 0))
    vec = pl.BlockSpec((1, d), lambda i, j: (0, 0))
    out_shape = [jax.ShapeDtypeStruct((m, d), F32)]
    out_specs = [row]
    if emit_resid:
        out_shape.append(jax.ShapeDtypeStruct((m, d), BF16))
        out_specs.append(row)
    return pl.pallas_call(
        functools.partial(_ffn_kernel, emit_resid=emit_resid, nfull=nfull),
        grid=(m // tm, nfull + 1),
        in_specs=[row, vec,
                  pl.BlockSpec((d, tf), lambda i, j: (0, jnp.minimum(j, last))),
                  pl.BlockSpec((d, tf), lambda i, j: (0, jnp.minimum(j, last))),
                  pl.BlockSpec((tf, d), lambda i, j: (jnp.minimum(j, last), 0)),
                  pl.BlockSpec((d, tail), lambda i, j: (0, 0), pipeline_mode=fixed),
                  pl.BlockSpec((d, tail), lambda i, j: (0, 0), pipeline_mode=fixed),
                  pl.BlockSpec((tail, d), lambda i, j: (0, 0), pipeline_mode=fixed),
                  vec],
        out_specs=out_specs,
        out_shape=out_shape,
        scratch_shapes=[pltpu.VMEM((tm, d), BF16)],
        compiler_params=_params(("parallel", "arbitrary")),
        name="ffn_resid" if emit_resid else "ffn_final",
    )(x, g, wg, wu, wd, wgt, wut, wdt, gn)


def _in_proj_kernel(x_ref, wt_ref, o_ref, wb_sc, *, lane_blocks):
    @pl.when(pl.program_id(1) == 0)
    def _():
        wb_sc[...] = wt_ref[...].T.astype(BF16)

    res = jnp.dot(x_ref[...], wb_sc[...], preferred_element_type=F32)
    if lane_blocks:
        for c in range(res.shape[1] // LANES):
            o_ref[c] = res[:, c * LANES:(c + 1) * LANES].astype(o_ref.dtype)
    else:
        o_ref[...] = res.astype(o_ref.dtype)


def _in_proj(x, wt, row0, n, out_dtype, *, tm, tn, name, lane_blocks=False):
    m, k = x.shape
    assert row0 % SUBLANES == 0 and n % tn == 0
    if lane_blocks:
        out_shape = jax.ShapeDtypeStruct((n // LANES, m, LANES), out_dtype)
        out_spec = pl.BlockSpec((tn // LANES, tm, LANES), lambda j, i: (j, i, 0))
    else:
        out_shape = jax.ShapeDtypeStruct((m, n), out_dtype)
        out_spec = pl.BlockSpec((tm, tn), lambda j, i: (i, j))
    return pl.pallas_call(
        functools.partial(_in_proj_kernel, lane_blocks=lane_blocks),
        grid=(n // tn, m // tm),
        in_specs=[pl.BlockSpec((tm, k), lambda j, i: (i, 0)),
                  pl.BlockSpec((pl.Element(tn), pl.Element(k)),
                               lambda j, i: (pl.multiple_of(row0 + j * tn, SUBLANES), 0))],
        out_specs=out_spec,
        out_shape=out_shape,
        scratch_shapes=[pltpu.VMEM((k, tn), BF16)],
        compiler_params=_params(("parallel", "arbitrary")),
        name=name,
    )(x, wt)


def _mm_res_kernel(x_ref, w_ref, r_ref, o_ref):
    o_ref[...] = r_ref[...] + jnp.dot(x_ref[...], w_ref[...], preferred_element_type=F32)


def _matmul_residual(x, w, r, *, tm, tn):
    m, k = x.shape
    n = w.shape[1]
    return pl.pallas_call(
        _mm_res_kernel,
        grid=(n // tn, m // tm),
        in_specs=[pl.BlockSpec((tm, k), lambda j, i: (i, 0)),
                  pl.BlockSpec((k, tn), lambda j, i: (0, j)),
                  pl.BlockSpec((tm, tn), lambda j, i: (i, j))],
        out_specs=pl.BlockSpec((tm, tn), lambda j, i: (i, j)),
        out_shape=jax.ShapeDtypeStruct((m, n), F32),
        compiler_params=_params(("parallel", "parallel")),
        name="out_proj",
    )(x, w, r)


def _merge_kernel(yr_ref, ya_ref, wr_ref, wa_ref, gr_ref, ga_ref, o_ref):
    pr = jnp.dot(yr_ref[...], wr_ref[...], preferred_element_type=F32)
    pa = jnp.dot(ya_ref[...], wa_ref[...], preferred_element_type=F32)
    o_ref[...] = (_sigmoid(gr_ref[...]) * pr + _sigmoid(ga_ref[...]) * pa).astype(o_ref.dtype)


def _merge(y_rnn, y_attn, w_r, w_a, pf, gr_col, ga_col, *, tm, tn):
    m, k = y_rnn.shape
    n = w_r.shape[1]
    act = pl.BlockSpec((tm, k), lambda j, i: (i, 0))
    wsp = pl.BlockSpec((k, tn), lambda j, i: (0, j))
    return pl.pallas_call(
        _merge_kernel,
        grid=(n // tn, m // tm),
        in_specs=[act, act, wsp, wsp,
                  pl.BlockSpec((tm, tn), lambda j, i: (i, gr_col // tn + j)),
                  pl.BlockSpec((tm, tn), lambda j, i: (i, ga_col // tn + j))],
        out_specs=pl.BlockSpec((tm, tn), lambda j, i: (i, j)),
        out_shape=jax.ShapeDtypeStruct((m, n), BF16),
        compiler_params=_params(("parallel", "parallel")),
        name="merge",
    )(y_rnn, y_attn, w_r, w_a, pf, pf)


def _rglru_kernel(rx_ref, rg_ref, cw_ref, cb_ref, wax_ref, ba_ref, bx_ref, lam_ref,
                  y_ref, xs_sc, h_sc):
    ts, tc = rx_ref.shape
    ng = ts // SUBLANES

    @pl.when(pl.program_id(2) == 0)
    def _():
        xs_sc[...] = jnp.zeros_like(xs_sc)
        h_sc[...] = jnp.zeros_like(h_sc)

    ri = lax.broadcasted_iota(I32, (1, SUBLANES, tc), 1)
    x = rx_ref[...]
    x3 = x.reshape(ng, SUBLANES, tc)
    tail = xs_sc[...]
    xs_sc[...] = x[ts - SUBLANES:ts, :]
    cw = cw_ref[...]
    xc3 = cb_ref[...][None] + x3 * cw[CONV_WIDTH - 1:CONV_WIDTH, :][None]
    for d in range(1, CONV_WIDTH):
        rot = pltpu.roll(x3, d, axis=1)
        rot_tail = pltpu.roll(tail, d, axis=0)[None]
        rot_prev = jnp.concatenate([rot_tail, rot[:ng - 1]], axis=0)
        k = CONV_WIDTH - 1 - d
        xc3 = xc3 + jnp.where(ri >= d, rot, rot_prev) * cw[k:k + 1, :][None]
    xc = xc3.reshape(ts, tc)

    xcb = xc.astype(BF16)
    r_parts, i_parts = [], []
    for n in range(tc // RNN_BLOCK):
        g = jnp.dot(xcb[:, n * RNN_BLOCK:(n + 1) * RNN_BLOCK], wax_ref[n],
                    preferred_element_type=F32)
        r_parts.append(g[:, :RNN_BLOCK])
        i_parts.append(g[:, RNN_BLOCK:])
    tr = jnp.tanh(jnp.concatenate(r_parts, axis=1) + ba_ref[...])
    ti = jnp.tanh(jnp.concatenate(i_parts, axis=1) + bx_ref[...])

    nl = -lam_ref[...]
    softplus = jnp.maximum(nl, 0.0) + jnp.log1p(jnp.exp(-jnp.abs(nl)))
    log_a = (tr + 1.0) * ((-0.5 * RG_C) * softplus)
    a = jnp.exp(log_a)
    mult = jnp.sqrt(-jnp.tanh(log_a) * (a * a + 1.0))
    u = mult * ((0.5 * ti + 0.5) * xc)

    av = a.reshape(ng, SUBLANES, tc)
    bv = u.reshape(ng, SUBLANES, tc)
    for d in (1, 2, 4):
        a_prev = jnp.where(ri >= d, pltpu.roll(av, d, axis=1), 1.0)
        b_prev = jnp.where(ri >= d, pltpu.roll(bv, d, axis=1), 0.0)
        bv = av * b_prev + bv
        av = av * a_prev
    carry = h_sc[0:1, :]
    hs = []
    for k in range(ng):
        hk = bv[k] + av[k] * carry
        hs.append(hk)
        carry = hk[SUBLANES - 1:SUBLANES, :]
    h_sc[0:1, :] = carry
    h = jnp.concatenate(hs, axis=0)

    gt = rg_ref[...]
    c1 = math.sqrt(2.0 / math.pi)
    inner = gt * (c1 + (c1 * 0.044715) * (gt * gt))
    y_ref[...] = ((h * (0.5 * gt)) * (1.0 + jnp.tanh(inner))).astype(y_ref.dtype)


def _rglru(pf, rx_col, rg_col, conv_w, conv_b, wax, b_a, b_x, lam, *, width, ts, tc):
    bsz, s, _ = pf.shape
    chan = lambda rows: pl.BlockSpec((rows, tc), lambda b, c, t: (0, c))
    return pl.pallas_call(
        _rglru_kernel,
        grid=(bsz, width // tc, s // ts),
        in_specs=[pl.BlockSpec((None, ts, tc), lambda b, c, t: (b, t, rx_col // tc + c)),
                  pl.BlockSpec((None, ts, tc), lambda b, c, t: (b, t, rg_col // tc + c)),
                  chan(CONV_WIDTH), chan(1),
                  pl.BlockSpec((tc // RNN_BLOCK, RNN_BLOCK, 2 * RNN_BLOCK),
                               lambda b, c, t: (c, 0, 0)),
                  chan(1), chan(1), chan(1)],
        out_specs=pl.BlockSpec((None, ts, tc), lambda b, c, t: (b, t, c)),
        out_shape=jax.ShapeDtypeStruct((bsz, s, width), BF16),
        scratch_shapes=[pltpu.VMEM((SUBLANES, tc), F32),
                        pltpu.VMEM((SUBLANES, tc), F32)],
        compiler_params=_params(("parallel", "parallel", "arbitrary")),
        name="rglru",
    )(pf, pf, conv_w, conv_b, wax, b_a, b_x, lam)


def _bias_kernel(rb_ref, o_ref):
    h = pl.program_id(0)
    far = rb_ref[NUM_BUCKETS - 1, h]
    ii = lax.broadcasted_iota(I32, (KEY_BLOCK, KEY_BLOCK), 0)
    jj = lax.broadcasted_iota(I32, (KEY_BLOCK, KEY_BLOCK), 1)
    max_exact = NUM_BUCKETS // 2
    for d in range(2):
        n = jnp.maximum(ii - jj + KEY_BLOCK * d, 0)
        nf = jnp.maximum(n, 1).astype(F32)
        large = max_exact + (jnp.log(nf / max_exact) / math.log(MAX_DISTANCE / max_exact)
                             * (NUM_BUCKETS - max_exact)).astype(I32)
        large = jnp.minimum(large, NUM_BUCKETS - 1)
        bucket = jnp.where(n < max_exact, n, large)
        acc = jnp.zeros((KEY_BLOCK, KEY_BLOCK), F32)
        for b in range(NUM_BUCKETS):
            acc = jnp.where(bucket == b, rb_ref[b, h], acc)
        o_ref[0, d] = (acc - far) * LOG2E
    o_ref[0, 2] = jnp.zeros((KEY_BLOCK, KEY_BLOCK), F32)


def _bias_tiles(rel_bias):
    heads = rel_bias.shape[1]
    return pl.pallas_call(
        _bias_kernel,
        grid=(heads,),
        in_specs=[pl.BlockSpec(memory_space=pltpu.SMEM)],
        out_specs=pl.BlockSpec((1, 3, KEY_BLOCK, KEY_BLOCK), lambda h: (h, 0, 0, 0)),
        out_shape=jax.ShapeDtypeStruct((heads, 3, KEY_BLOCK, KEY_BLOCK), F32),
        compiler_params=_params(("parallel",)),
        name="bias_tiles",
    )(rel_bias)


SCORE_CHUNK = 4


def _attn_body(q_ref, k_ref, v_ref, iq_ref, ik_ref, iw_ref, bt_ref, y_ref,
               keys_sc, keyst_sc, am_sc, wb_sc, pb_sc, va_sc, *, k_top, nkb, class_blocks):
    heads, tq, _ = q_ref.shape
    groups = heads // KV_HEADS
    nk = nkb * KEY_BLOCK
    qi = pl.program_id(1)
    row = qi * tq + lax.broadcasted_iota(I32, (tq, KEY_BLOCK), 0)
    lane = lax.broadcasted_iota(I32, (tq, KEY_BLOCK), 1)

    w_scale = (IDX_HEADS ** -0.5) * (IDX_DIM ** -0.5)
    iw = iw_ref[...]
    for h in range(IDX_HEADS):
        col = iw[:, IDX_DIM + h:IDX_DIM + h + 1] * w_scale
        wb_sc[h] = jnp.broadcast_to(col, (tq, KEY_BLOCK))
    per_slab = LANES // IDX_DIM
    iq_rows = jnp.concatenate(
        [iq_ref[h // per_slab][:, (h % per_slab) * IDX_DIM:(h % per_slab + 1) * IDX_DIM]
         for h in range(IDX_HEADS)], axis=0)

    for c0 in range(0, nkb, SCORE_CHUNK):
        nb = min(SCORE_CHUNK, nkb - c0)
        ikc = ik_ref[c0 * KEY_BLOCK:(c0 + nb) * KEY_BLOCK, 0:IDX_DIM].astype(BF16)
        dots = lax.dot_general(iq_rows, ikc, (((1,), (1,)), ((), ())),
                               preferred_element_type=F32)
        for sub in range(nb):
            kb = c0 + sub
            acc = jnp.zeros((tq, KEY_BLOCK), F32)
            for h in range(IDX_HEADS):
                d = dots[h * tq:(h + 1) * tq, sub * KEY_BLOCK:(sub + 1) * KEY_BLOCK]
                acc = acc + jnp.maximum(d, 0.0) * wb_sc[h]
            score = jnp.where(kb * KEY_BLOCK + lane <= row, acc, -jnp.inf)
            bits = pltpu.bitcast(score, I32)
            key = bits ^ ((bits >> 31) & 0x7FFFFFFF)
            keys_sc[kb] = key
            keyst_sc[kb] = key.T.reshape(KEY_BLOCK // SUBLANES, SUBLANES, tq)

    def count_where(pred, ref8):
        parts = [jnp.sum(pred(keyst_sc[kb], ref8).astype(I32), axis=0) for kb in range(nkb)]
        return jnp.sum(_tree_sum(parts), axis=0, keepdims=True)

    def bit_step(it, carry):
        thr, cnt = carry
        cand = thr + lax.shift_left(jnp.int32(1), 31 - it)
        total = count_where(jnp.greater_equal, jnp.broadcast_to(cand, (SUBLANES, tq))[None])
        take = total >= k_top
        return jnp.where(take, cand, thr), jnp.where(take, total, cnt)

    thr, cnt = lax.fori_loop(0, 32, bit_step, (jnp.full((1, tq), INT32_MIN, I32),
                                               jnp.full((1, tq), nk, I32)))
    full_row = thr > KEY_NEG_INF
    thr = jnp.maximum(thr, KEY_NEG_INF + 1)
    has_ties = jnp.max(jnp.where(full_row, cnt, 0)) > k_top

    thr_b = jnp.broadcast_to(thr, (KEY_BLOCK, tq)).T
    for kb in range(nkb):
        am_sc[kb] = jnp.where(keys_sc[kb] >= thr_b, 0.0, NEG_LOGIT)

    @pl.when(has_ties)
    def _():
        thr_t = jnp.broadcast_to(thr, (KEY_BLOCK, tq))
        n_eq = count_where(jnp.equal, jnp.broadcast_to(thr, (SUBLANES, tq))[None])
        keep = jnp.where(full_row, k_top - (cnt - n_eq), nk).astype(F32)
        keep_t = jnp.broadcast_to(keep, (KEY_BLOCK, tq))
        ri = lax.broadcasted_iota(I32, (KEY_BLOCK, KEY_BLOCK), 0)
        ci = lax.broadcasted_iota(I32, (KEY_BLOCK, KEY_BLOCK), 1)
        lower = jnp.where(ri >= ci, 1.0, 0.0).astype(BF16)
        seen = jnp.zeros((1, tq), F32)
        for kb in range(nkb):
            kt = keyst_sc[kb].reshape(KEY_BLOCK, tq)
            eq = jnp.where(kt == thr_t, 1.0, 0.0)
            rank = jnp.dot(lower, eq.astype(BF16), preferred_element_type=F32) + seen
            seen = rank[KEY_BLOCK - 1:KEY_BLOCK, :]
            tie_ok = jnp.where(rank <= keep_t, eq, 0.0)
            sel = jnp.where(kt > thr_t, 1.0, tie_ok)
            am_sc[kb] = jnp.where(sel.T > 0.5, 0.0, NEG_LOGIT)

    first_near = max(nkb - class_blocks - 1, 0)

    def logit(s, g, j, kb):
        sj = (s[j * tq:(j + 1) * tq, kb * KEY_BLOCK:(kb + 1) * KEY_BLOCK] + am_sc[kb])
        if kb >= first_near:
            sj = sj + bt_ref[g * groups + j, jnp.clip(qi - kb, 0, 2)]
        return sj

    scale2 = (HEAD_DIM ** -0.5) * LOG2E

    def raw_logits(g):
        qg = q_ref[g * groups:(g + 1) * groups].reshape(groups * tq, HEAD_DIM)
        return lax.dot_general(qg, k_ref[g, 0:nk, :], (((1,), (1,)), ((), ())),
                               preferred_element_type=F32) * scale2

    s_next = raw_logits(0)
    for g in range(KV_HEADS):
        s = s_next
        if g + 1 < KV_HEADS:
            s_next = raw_logits(g + 1)
        for j in range(groups):
            m_run = logit(s, g, j, 0)
            for kb in range(1, nkb):
                m_run = jnp.maximum(m_run, logit(s, g, j, kb))
            m_b = jnp.broadcast_to(jnp.max(m_run, axis=1, keepdims=True), (tq, KEY_BLOCK))
            for kb in range(nkb):
                pb_sc[g, j * tq:(j + 1) * tq, kb * KEY_BLOCK:(kb + 1) * KEY_BLOCK] = (
                    jnp.exp2(logit(s, g, j, kb) - m_b).astype(BF16))
        o = jnp.dot(pb_sc[g, :, 0:nk], va_sc[g, 0:nk, :], preferred_element_type=F32)
        out = o[:, 0:HEAD_DIM] / o[:, HEAD_DIM:2 * HEAD_DIM]
        for j in range(groups):
            hd = (g * groups + j) * HEAD_DIM
            y_ref[:, hd:hd + HEAD_DIM] = out[j * tq:(j + 1) * tq].astype(y_ref.dtype)


def _attn_kernel(*refs, k_top, class_blocks, nq):
    qi = pl.program_id(1)
    v_ref, va_sc = refs[2], refs[-1]

    @pl.when(qi == 0)
    def _():
        va_sc[:, :, 0:HEAD_DIM] = v_ref[...]
        va_sc[:, :, HEAD_DIM:2 * HEAD_DIM] = jnp.ones(v_ref.shape, v_ref.dtype)

    for c in range(-(-nq // class_blocks)):
        nkb = min((c + 1) * class_blocks, nq)

        @pl.when(qi // class_blocks == c)
        def _(nkb=nkb):
            _attn_body(*refs, k_top=k_top, nkb=nkb, class_blocks=class_blocks)


def _attention(pb, ps, btab, *, bsz, heads, tq, class_blocks):
    s = ps.shape[1]
    nq = s // tq
    iq_slabs = IDX_HEADS * IDX_DIM // LANES
    assert HEAD_DIM == LANES and heads % KV_HEADS == 0 and heads % iq_slabs == 0
    assert pb.shape[0] == heads + 2 * KV_HEADS + iq_slabs
    nkb_max = s // KEY_BLOCK
    groups = heads // KV_HEADS
    k_top = min(TOPK_MAX, s // 4)
    assert tq == KEY_BLOCK and nq % class_blocks == 0
    return pl.pallas_call(
        functools.partial(_attn_kernel, k_top=k_top, class_blocks=class_blocks, nq=nq),
        grid=(bsz, nq),
        in_specs=[pl.BlockSpec((heads, tq, LANES), lambda b, i: (0, b * nq + i, 0)),
                  pl.BlockSpec((KV_HEADS, s, LANES), lambda b, i: (heads // KV_HEADS, b, 0)),
                  pl.BlockSpec((KV_HEADS, s, LANES), lambda b, i: (heads // KV_HEADS + 1, b, 0)),
                  pl.BlockSpec((iq_slabs, tq, LANES),
                               lambda b, i: ((heads + 2 * KV_HEADS) // iq_slabs, b * nq + i, 0)),
                  pl.BlockSpec((None, s, LANES), lambda b, i: (b, 0, 0)),
                  pl.BlockSpec((None, tq, LANES), lambda b, i: (b, i, 0)),
                  pl.BlockSpec(btab.shape, lambda b, i: (0, 0, 0, 0))],
        out_specs=pl.BlockSpec((tq, heads * HEAD_DIM), lambda b, i: (b * nq + i, 0)),
        out_shape=jax.ShapeDtypeStruct((bsz * s, heads * HEAD_DIM), BF16),
        scratch_shapes=[
            pltpu.VMEM((nkb_max, tq, KEY_BLOCK), I32),
            pltpu.VMEM((nkb_max, KEY_BLOCK // SUBLANES, SUBLANES, tq), I32),
            pltpu.VMEM((nkb_max, tq, KEY_BLOCK), F32),
            pltpu.VMEM((IDX_HEADS, tq, KEY_BLOCK), F32),
            pltpu.VMEM((KV_HEADS, groups * tq, s), BF16),
            pltpu.VMEM((KV_HEADS, s, 2 * HEAD_DIM), BF16)],
        compiler_params=_params(("parallel", "arbitrary")),
        name="sparse_attn",
    )(pb, pb, pb, pb, ps, ps, btab)


FFN_TM, FFN_TF = 512, 512
MM_TM, MM_TN = 1024, 1024
RGLRU_TS, RGLRU_TC = 512, 512
ATTN_CLASS_BLOCKS = 4


def _layer(x, ffn1_norm, ffn1_w_gate, ffn1_w_up, ffn1_w_down, mix_norm, w_in, conv_w, conv_b,
           rg_w_a, rg_b_a, rg_w_x, rg_b_x, rg_lambda, btab, w_proj_rnn, w_proj_attn, w_out,
           ffn2_norm, ffn2_w_gate, ffn2_w_up, ffn2_w_down, next_norm, *, last):
    bsz, s, d = x.shape
    m = bsz * s
    width = d
    q_width = d
    kv_width = KV_HEADS * HEAD_DIM
    iq_width = IDX_HEADS * IDX_DIM
    tm = min(FFN_TM, m)
    tmm = min(MM_TM, m)

    o_q = 2 * width
    o_ik = o_q + q_width + 2 * kv_width + iq_width
    o_gr = o_ik + IDX_DIM + IDX_HEADS

    bf = lambda w: w.astype(BF16)
    vec = lambda p: p.reshape(1, -1)

    x1, hn = _ffn(x.reshape(m, d), vec(ffn1_norm), bf(ffn1_w_gate), bf(ffn1_w_up), bf(ffn1_w_down),
                  vec(mix_norm), emit_resid=True, tm=tm, tf=FFN_TF)

    w_t = w_in.T
    pf = _in_proj(hn, w_t, 0, o_q, F32, tm=tmm, tn=MM_TN, name="in_proj_rnn")
    pb = _in_proj(hn, w_t, o_q, o_ik - o_q, BF16, tm=tmm, tn=MM_TN, name="in_proj_attn",
                  lane_blocks=True)
    ps = _in_proj(hn, w_t, o_ik, LANES, F32, tm=tmm, tn=LANES, name="in_proj_idx")
    pg = _in_proj(hn, w_t, o_gr, 2 * d, F32, tm=tmm, tn=MM_TN, name="in_proj_gates")

    wax = bf(0.5 * jnp.concatenate([rg_w_a, rg_w_x], axis=-1))
    y_rnn = _rglru(pf.reshape(bsz, s, -1), 0, width, conv_w, vec(conv_b), wax,
                   vec(0.5 * rg_b_a), vec(0.5 * rg_b_x), vec(rg_lambda),
                   width=width, ts=min(RGLRU_TS, s), tc=RGLRU_TC)

    y_attn = _attention(pb, ps.reshape(bsz, s, LANES), btab, bsz=bsz, heads=q_width // HEAD_DIM,
                        tq=KEY_BLOCK, class_blocks=ATTN_CLASS_BLOCKS)

    merged = _merge(y_rnn.reshape(m, width), y_attn,
                    bf(w_proj_rnn), bf(w_proj_attn), pg, 0, d, tm=tm, tn=MM_TN)
    x2 = _matmul_residual(merged, bf(w_out), x1, tm=tmm, tn=MM_TN)

    outs = _ffn(x2, vec(ffn2_norm), bf(ffn2_w_gate), bf(ffn2_w_up), bf(ffn2_w_down),
                vec(next_norm), emit_resid=not last, tm=tm, tf=FFN_TF)
    return outs[0].reshape(bsz, s, d)


def kernel(x, ffn1_norm, ffn1_w_gate, ffn1_w_up, ffn1_w_down, mix_norm, w_in, conv_w, conv_b,
           rg_w_a, rg_b_a, rg_w_x, rg_b_x, rg_lambda, rel_bias, w_proj_rnn, w_proj_attn, w_out,
           ffn2_norm, ffn2_w_gate, ffn2_w_up, ffn2_w_down, final_norm):
    depth = ffn1_norm.shape[0]
    btab = _bias_tiles(rel_bias)
    for l in range(depth):
        last = l == depth - 1
        x = _layer(x, ffn1_norm[l], ffn1_w_gate[l], ffn1_w_up[l], ffn1_w_down[l], mix_norm[l],
                   w_in[l], conv_w[l], conv_b[l], rg_w_a[l], rg_b_a[l], rg_w_x[l], rg_b_x[l],
                   rg_lambda[l], btab, w_proj_rnn[l], w_proj_attn[l], w_out[l], ffn2_norm[l],
                   ffn2_w_gate[l], ffn2_w_up[l], ffn2_w_down[l],
                   final_norm if last else ffn1_norm[l + 1], last=last)
    return x
```

```python
import functools
import math

import jax
import jax.numpy as jnp
from jax import lax
from jax.experimental import pallas as pl
from jax.experimental.pallas import tpu as pltpu

F32 = jnp.float32
BF16 = jnp.bfloat16
I32 = jnp.int32

RMS_EPS = 1e-6
CONV_WIDTH = 4
RG_C = 8.0
RNN_BLOCK = 128
HEAD_DIM = 128
KV_HEADS = 4
IDX_HEADS = 16
IDX_DIM = 64
TOPK_MAX = 256
NUM_BUCKETS = 32
MAX_DISTANCE = 128

LANES = 128
SUBLANES = 8
VMEM_LIMIT_BYTES = 58 * 1024 * 1024

KEY_BLOCK = 128
NEG_LOGIT = -1e30
INT32_MIN = -(2 ** 31)
KEY_NEG_INF = -2139095041
LOG2E = 1.4426950408889634


def _params(semantics):
    return pltpu.CompilerParams(dimension_semantics=semantics,
                                vmem_limit_bytes=VMEM_LIMIT_BYTES)


def _sigmoid(x):
    return 0.5 * (jnp.tanh(0.5 * x) + 1.0)


def _rms(x, g):
    ms = jnp.mean(x * x, axis=-1, keepdims=True)
    return x * lax.rsqrt(ms + RMS_EPS) * g


def _tree_sum(parts):
    while len(parts) > 1:
        parts = [a + b for a, b in zip(parts[::2], parts[1::2])] + (
            [parts[-1]] if len(parts) % 2 else [])
    return parts[0]


def _ffn_kernel(x_ref, g_ref, wg_ref, wu_ref, wd_ref, wgt_ref, wut_ref, wdt_ref, gn_ref, *rest,
                emit_resid, nfull):
    if emit_resid:
        acc_ref, hn_ref, xn_sc = rest
    else:
        acc_ref, xn_sc = rest
        hn_ref = acc_ref
    j = pl.program_id(1)

    @pl.when(j == 0)
    def _():
        xn_sc[...] = _rms(x_ref[...], g_ref[...]).astype(BF16)
        acc_ref[...] = jnp.zeros_like(acc_ref)

    def accumulate(wg, wu, wd):
        xn = xn_sc[...]
        h = jnp.dot(xn, wg, preferred_element_type=F32)
        u = jnp.dot(xn, wu, preferred_element_type=F32)
        a = (h * _sigmoid(h) * u).astype(BF16)
        acc_ref[...] += jnp.dot(a, wd, preferred_element_type=F32)

    @pl.when(j < nfull)
    def _():
        accumulate(wg_ref[...], wu_ref[...], wd_ref[...])

    @pl.when(j == nfull)
    def _():
        accumulate(wgt_ref[...], wut_ref[...], wdt_ref[...])
        y = x_ref[...] + 0.5 * acc_ref[...]
        if emit_resid:
            acc_ref[...] = y
        hn_ref[...] = _rms(y, gn_ref[...]).astype(hn_ref.dtype)


def _ffn(x, g, wg, wu, wd, gn, *, emit_resid, tm, tf):
    m, d = x.shape
    f = wg.shape[1]
    nfull = (f - 1) // tf
    tail = f - nfull * tf
    assert nfull >= 1 and tail % LANES == 0
    wgt, wut, wdt = wg[:, nfull * tf:], wu[:, nfull * tf:], wd[nfull * tf:, :]
    last = nfull - 1
    fixed = pl.Buffered(1)
    row = pl.BlockSpec((tm, d), lambda i, j: (i, 0))
    vec = pl.BlockSpec((1, d), lambda i, j: (0, 0))
    out_shape = [jax.ShapeDtypeStruct((m, d), F32)]
    out_specs = [row]
    if emit_resid:
        out_shape.append(jax.ShapeDtypeStruct((m, d), BF16))
        out_specs.append(row)
    return pl.pallas_call(
        functools.partial(_ffn_kernel, emit_resid=emit_resid, nfull=nfull),
        grid=(m // tm, nfull + 1),
        in_specs=[row, vec,
                  pl.BlockSpec((d, tf), lambda i, j: (0, jnp.minimum(j, last))),
                  pl.BlockSpec((d, tf), lambda i, j: (0, jnp.minimum(j, last))),
                  pl.BlockSpec((tf, d), lambda i, j: (jnp.minimum(j, last), 0)),
                  pl.BlockSpec((d, tail), lambda i, j: (0, 0), pipeline_mode=fixed),
                  pl.BlockSpec((d, tail), lambda i, j: (0, 0), pipeline_mode=fixed),
                  pl.BlockSpec((tail, d), lambda i, j: (0, 0), pipeline_mode=fixed),
                  vec],
        out_specs=out_specs,
        out_shape=out_shape,
        scratch_shapes=[pltpu.VMEM((tm, d), BF16)],
        compiler_params=_params(("parallel", "arbitrary")),
        name="ffn_resid" if emit_resid else "ffn_final",
    )(x, g, wg, wu, wd, wgt, wut, wdt, gn)


def _in_proj_kernel(x_ref, wt_ref, o_ref, wb_sc, *, lane_blocks):
    @pl.when(pl.program_id(1) == 0)
    def _():
        wb_sc[...] = wt_ref[...].T.astype(BF16)

    res = jnp.dot(x_ref[...], wb_sc[...], preferred_element_type=F32)
    if lane_blocks:
        for c in range(res.shape[1] // LANES):
            o_ref[c] = res[:, c * LANES:(c + 1) * LANES].astype(o_ref.dtype)
    else:
        o_ref[...] = res.astype(o_ref.dtype)


def _in_proj(x, wt, row0, n, out_dtype, *, tm, tn, name, lane_blocks=False):
    m, k = x.shape
    assert row0 % SUBLANES == 0 and n % tn == 0
    if lane_blocks:
        out_shape = jax.ShapeDtypeStruct((n // LANES, m, LANES), out_dtype)
        out_spec = pl.BlockSpec((tn // LANES, tm, LANES), lambda j, i: (j, i, 0))
    else:
        out_shape = jax.ShapeDtypeStruct((m, n), out_dtype)
        out_spec = pl.BlockSpec((tm, tn), lambda j, i: (i, j))
    return pl.pallas_call(
        functools.partial(_in_proj_kernel, lane_blocks=lane_blocks),
        grid=(n // tn, m // tm),
        in_specs=[pl.BlockSpec((tm, k), lambda j, i: (i, 0)),
                  pl.BlockSpec((pl.Element(tn), pl.Element(k)),
                               lambda j, i: (pl.multiple_of(row0 + j * tn, SUBLANES), 0))],
        out_specs=out_spec,
        out_shape=out_shape,
        scratch_shapes=[pltpu.VMEM((k, tn), BF16)],
        compiler_params=_params(("parallel", "arbitrary")),
        name=name,
    )(x, wt)


def _mm_res_kernel(x_ref, w_ref, r_ref, o_ref):
    o_ref[...] = r_ref[...] + jnp.dot(x_ref[...], w_ref[...], preferred_element_type=F32)


def _matmul_residual(x, w, r, *, tm, tn):
    m, k = x.shape
    n = w.shape[1]
    return pl.pallas_call(
        _mm_res_kernel,
        grid=(n // tn, m // tm),
        in_specs=[pl.BlockSpec((tm, k), lambda j, i: (i, 0)),
                  pl.BlockSpec((k, tn), lambda j, i: (0, j)),
                  pl.BlockSpec((tm, tn), lambda j, i: (i, j))],
        out_specs=pl.BlockSpec((tm, tn), lambda j, i: (i, j)),
        out_shape=jax.ShapeDtypeStruct((m, n), F32),
        compiler_params=_params(("parallel", "parallel")),
        name="out_proj",
    )(x, w, r)


def _merge_kernel(yr_ref, ya_ref, wr_ref, wa_ref, gr_ref, ga_ref, o_ref):
    pr = jnp.dot(yr_ref[...], wr_ref[...], preferred_element_type=F32)
    pa = jnp.dot(ya_ref[...], wa_ref[...], preferred_element_type=F32)
    o_ref[...] = (_sigmoid(gr_ref[...]) * pr + _sigmoid(ga_ref[...]) * pa).astype(o_ref.dtype)


def _merge(y_rnn, y_attn, w_r, w_a, pf, gr_col, ga_col, *, tm, tn):
    m, k = y_rnn.shape
    n = w_r.shape[1]
    act = pl.BlockSpec((tm, k), lambda j, i: (i, 0))
    wsp = pl.BlockSpec((k, tn), lambda j, i: (0, j))
    return pl.pallas_call(
        _merge_kernel,
        grid=(n // tn, m // tm),
        in_specs=[act, act, wsp, wsp,
                  pl.BlockSpec((tm, tn), lambda j, i: (i, gr_col // tn + j)),
                  pl.BlockSpec((tm, tn), lambda j, i: (i, ga_col // tn + j))],
        out_specs=pl.BlockSpec((tm, tn), lambda j, i: (i, j)),
        out_shape=jax.ShapeDtypeStruct((m, n), BF16),
        compiler_params=_params(("parallel", "parallel")),
        name="merge",
    )(y_rnn, y_attn, w_r, w_a, pf, pf)


def _rglru_kernel(rx_ref, rg_ref, cw_ref, cb_ref, wax_ref, ba_ref, bx_ref, lam_ref,
                  y_ref, xs_sc, h_sc):
    ts, tc = rx_ref.shape
    ng = ts // SUBLANES

    @pl.when(pl.program_id(2) == 0)
    def _():
        xs_sc[...] = jnp.zeros_like(xs_sc)
        h_sc[...] = jnp.zeros_like(h_sc)

    ri = lax.broadcasted_iota(I32, (1, SUBLANES, tc), 1)
    x = rx_ref[...]
    x3 = x.reshape(ng, SUBLANES, tc)
    tail = xs_sc[...]
    xs_sc[...] = x[ts - SUBLANES:ts, :]
    cw = cw_ref[...]
    xc3 = cb_ref[...][None] + x3 * cw[CONV_WIDTH - 1:CONV_WIDTH, :][None]
    for d in range(1, CONV_WIDTH):
        rot = pltpu.roll(x3, d, axis=1)
        rot_tail = pltpu.roll(tail, d, axis=0)[None]
        rot_prev = jnp.concatenate([rot_tail, rot[:ng - 1]], axis=0)
        k = CONV_WIDTH - 1 - d
        xc3 = xc3 + jnp.where(ri >= d, rot, rot_prev) * cw[k:k + 1, :][None]
    xc = xc3.reshape(ts, tc)

    xcb = xc.astype(BF16)
    r_parts, i_parts = [], []
    for n in range(tc // RNN_BLOCK):
        g = jnp.dot(xcb[:, n * RNN_BLOCK:(n + 1) * RNN_BLOCK], wax_ref[n],
                    preferred_element_type=F32)
        r_parts.append(g[:, :RNN_BLOCK])
        i_parts.append(g[:, RNN_BLOCK:])
    tr = jnp.tanh(jnp.concatenate(r_parts, axis=1) + ba_ref[...])
    ti = jnp.tanh(jnp.concatenate(i_parts, axis=1) + bx_ref[...])

    nl = -lam_ref[...]
    softplus = jnp.maximum(nl, 0.0) + jnp.log1p(jnp.exp(-jnp.abs(nl)))
    log_a = (tr + 1.0) * ((-0.5 * RG_C) * softplus)
    a = jnp.exp(log_a)
    mult = jnp.sqrt(-jnp.tanh(log_a) * (a * a + 1.0))
    u = mult * ((0.5 * ti + 0.5) * xc)

    av = a.reshape(ng, SUBLANES, tc)
    bv = u.reshape(ng, SUBLANES, tc)
    for d in (1, 2, 4):
        a_prev = jnp.where(ri >= d, pltpu.roll(av, d, axis=1), 1.0)
        b_prev = jnp.where(ri >= d, pltpu.roll(bv, d, axis=1), 0.0)
        bv = av * b_prev + bv
        av = av * a_prev
    carry = h_sc[0:1, :]
    hs = []
    for k in range(ng):
        hk = bv[k] + av[k] * carry
        hs.append(hk)
        carry = hk[SUBLANES - 1:SUBLANES, :]
    h_sc[0:1, :] = carry
    h = jnp.concatenate(hs, axis=0)

    gt = rg_ref[...]
    c1 = math.sqrt(2.0 / math.pi)
    inner = gt * (c1 + (c1 * 0.044715) * (gt * gt))
    y_ref[...] = ((h * (0.5 * gt)) * (1.0 + jnp.tanh(inner))).astype(y_ref.dtype)


def _rglru(pf, rx_col, rg_col, conv_w, conv_b, wax, b_a, b_x, lam, *, width, ts, tc):
    bsz, s, _ = pf.shape
    chan = lambda rows: pl.BlockSpec((rows, tc), lambda b, c, t: (0, c))
    return pl.pallas_call(
        _rglru_kernel,
        grid=(bsz, width // tc, s // ts),
        in_specs=[pl.BlockSpec((None, ts, tc), lambda b, c, t: (b, t, rx_col // tc + c)),
                  pl.BlockSpec((None, ts, tc), lambda b, c, t: (b, t, rg_col // tc + c)),
                  chan(CONV_WIDTH), chan(1),
                  pl.BlockSpec((tc // RNN_BLOCK, RNN_BLOCK, 2 * RNN_BLOCK),
                               lambda b, c, t: (c, 0, 0)),
                  chan(1), chan(1), chan(1)],
        out_specs=pl.BlockSpec((None, ts, tc), lambda b, c, t: (b, t, c)),
        out_shape=jax.ShapeDtypeStruct((bsz, s, width), BF16),
        scratch_shapes=[pltpu.VMEM((SUBLANES, tc), F32),
                        pltpu.VMEM((SUBLANES, tc), F32)],
        compiler_params=_params(("parallel", "parallel", "arbitrary")),
        name="rglru",
    )(pf, pf, conv_w, conv_b, wax, b_a, b_x, lam)


def _bias_kernel(rb_ref, o_ref):
    h = pl.program_id(0)
    far = rb_ref[NUM_BUCKETS - 1, h]
    ii = lax.broadcasted_iota(I32, (KEY_BLOCK, KEY_BLOCK), 0)
    jj = lax.broadcasted_iota(I32, (KEY_BLOCK, KEY_BLOCK), 1)
    max_exact = NUM_BUCKETS // 2
    for d in range(2):
        n = jnp.maximum(ii - jj + KEY_BLOCK * d, 0)
        nf = jnp.maximum(n, 1).astype(F32)
        large = max_exact + (jnp.log(nf / max_exact) / math.log(MAX_DISTANCE / max_exact)
                             * (NUM_BUCKETS - max_exact)).astype(I32)
        large = jnp.minimum(large, NUM_BUCKETS - 1)
        bucket = jnp.where(n < max_exact, n, large)
        acc = jnp.zeros((KEY_BLOCK, KEY_BLOCK), F32)
        for b in range(NUM_BUCKETS):
            acc = jnp.where(bucket == b, rb_ref[b, h], acc)
        o_ref[0, d] = (acc - far) * LOG2E
    o_ref[0, 2] = jnp.zeros((KEY_BLOCK, KEY_BLOCK), F32)


def _bias_tiles(rel_bias):
    heads = rel_bias.shape[1]
    return pl.pallas_call(
        _bias_kernel,
        grid=(heads,),
        in_specs=[pl.BlockSpec(memory_space=pltpu.SMEM)],
        out_specs=pl.BlockSpec((1, 3, KEY_BLOCK, KEY_BLOCK), lambda h: (h, 0, 0, 0)),
        out_shape=jax.ShapeDtypeStruct((heads, 3, KEY_BLOCK, KEY_BLOCK), F32),
        compiler_params=_params(("parallel",)),
        name="bias_tiles",
    )(rel_bias)


SCORE_CHUNK = 4


def _attn_body(q_ref, k_ref, v_ref, iq_ref, ik_ref, iw_ref, bt_ref, y_ref,
               keys_sc, keyst_sc, am_sc, wb_sc, pb_sc, va_sc, *, k_top, nkb, class_blocks):
    heads, tq, _ = q_ref.shape
    groups = heads // KV_HEADS
    nk = nkb * KEY_BLOCK
    qi = pl.program_id(1)
    row = qi * tq + lax.broadcasted_iota(I32, (tq, KEY_BLOCK), 0)
    lane = lax.broadcasted_iota(I32, (tq, KEY_BLOCK), 1)

    w_scale = (IDX_HEADS ** -0.5) * (IDX_DIM ** -0.5)
    iw = iw_ref[...]
    for h in range(IDX_HEADS):
        col = iw[:, IDX_DIM + h:IDX_DIM + h + 1] * w_scale
        wb_sc[h] = jnp.broadcast_to(col, (tq, KEY_BLOCK))
    per_slab = LANES // IDX_DIM
    iq_rows = jnp.concatenate(
        [iq_ref[h // per_slab][:, (h % per_slab) * IDX_DIM:(h % per_slab + 1) * IDX_DIM]
         for h in range(IDX_HEADS)], axis=0)

    for c0 in range(0, nkb, SCORE_CHUNK):
        nb = min(SCORE_CHUNK, nkb - c0)
        ikc = ik_ref[c0 * KEY_BLOCK:(c0 + nb) * KEY_BLOCK, 0:IDX_DIM].astype(BF16)
        dots = lax.dot_general(iq_rows, ikc, (((1,), (1,)), ((), ())),
                               preferred_element_type=F32)
        for sub in range(nb):
            kb = c0 + sub
            acc = jnp.zeros((tq, KEY_BLOCK), F32)
            for h in range(IDX_HEADS):
                d = dots[h * tq:(h + 1) * tq, sub * KEY_BLOCK:(sub + 1) * KEY_BLOCK]
                acc = acc + jnp.maximum(d, 0.0) * wb_sc[h]
            score = jnp.where(kb * KEY_BLOCK + lane <= row, acc, -jnp.inf)
            bits = pltpu.bitcast(score, I32)
            key = bits ^ ((bits >> 31) & 0x7FFFFFFF)
            keys_sc[kb] = key
            keyst_sc[kb] = key.T.reshape(KEY_BLOCK // SUBLANES, SUBLANES, tq)

    def count_where(pred, ref8):
        parts = [jnp.sum(pred(keyst_sc[kb], ref8).astype(I32), axis=0) for kb in range(nkb)]
        return jnp.sum(_tree_sum(parts), axis=0, keepdims=True)

    def bit_step(it, carry):
        thr, cnt = carry
        cand = thr + lax.shift_left(jnp.int32(1), 31 - it)
        total = count_where(jnp.greater_equal, jnp.broadcast_to(cand, (SUBLANES, tq))[None])
        take = total >= k_top
        return jnp.where(take, cand, thr), jnp.where(take, total, cnt)

    thr, cnt = lax.fori_loop(0, 32, bit_step, (jnp.full((1, tq), INT32_MIN, I32),
                                               jnp.full((1, tq), nk, I32)))
    full_row = thr > KEY_NEG_INF
    thr = jnp.maximum(thr, KEY_NEG_INF + 1)
    has_ties = jnp.max(jnp.where(full_row, cnt, 0)) > k_top

    thr_b = jnp.broadcast_to(thr, (KEY_BLOCK, tq)).T
    for kb in range(nkb):
        am_sc[kb] = jnp.where(keys_sc[kb] >= thr_b, 0.0, NEG_LOGIT)

    @pl.when(has_ties)
    def _():
        thr_t = jnp.broadcast_to(thr, (KEY_BLOCK, tq))
        n_eq = count_where(jnp.equal, jnp.broadcast_to(thr, (SUBLANES, tq))[None])
        keep = jnp.where(full_row, k_top - (cnt - n_eq), nk).astype(F32)
        keep_t = jnp.broadcast_to(keep, (KEY_BLOCK, tq))
        ri = lax.broadcasted_iota(I32, (KEY_BLOCK, KEY_BLOCK), 0)
        ci = lax.broadcasted_iota(I32, (KEY_BLOCK, KEY_BLOCK), 1)
        lower = jnp.where(ri >= ci, 1.0, 0.0).astype(BF16)
        seen = jnp.zeros((1, tq), F32)
        for kb in range(nkb):
            kt = keyst_sc[kb].reshape(KEY_BLOCK, tq)
            eq = jnp.where(kt == thr_t, 1.0, 0.0)
            rank = jnp.dot(lower, eq.astype(BF16), preferred_element_type=F32) + seen
            seen = rank[KEY_BLOCK - 1:KEY_BLOCK, :]
            tie_ok = jnp.where(rank <= keep_t, eq, 0.0)
            sel = jnp.where(kt > thr_t, 1.0, tie_ok)
            am_sc[kb] = jnp.where(sel.T > 0.5, 0.0, NEG_LOGIT)

    first_near = max(nkb - class_blocks - 1, 0)

    def logit(s, g, j, kb):
        sj = (s[j * tq:(j + 1) * tq, kb * KEY_BLOCK:(kb + 1) * KEY_BLOCK] + am_sc[kb])
        if kb >= first_near:
            sj = sj + bt_ref[g * groups + j, jnp.clip(qi - kb, 0, 2)]
        return sj

    scale2 = (HEAD_DIM ** -0.5) * LOG2E

    def raw_logits(g):
        qg = q_ref[g * groups:(g + 1) * groups].reshape(groups * tq, HEAD_DIM)
        return lax.dot_general(qg, k_ref[g, 0:nk, :], (((1,), (1,)), ((), ())),
                               preferred_element_type=F32) * scale2

    s_next = raw_logits(0)
    for g in range(KV_HEADS):
        s = s_next
        if g + 1 < KV_HEADS:
            s_next = raw_logits(g + 1)
        for j in range(groups):
            m_run = logit(s, g, j, 0)
            for kb in range(1, nkb):
                m_run = jnp.maximum(m_run, logit(s, g, j, kb))
            m_b = jnp.broadcast_to(jnp.max(m_run, axis=1, keepdims=True), (tq, KEY_BLOCK))
            for kb in range(nkb):
                pb_sc[g, j * tq:(j + 1) * tq, kb * KEY_BLOCK:(kb + 1) * KEY_BLOCK] = (
                    jnp.exp2(logit(s, g, j, kb) - m_b).astype(BF16))
        o = jnp.dot(pb_sc[g, :, 0:nk], va_sc[g, 0:nk, :], preferred_element_type=F32)
        out = o[:, 0:HEAD_DIM] / o[:, HEAD_DIM:2 * HEAD_DIM]
        for j in range(groups):
            hd = (g * groups + j) * HEAD_DIM
            y_ref[:, hd:hd + HEAD_DIM] = out[j * tq:(j + 1) * tq].astype(y_ref.dtype)


def _attn_kernel(*refs, k_top, class_blocks, nq):
    qi = pl.program_id(1)
    v_ref, va_sc = refs[2], refs[-1]

    @pl.when(qi == 0)
    def _():
        va_sc[:, :, 0:HEAD_DIM] = v_ref[...]
        va_sc[:, :, HEAD_DIM:2 * HEAD_DIM] = jnp.ones(v_ref.shape, v_ref.dtype)

    for c in range(-(-nq // class_blocks)):
        nkb = min((c + 1) * class_blocks, nq)

        @pl.when(qi // class_blocks == c)
        def _(nkb=nkb):
            _attn_body(*refs, k_top=k_top, nkb=nkb, class_blocks=class_blocks)


def _attention(pb, ps, btab, *, bsz, heads, tq, class_blocks):
    s = ps.shape[1]
    nq = s // tq
    iq_slabs = IDX_HEADS * IDX_DIM // LANES
    assert HEAD_DIM == LANES and heads % KV_HEADS == 0 and heads % iq_slabs == 0
    assert pb.shape[0] == heads + 2 * KV_HEADS + iq_slabs
    nkb_max = s // KEY_BLOCK
    groups = heads // KV_HEADS
    k_top = min(TOPK_MAX, s // 4)
    assert tq == KEY_BLOCK and nq % class_blocks == 0
    return pl.pallas_call(
        functools.partial(_attn_kernel, k_top=k_top, class_blocks=class_blocks, nq=nq),
        grid=(bsz, nq),
        in_specs=[pl.BlockSpec((heads, tq, LANES), lambda b, i: (0, b * nq + i, 0)),
                  pl.BlockSpec((KV_HEADS, s, LANES), lambda b, i: (heads // KV_HEADS, b, 0)),
                  pl.BlockSpec((KV_HEADS, s, LANES), lambda b, i: (heads // KV_HEADS + 1, b, 0)),
                  pl.BlockSpec((iq_slabs, tq, LANES),
                               lambda b, i: ((heads + 2 * KV_HEADS) // iq_slabs, b * nq + i, 0)),
                  pl.BlockSpec((None, s, LANES), lambda b, i: (b, 0, 0)),
                  pl.BlockSpec((None, tq, LANES), lambda b, i: (b, i, 0)),
                  pl.BlockSpec(btab.shape, lambda b, i: (0, 0, 0, 0))],
        out_specs=pl.BlockSpec((tq, heads * HEAD_DIM), lambda b, i: (b * nq + i, 0)),
        out_shape=jax.ShapeDtypeStruct((bsz * s, heads * HEAD_DIM), BF16),
        scratch_shapes=[
            pltpu.VMEM((nkb_max, tq, KEY_BLOCK), I32),
            pltpu.VMEM((nkb_max, KEY_BLOCK // SUBLANES, SUBLANES, tq), I32),
            pltpu.VMEM((nkb_max, tq, KEY_BLOCK), F32),
            pltpu.VMEM((IDX_HEADS, tq, KEY_BLOCK), F32),
            pltpu.VMEM((KV_HEADS, groups * tq, s), BF16),
            pltpu.VMEM((KV_HEADS, s, 2 * HEAD_DIM), BF16)],
        compiler_params=_params(("parallel", "arbitrary")),
        name="sparse_attn",
    )(pb, pb, pb, pb, ps, ps, btab)


FFN_TM, FFN_TF = 512, 1024
MM_TM, MM_TN = 1024, 1024
RGLRU_TS, RGLRU_TC = 512, 512
ATTN_CLASS_BLOCKS = 4


def _layer(x, ffn1_norm, ffn1_w_gate, ffn1_w_up, ffn1_w_down, mix_norm, w_in, conv_w, conv_b,
           rg_w_a, rg_b_a, rg_w_x, rg_b_x, rg_lambda, btab, w_proj_rnn, w_proj_attn, w_out,
           ffn2_norm, ffn2_w_gate, ffn2_w_up, ffn2_w_down, next_norm, *, last):
    bsz, s, d = x.shape
    m = bsz * s
    width = d
    q_width = d
    kv_width = KV_HEADS * HEAD_DIM
    iq_width = IDX_HEADS * IDX_DIM
    tm = min(FFN_TM, m)
    tmm = min(MM_TM, m)

    o_q = 2 * width
    o_ik = o_q + q_width + 2 * kv_width + iq_width
    o_gr = o_ik + IDX_DIM + IDX_HEADS

    bf = lambda w: w.astype(BF16)
    vec = lambda p: p.reshape(1, -1)

    x1, hn = _ffn(x.reshape(m, d), vec(ffn1_norm), bf(ffn1_w_gate), bf(ffn1_w_up), bf(ffn1_w_down),
                  vec(mix_norm), emit_resid=True, tm=tm, tf=FFN_TF)

    w_t = w_in.T
    pf = _in_proj(hn, w_t, 0, o_q, F32, tm=tmm, tn=MM_TN, name="in_proj_rnn")
    pb = _in_proj(hn, w_t, o_q, o_ik - o_q, BF16, tm=tmm, tn=MM_TN, name="in_proj_attn",
                  lane_blocks=True)
    ps = _in_proj(hn, w_t, o_ik, LANES, F32, tm=tmm, tn=LANES, name="in_proj_idx")
    pg = _in_proj(hn, w_t, o_gr, 2 * d, F32, tm=tmm, tn=MM_TN, name="in_proj_gates")

    wax = bf(0.5 * jnp.concatenate([rg_w_a, rg_w_x], axis=-1))
    y_rnn = _rglru(pf.reshape(bsz, s, -1), 0, width, conv_w, vec(conv_b), wax,
                   vec(0.5 * rg_b_a), vec(0.5 * rg_b_x), vec(rg_lambda),
                   width=width, ts=min(RGLRU_TS, s), tc=RGLRU_TC)

    y_attn = _attention(pb, ps.reshape(bsz, s, LANES), btab, bsz=bsz, heads=q_width // HEAD_DIM,
                        tq=KEY_BLOCK, class_blocks=ATTN_CLASS_BLOCKS)

    merged = _merge(y_rnn.reshape(m, width), y_attn,
                    bf(w_proj_rnn), bf(w_proj_attn), pg, 0, d, tm=tm, tn=MM_TN)
    x2 = _matmul_residual(merged, bf(w_out), x1, tm=tmm, tn=MM_TN)

    outs = _ffn(x2, vec(ffn2_norm), bf(ffn2_w_gate), bf(ffn2_w_up), bf(ffn2_w_down),
                vec(next_norm), emit_resid=not last, tm=tm, tf=FFN_TF)
    return outs[0].reshape(bsz, s, d)


def kernel(x, ffn1_norm, ffn1_w_gate, ffn1_w_up, ffn1_w_down, mix_norm, w_in, conv_w, conv_b,
           rg_w_a, rg_b_a, rg_w_x, rg_b_x, rg_lambda, rel_bias, w_proj_rnn, w_proj_attn, w_out,
           ffn2_norm, ffn2_w_gate, ffn2_w_up, ffn2_w_down, final_norm):
    depth = ffn1_norm.shape[0]
    btab = _bias_tiles(rel_bias)
    for l in range(depth):
        last = l == depth - 1
        x = _layer(x, ffn1_norm[l], ffn1_w_gate[l], ffn1_w_up[l], ffn1_w_down[l], mix_norm[l],
                   w_in[l], conv_w[l], conv_b[l], rg_w_a[l], rg_b_a[l], rg_w_x[l], rg_b_x[l],
                   rg_lambda[l], btab, w_proj_rnn[l], w_proj_attn[l], w_out[l], ffn2_norm[l],
                   ffn2_w_gate[l], ffn2_w_up[l], ffn2_w_down[l],
                   final_norm if last else ffn1_norm[l + 1], last=last)
    return x
```

```python
import functools
import math

import jax
import jax.numpy as jnp
from jax import lax
from jax.experimental import pallas as pl
from jax.experimental.pallas import tpu as pltpu

F32 = jnp.float32
BF16 = jnp.bfloat16
I32 = jnp.int32

RMS_EPS = 1e-6
CONV_WIDTH = 4
RG_C = 8.0
RNN_BLOCK = 128
HEAD_DIM = 128
KV_HEADS = 4
IDX_HEADS = 16
IDX_DIM = 64
TOPK_MAX = 256
NUM_BUCKETS = 32
MAX_DISTANCE = 128

LANES = 128
SUBLANES = 8
VMEM_LIMIT_BYTES = 56 * 1024 * 1024

KEY_BLOCK = 128
NEG_LOGIT = -1e30
INT32_MIN = -(2 ** 31)
KEY_NEG_INF = -2139095041
LOG2E = 1.4426950408889634


def _params(semantics):
    return pltpu.CompilerParams(dimension_semantics=semantics,
                                vmem_limit_bytes=VMEM_LIMIT_BYTES)


def _sigmoid(x):
    return 0.5 * (jnp.tanh(0.5 * x) + 1.0)


def _rms(x, g):
    ms = jnp.mean(x * x, axis=-1, keepdims=True)
    return x * lax.rsqrt(ms + RMS_EPS) * g


def _tree_sum(parts):
    while len(parts) > 1:
        parts = [a + b for a, b in zip(parts[::2], parts[1::2])] + (
            [parts[-1]] if len(parts) % 2 else [])
    return parts[0]


def _ffn_kernel(x_ref, g_ref, wg_ref, wu_ref, wd_ref, wgt_ref, wut_ref, wdt_ref, gn_ref, *rest,
                emit_resid, nfull):
    if emit_resid:
        acc_ref, hn_ref, xn_sc = rest
    else:
        acc_ref, xn_sc = rest
        hn_ref = acc_ref
    j = pl.program_id(1)

    @pl.when(j == 0)
    def _():
        xn_sc[...] = _rms(x_ref[...], g_ref[...]).astype(BF16)
        acc_ref[...] = jnp.zeros_like(acc_ref)

    def accumulate(wg, wu, wd):
        xn = xn_sc[...]
        h = jnp.dot(xn, wg, preferred_element_type=F32)
        u = jnp.dot(xn, wu, preferred_element_type=F32)
        a = (h * _sigmoid(h) * u).astype(BF16)
        acc_ref[...] += jnp.dot(a, wd, preferred_element_type=F32)

    @pl.when(j < nfull)
    def _():
        accumulate(wg_ref[...], wu_ref[...], wd_ref[...])

    @pl.when(j == nfull)
    def _():
        accumulate(wgt_ref[...], wut_ref[...], wdt_ref[...])
        y = x_ref[...] + 0.5 * acc_ref[...]
        if emit_resid:
            acc_ref[...] = y
        hn_ref[...] = _rms(y, gn_ref[...]).astype(hn_ref.dtype)


def _ffn(x, g, wg, wu, wd, gn, *, emit_resid, tm, tf):
    m, d = x.shape
    f = wg.shape[1]
    nfull = (f - 1) // tf
    tail = f - nfull * tf
    assert nfull >= 1 and tail % LANES == 0
    wgt, wut, wdt = wg[:, nfull * tf:], wu[:, nfull * tf:], wd[nfull * tf:, :]
    last = nfull - 1
    fixed = pl.Buffered(1)
    row = pl.BlockSpec((tm, d), lambda i, j: (i, 0))
    vec = pl.BlockSpec((1, d), lambda i, j: (0, 0))
    out_shape = [jax.ShapeDtypeStruct((m, d), F32)]
    out_specs = [row]
    if emit_resid:
        out_shape.append(jax.ShapeDtypeStruct((m, d), BF16))
        out_specs.append(row)
    return pl.pallas_call(
        functools.partial(_ffn_kernel, emit_resid=emit_resid, nfull=nfull),
        grid=(m // tm, nfull + 1),
        in_specs=[row, vec,
                  pl.BlockSpec((d, tf), lambda i, j: (0, jnp.minimum(j, last))),
                  pl.BlockSpec((d, tf), lambda i, j: (0, jnp.minimum(j, last))),
                  pl.BlockSpec((tf, d), lambda i, j: (jnp.minimum(j, last), 0)),
                  pl.BlockSpec((d, tail), lambda i, j: (0, 0), pipeline_mode=fixed),
                  pl.BlockSpec((d, tail), lambda i, j: (0, 0), pipeline_mode=fixed),
                  pl.BlockSpec((tail, d), lambda i, j: (0, 0), pipeline_mode=fixed),
                  vec],
        out_specs=out_specs,
        out_shape=out_shape,
        scratch_shapes=[pltpu.VMEM((tm, d), BF16)],
        compiler_params=_params(("parallel", "arbitrary")),
        name="ffn_resid" if emit_resid else "ffn_final",
    )(x, g, wg, wu, wd, wgt, wut, wdt, gn)


def _in_proj_kernel(x_ref, wt_ref, o_ref, wb_sc, *, lane_blocks):
    @pl.when(pl.program_id(1) == 0)
    def _():
        wb_sc[...] = wt_ref[...].T.astype(BF16)

    res = jnp.dot(x_ref[...], wb_sc[...], preferred_element_type=F32)
    if lane_blocks:
        for c in range(res.shape[1] // LANES):
            o_ref[c] = res[:, c * LANES:(c + 1) * LANES].astype(o_ref.dtype)
    else:
        o_ref[...] = res.astype(o_ref.dtype)


def _in_proj(x, wt, row0, n, out_dtype, *, tm, tn, name, lane_blocks=False):
    m, k = x.shape
    assert row0 % SUBLANES == 0 and n % tn == 0
    if lane_blocks:
        out_shape = jax.ShapeDtypeStruct((n // LANES, m, LANES), out_dtype)
        out_spec = pl.BlockSpec((tn // LANES, tm, LANES), lambda j, i: (j, i, 0))
    else:
        out_shape = jax.ShapeDtypeStruct((m, n), out_dtype)
        out_spec = pl.BlockSpec((tm, tn), lambda j, i: (i, j))
    return pl.pallas_call(
        functools.partial(_in_proj_kernel, lane_blocks=lane_blocks),
        grid=(n // tn, m // tm),
        in_specs=[pl.BlockSpec((tm, k), lambda j, i: (i, 0)),
                  pl.BlockSpec((pl.Element(tn), pl.Element(k)),
                               lambda j, i: (pl.multiple_of(row0 + j * tn, SUBLANES), 0))],
        out_specs=out_spec,
        out_shape=out_shape,
        scratch_shapes=[pltpu.VMEM((k, tn), BF16)],
        compiler_params=_params(("parallel", "arbitrary")),
        name=name,
    )(x, wt)


def _mm_res_kernel(x_ref, w_ref, r_ref, o_ref):
    o_ref[...] = r_ref[...] + jnp.dot(x_ref[...], w_ref[...], preferred_element_type=F32)


def _matmul_residual(x, w, r, *, tm, tn):
    m, k = x.shape
    n = w.shape[1]
    return pl.pallas_call(
        _mm_res_kernel,
        grid=(n // tn, m // tm),
        in_specs=[pl.BlockSpec((tm, k), lambda j, i: (i, 0)),
                  pl.BlockSpec((k, tn), lambda j, i: (0, j)),
                  pl.BlockSpec((tm, tn), lambda j, i: (i, j))],
        out_specs=pl.BlockSpec((tm, tn), lambda j, i: (i, j)),
        out_shape=jax.ShapeDtypeStruct((m, n), F32),
        compiler_params=_params(("parallel", "parallel")),
        name="out_proj",
    )(x, w, r)


def _merge_kernel(yr_ref, ya_ref, wr_ref, wa_ref, gr_ref, ga_ref, o_ref):
    pr = jnp.dot(yr_ref[...], wr_ref[...], preferred_element_type=F32)
    pa = jnp.dot(ya_ref[...], wa_ref[...], preferred_element_type=F32)
    o_ref[...] = (_sigmoid(gr_ref[...]) * pr + _sigmoid(ga_ref[...]) * pa).astype(o_ref.dtype)


def _merge(y_rnn, y_attn, w_r, w_a, pf, gr_col, ga_col, *, tm, tn):
    m, k = y_rnn.shape
    n = w_r.shape[1]
    act = pl.BlockSpec((tm, k), lambda j, i: (i, 0))
    wsp = pl.BlockSpec((k, tn), lambda j, i: (0, j),
                       pipeline_mode=pl.Buffered(1 if n == tn else 2))
    return pl.pallas_call(
        _merge_kernel,
        grid=(n // tn, m // tm),
        in_specs=[act, act, wsp, wsp,
                  pl.BlockSpec((tm, tn), lambda j, i: (i, gr_col // tn + j)),
                  pl.BlockSpec((tm, tn), lambda j, i: (i, ga_col // tn + j))],
        out_specs=pl.BlockSpec((tm, tn), lambda j, i: (i, j)),
        out_shape=jax.ShapeDtypeStruct((m, n), BF16),
        compiler_params=_params(("parallel", "parallel")),
        name="merge",
    )(y_rnn, y_attn, w_r, w_a, pf, pf)


def _rglru_kernel(rx_ref, rg_ref, cw_ref, cb_ref, wax_ref, ba_ref, bx_ref, lam_ref,
                  y_ref, xs_sc, h_sc):
    ts, tc = rx_ref.shape
    ng = ts // SUBLANES

    @pl.when(pl.program_id(2) == 0)
    def _():
        xs_sc[...] = jnp.zeros_like(xs_sc)
        h_sc[...] = jnp.zeros_like(h_sc)

    ri = lax.broadcasted_iota(I32, (1, SUBLANES, tc), 1)
    x = rx_ref[...]
    x3 = x.reshape(ng, SUBLANES, tc)
    tail = xs_sc[...]
    xs_sc[...] = x[ts - SUBLANES:ts, :]
    cw = cw_ref[...]
    xc3 = cb_ref[...][None] + x3 * cw[CONV_WIDTH - 1:CONV_WIDTH, :][None]
    for d in range(1, CONV_WIDTH):
        rot = pltpu.roll(x3, d, axis=1)
        rot_tail = pltpu.roll(tail, d, axis=0)[None]
        rot_prev = jnp.concatenate([rot_tail, rot[:ng - 1]], axis=0)
        k = CONV_WIDTH - 1 - d
        xc3 = xc3 + jnp.where(ri >= d, rot, rot_prev) * cw[k:k + 1, :][None]
    xc = xc3.reshape(ts, tc)

    xcb = xc.astype(BF16)
    r_parts, i_parts = [], []
    for n in range(tc // RNN_BLOCK):
        g = jnp.dot(xcb[:, n * RNN_BLOCK:(n + 1) * RNN_BLOCK], wax_ref[n],
                    preferred_element_type=F32)
        r_parts.append(g[:, :RNN_BLOCK])
        i_parts.append(g[:, RNN_BLOCK:])
    tr = jnp.tanh(jnp.concatenate(r_parts, axis=1) + ba_ref[...])
    ti = jnp.tanh(jnp.concatenate(i_parts, axis=1) + bx_ref[...])

    nl = -lam_ref[...]
    softplus = jnp.maximum(nl, 0.0) + jnp.log1p(jnp.exp(-jnp.abs(nl)))
    nla = (tr + 1.0) * ((0.5 * RG_C) * softplus)
    a = jnp.exp2(nla * (-LOG2E))
    m2 = jnp.tanh(nla) * (a * a + 1.0)
    mult = jnp.where(m2 > 0.0, m2 * lax.rsqrt(m2), 0.0)
    u = mult * ((0.5 * ti + 0.5) * xc)

    av = a.reshape(ng, SUBLANES, tc)
    bv = u.reshape(ng, SUBLANES, tc)
    for d in (1, 2, 4):
        a_prev = jnp.where(ri >= d, pltpu.roll(av, d, axis=1), 1.0)
        b_prev = jnp.where(ri >= d, pltpu.roll(bv, d, axis=1), 0.0)
        bv = av * b_prev + bv
        av = av * a_prev
    carry = h_sc[0:1, :]
    hs = []
    for k in range(ng):
        hk = bv[k] + av[k] * carry
        hs.append(hk)
        carry = hk[SUBLANES - 1:SUBLANES, :]
    h_sc[0:1, :] = carry
    h = jnp.concatenate(hs, axis=0)

    gt = rg_ref[...]
    c1 = math.sqrt(2.0 / math.pi)
    inner = gt * (c1 + (c1 * 0.044715) * (gt * gt))
    y_ref[...] = ((h * (0.5 * gt)) * (1.0 + jnp.tanh(inner))).astype(y_ref.dtype)


def _rglru(pf, rx_col, rg_col, conv_w, conv_b, wax, b_a, b_x, lam, *, width, ts, tc):
    bsz, s, _ = pf.shape
    chan = lambda rows: pl.BlockSpec((rows, tc), lambda b, c, t: (0, c))
    return pl.pallas_call(
        _rglru_kernel,
        grid=(bsz, width // tc, s // ts),
        in_specs=[pl.BlockSpec((None, ts, tc), lambda b, c, t: (b, t, rx_col // tc + c)),
                  pl.BlockSpec((None, ts, tc), lambda b, c, t: (b, t, rg_col // tc + c)),
                  chan(CONV_WIDTH), chan(1),
                  pl.BlockSpec((tc // RNN_BLOCK, RNN_BLOCK, 2 * RNN_BLOCK),
                               lambda b, c, t: (c, 0, 0)),
                  chan(1), chan(1), chan(1)],
        out_specs=pl.BlockSpec((None, ts, tc), lambda b, c, t: (b, t, c)),
        out_shape=jax.ShapeDtypeStruct((bsz, s, width), BF16),
        scratch_shapes=[pltpu.VMEM((SUBLANES, tc), F32),
                        pltpu.VMEM((SUBLANES, tc), F32)],
        compiler_params=_params(("parallel", "parallel", "arbitrary")),
        name="rglru",
    )(pf, pf, conv_w, conv_b, wax, b_a, b_x, lam)


def _bias_kernel(rb_ref, o_ref):
    h = pl.program_id(0)
    far = rb_ref[NUM_BUCKETS - 1, h]
    ii = lax.broadcasted_iota(I32, (KEY_BLOCK, KEY_BLOCK), 0)
    jj = lax.broadcasted_iota(I32, (KEY_BLOCK, KEY_BLOCK), 1)
    max_exact = NUM_BUCKETS // 2
    for d in range(2):
        n = jnp.maximum(ii - jj + KEY_BLOCK * d, 0)
        nf = jnp.maximum(n, 1).astype(F32)
        large = max_exact + (jnp.log(nf / max_exact) / math.log(MAX_DISTANCE / max_exact)
                             * (NUM_BUCKETS - max_exact)).astype(I32)
        large = jnp.minimum(large, NUM_BUCKETS - 1)
        bucket = jnp.where(n < max_exact, n, large)
        acc = jnp.zeros((KEY_BLOCK, KEY_BLOCK), F32)
        for b in range(NUM_BUCKETS):
            acc = jnp.where(bucket == b, rb_ref[b, h], acc)
        o_ref[0, d] = (acc - far) * LOG2E
    o_ref[0, 2] = jnp.zeros((KEY_BLOCK, KEY_BLOCK), F32)


def _bias_tiles(rel_bias):
    heads = rel_bias.shape[1]
    return pl.pallas_call(
        _bias_kernel,
        grid=(heads,),
        in_specs=[pl.BlockSpec(memory_space=pltpu.SMEM)],
        out_specs=pl.BlockSpec((1, 3, KEY_BLOCK, KEY_BLOCK), lambda h: (h, 0, 0, 0)),
        out_shape=jax.ShapeDtypeStruct((heads, 3, KEY_BLOCK, KEY_BLOCK), F32),
        compiler_params=_params(("parallel",)),
        name="bias_tiles",
    )(rel_bias)


SCORE_CHUNK = 4


def _attn_body(q_ref, k_ref, v_ref, iq_ref, ik_ref, iw_ref, bt_ref, y_ref,
               keys_sc, keyst_sc, am_sc, wb_sc, pb_sc, va_sc, *, k_top, nkb, class_blocks):
    heads, tq, _ = q_ref.shape
    groups = heads // KV_HEADS
    nk = nkb * KEY_BLOCK
    qi = pl.program_id(1)
    row = qi * tq + lax.broadcasted_iota(I32, (tq, KEY_BLOCK), 0)
    lane = lax.broadcasted_iota(I32, (tq, KEY_BLOCK), 1)

    w_scale = (IDX_HEADS ** -0.5) * (IDX_DIM ** -0.5)
    iw = iw_ref[...]
    for h in range(IDX_HEADS):
        col = iw[:, IDX_DIM + h:IDX_DIM + h + 1] * w_scale
        wb_sc[h] = jnp.broadcast_to(col, (tq, KEY_BLOCK))
    per_slab = LANES // IDX_DIM
    iq_rows = jnp.concatenate(
        [iq_ref[h // per_slab][:, (h % per_slab) * IDX_DIM:(h % per_slab + 1) * IDX_DIM]
         for h in range(IDX_HEADS)], axis=0)

    for c0 in range(0, nkb, SCORE_CHUNK):
        nb = min(SCORE_CHUNK, nkb - c0)
        ikc = ik_ref[c0 * KEY_BLOCK:(c0 + nb) * KEY_BLOCK, 0:IDX_DIM].astype(BF16)
        dots = lax.dot_general(iq_rows, ikc, (((1,), (1,)), ((), ())),
                               preferred_element_type=F32)
        for sub in range(nb):
            kb = c0 + sub
            acc = jnp.zeros((tq, KEY_BLOCK), F32)
            for h in range(IDX_HEADS):
                d = dots[h * tq:(h + 1) * tq, sub * KEY_BLOCK:(sub + 1) * KEY_BLOCK]
                acc = acc + jnp.maximum(d, 0.0) * wb_sc[h]
            score = jnp.where(kb * KEY_BLOCK + lane <= row, acc, -jnp.inf)
            bits = pltpu.bitcast(score, I32)
            key = bits ^ ((bits >> 31) & 0x7FFFFFFF)
            keys_sc[kb] = key
            keyst_sc[kb] = key.T.reshape(KEY_BLOCK // SUBLANES, SUBLANES, tq)

    def count_where(pred, ref8, blocks=nkb):
        parts = [jnp.sum(pred(keyst_sc[kb], ref8).astype(I32), axis=0) for kb in range(blocks)]
        return jnp.sum(_tree_sum(parts), axis=0, keepdims=True)

    def search(blocks):
        def bit_step(it, carry):
            thr, cnt = carry
            cand = thr + lax.shift_left(jnp.int32(1), 31 - it)
            total = count_where(jnp.greater_equal,
                                jnp.broadcast_to(cand, (SUBLANES, tq))[None], blocks)
            take = total >= k_top
            return jnp.where(take, cand, thr), jnp.where(take, total, cnt)

        return lax.fori_loop(0, 32, bit_step, (jnp.full((1, tq), INT32_MIN, I32),
                                               jnp.full((1, tq), blocks * KEY_BLOCK, I32)))

    short = nkb - class_blocks // 2
    if 0 < short < nkb:
        thr, cnt = lax.cond(qi < short, lambda: search(short), lambda: search(nkb))
    else:
        thr, cnt = search(nkb)
    full_row = thr > KEY_NEG_INF
    thr = jnp.maximum(thr, KEY_NEG_INF + 1)
    has_ties = jnp.max(jnp.where(full_row, cnt, 0)) > k_top

    thr_b = jnp.broadcast_to(thr, (KEY_BLOCK, tq)).T
    for kb in range(nkb):
        am_sc[kb] = jnp.where(keys_sc[kb] >= thr_b, 0.0, NEG_LOGIT)

    @pl.when(has_ties)
    def _():
        thr_t = jnp.broadcast_to(thr, (KEY_BLOCK, tq))
        n_eq = count_where(jnp.equal, jnp.broadcast_to(thr, (SUBLANES, tq))[None])
        keep = jnp.where(full_row, k_top - (cnt - n_eq), nk).astype(F32)
        keep_t = jnp.broadcast_to(keep, (KEY_BLOCK, tq))
        ri = lax.broadcasted_iota(I32, (KEY_BLOCK, KEY_BLOCK), 0)
        ci = lax.broadcasted_iota(I32, (KEY_BLOCK, KEY_BLOCK), 1)
        lower = jnp.where(ri >= ci, 1.0, 0.0).astype(BF16)
        seen = jnp.zeros((1, tq), F32)
        for kb in range(nkb):
            kt = keyst_sc[kb].reshape(KEY_BLOCK, tq)
            eq = jnp.where(kt == thr_t, 1.0, 0.0)
            rank = jnp.dot(lower, eq.astype(BF16), preferred_element_type=F32) + seen
            seen = rank[KEY_BLOCK - 1:KEY_BLOCK, :]
            tie_ok = jnp.where(rank <= keep_t, eq, 0.0)
            sel = jnp.where(kt > thr_t, 1.0, tie_ok)
            am_sc[kb] = jnp.where(sel.T > 0.5, 0.0, NEG_LOGIT)

    first_near = max(nkb - class_blocks - 1, 0)

    def logit(s, g, j, kb):
        sj = (s[j * tq:(j + 1) * tq, kb * KEY_BLOCK:(kb + 1) * KEY_BLOCK] + am_sc[kb])
        if kb >= first_near:
            sj = sj + bt_ref[g * groups + j, jnp.clip(qi - kb, 0, 2)]
        return sj

    scale2 = (HEAD_DIM ** -0.5) * LOG2E

    def raw_logits(g):
        qg = q_ref[g * groups:(g + 1) * groups].reshape(groups * tq, HEAD_DIM)
        return lax.dot_general(qg, k_ref[g, 0:nk, :], (((1,), (1,)), ((), ())),
                               preferred_element_type=F32) * scale2

    s_next = raw_logits(0)
    for g in range(KV_HEADS):
        s = s_next
        if g + 1 < KV_HEADS:
            s_next = raw_logits(g + 1)
        for j in range(groups):
            m_run = logit(s, g, j, 0)
            for kb in range(1, nkb):
                m_run = jnp.maximum(m_run, logit(s, g, j, kb))
            m_b = jnp.broadcast_to(jnp.max(m_run, axis=1, keepdims=True), (tq, KEY_BLOCK))
            for kb in range(nkb):
                pb_sc[g, j * tq:(j + 1) * tq, kb * KEY_BLOCK:(kb + 1) * KEY_BLOCK] = (
                    jnp.exp2(logit(s, g, j, kb) - m_b).astype(BF16))
        o = jnp.dot(pb_sc[g, :, 0:nk], va_sc[g, 0:nk, :], preferred_element_type=F32)
        out = o[:, 0:HEAD_DIM] / o[:, HEAD_DIM:2 * HEAD_DIM]
        for j in range(groups):
            hd = (g * groups + j) * HEAD_DIM
            y_ref[:, hd:hd + HEAD_DIM] = out[j * tq:(j + 1) * tq].astype(y_ref.dtype)


def _attn_kernel(*refs, k_top, class_blocks, nq):
    qi = pl.program_id(1)
    v_ref, va_sc = refs[2], refs[-1]

    @pl.when(qi == 0)
    def _():
        va_sc[:, :, 0:HEAD_DIM] = v_ref[...]
        va_sc[:, :, HEAD_DIM:2 * HEAD_DIM] = jnp.ones(v_ref.shape, v_ref.dtype)

    for c in range(-(-nq // class_blocks)):
        nkb = min((c + 1) * class_blocks, nq)

        @pl.when(qi // class_blocks == c)
        def _(nkb=nkb):
            _attn_body(*refs, k_top=k_top, nkb=nkb, class_blocks=class_blocks)


def _attention(pb, ps, btab, *, bsz, heads, tq, class_blocks):
    s = ps.shape[1]
    nq = s // tq
    iq_slabs = IDX_HEADS * IDX_DIM // LANES
    assert HEAD_DIM == LANES and heads % KV_HEADS == 0 and heads % iq_slabs == 0
    assert pb.shape[0] == heads + 2 * KV_HEADS + iq_slabs
    nkb_max = s // KEY_BLOCK
    groups = heads // KV_HEADS
    k_top = min(TOPK_MAX, s // 4)
    assert tq == KEY_BLOCK and nq % class_blocks == 0
    return pl.pallas_call(
        functools.partial(_attn_kernel, k_top=k_top, class_blocks=class_blocks, nq=nq),
        grid=(bsz, nq),
        in_specs=[pl.BlockSpec((heads, tq, LANES), lambda b, i: (0, b * nq + i, 0)),
                  pl.BlockSpec((KV_HEADS, s, LANES), lambda b, i: (heads // KV_HEADS, b, 0)),
                  pl.BlockSpec((KV_HEADS, s, LANES), lambda b, i: (heads // KV_HEADS + 1, b, 0)),
                  pl.BlockSpec((iq_slabs, tq, LANES),
                               lambda b, i: ((heads + 2 * KV_HEADS) // iq_slabs, b * nq + i, 0)),
                  pl.BlockSpec((None, s, LANES), lambda b, i: (b, 0, 0)),
                  pl.BlockSpec((None, tq, LANES), lambda b, i: (b, i, 0)),
                  pl.BlockSpec(btab.shape, lambda b, i: (0, 0, 0, 0))],
        out_specs=pl.BlockSpec((tq, heads * HEAD_DIM), lambda b, i: (b * nq + i, 0)),
        out_shape=jax.ShapeDtypeStruct((bsz * s, heads * HEAD_DIM), BF16),
        scratch_shapes=[
            pltpu.VMEM((nkb_max, tq, KEY_BLOCK), I32),
            pltpu.VMEM((nkb_max, KEY_BLOCK // SUBLANES, SUBLANES, tq), I32),
            pltpu.VMEM((nkb_max, tq, KEY_BLOCK), F32),
            pltpu.VMEM((IDX_HEADS, tq, KEY_BLOCK), F32),
            pltpu.VMEM((KV_HEADS, groups * tq, s), BF16),
            pltpu.VMEM((KV_HEADS, s, 2 * HEAD_DIM), BF16)],
        compiler_params=_params(("parallel", "arbitrary")),
        name="sparse_attn",
    )(pb, pb, pb, pb, ps, ps, btab)


FFN_TM, FFN_TF = 512, 512
MM_TM, MM_TN = 1024, 1024
MERGE_TN = 2048
RGLRU_TS, RGLRU_TC = 512, 512
ATTN_CLASS_BLOCKS = 4


def _layer(x, ffn1_norm, ffn1_w_gate, ffn1_w_up, ffn1_w_down, mix_norm, w_in, conv_w, conv_b,
           rg_w_a, rg_b_a, rg_w_x, rg_b_x, rg_lambda, btab, w_proj_rnn, w_proj_attn, w_out,
           ffn2_norm, ffn2_w_gate, ffn2_w_up, ffn2_w_down, next_norm, *, last):
    bsz, s, d = x.shape
    m = bsz * s
    width = d
    q_width = d
    kv_width = KV_HEADS * HEAD_DIM
    iq_width = IDX_HEADS * IDX_DIM
    tm = min(FFN_TM, m)
    tmm = min(MM_TM, m)

    o_q = 2 * width
    o_ik = o_q + q_width + 2 * kv_width + iq_width
    o_gr = o_ik + IDX_DIM + IDX_HEADS

    bf = lambda w: w.astype(BF16)
    vec = lambda p: p.reshape(1, -1)

    x1, hn = _ffn(x.reshape(m, d), vec(ffn1_norm), bf(ffn1_w_gate), bf(ffn1_w_up), bf(ffn1_w_down),
                  vec(mix_norm), emit_resid=True, tm=tm, tf=FFN_TF)

    w_t = w_in.T
    pf = _in_proj(hn, w_t, 0, o_q, F32, tm=tmm, tn=MM_TN, name="in_proj_rnn")
    pb = _in_proj(hn, w_t, o_q, o_ik - o_q, BF16, tm=tmm, tn=MM_TN, name="in_proj_attn",
                  lane_blocks=True)
    ps = _in_proj(hn, w_t, o_ik, LANES, F32, tm=tmm, tn=LANES, name="in_proj_idx")
    pg = _in_proj(hn, w_t, o_gr, 2 * d, F32, tm=tmm, tn=MM_TN, name="in_proj_gates")

    wax = bf(0.5 * jnp.concatenate([rg_w_a, rg_w_x], axis=-1))
    y_rnn = _rglru(pf.reshape(bsz, s, -1), 0, width, conv_w, vec(conv_b), wax,
                   vec(0.5 * rg_b_a), vec(0.5 * rg_b_x), vec(rg_lambda),
                   width=width, ts=min(RGLRU_TS, s), tc=RGLRU_TC)

    y_attn = _attention(pb, ps.reshape(bsz, s, LANES), btab, bsz=bsz, heads=q_width // HEAD_DIM,
                        tq=KEY_BLOCK, class_blocks=ATTN_CLASS_BLOCKS)

    merged = _merge(y_rnn.reshape(m, width), y_attn,
                    bf(w_proj_rnn), bf(w_proj_attn), pg, 0, d, tm=tm, tn=MERGE_TN)
    x2 = _matmul_residual(merged, bf(w_out), x1, tm=tmm, tn=MM_TN)

    outs = _ffn(x2, vec(ffn2_norm), bf(ffn2_w_gate), bf(ffn2_w_up), bf(ffn2_w_down),
                vec(next_norm), emit_resid=not last, tm=tm, tf=FFN_TF)
    return outs[0].reshape(bsz, s, d)


def kernel(x, ffn1_norm, ffn1_w_gate, ffn1_w_up, ffn1_w_down, mix_norm, w_in, conv_w, conv_b,
           rg_w_a, rg_b_a, rg_w_x, rg_b_x, rg_lambda, rel_bias, w_proj_rnn, w_proj_attn, w_out,
           ffn2_norm, ffn2_w_gate, ffn2_w_up, ffn2_w_down, final_norm):
    depth = ffn1_norm.shape[0]
    btab = _bias_tiles(rel_bias)
    for l in range(depth):
        last = l == depth - 1
        x = _layer(x, ffn1_norm[l], ffn1_w_gate[l], ffn1_w_up[l], ffn1_w_down[l], mix_norm[l],
                   w_in[l], conv_w[l], conv_b[l], rg_w_a[l], rg_b_a[l], rg_w_x[l], rg_b_x[l],
                   rg_lambda[l], btab, w_proj_rnn[l], w_proj_attn[l], w_out[l], ffn2_norm[l],
                   ffn2_w_gate[l], ffn2_w_up[l], ffn2_w_down[l],
                   final_norm if last else ffn1_norm[l + 1], last=last)
    return x
```

```python
import functools
import math

import jax
import jax.numpy as jnp
from jax import lax
from jax.experimental import pallas as pl
from jax.experimental.pallas import tpu as pltpu

F32 = jnp.float32
BF16 = jnp.bfloat16
I32 = jnp.int32

RMS_EPS = 1e-6
CONV_WIDTH = 4
RG_C = 8.0
RNN_BLOCK = 128
HEAD_DIM = 128
KV_HEADS = 4
IDX_HEADS = 16
IDX_DIM = 64
TOPK_MAX = 256
NUM_BUCKETS = 32
MAX_DISTANCE = 128

LANES = 128
SUBLANES = 8
VMEM_LIMIT_BYTES = 56 * 1024 * 1024

KEY_BLOCK = 128
NEG_LOGIT = -1e30
INT32_MIN = -(2 ** 31)
KEY_NEG_INF = -2139095041
LOG2E = 1.4426950408889634


def _params(semantics):
    return pltpu.CompilerParams(dimension_semantics=semantics,
                                vmem_limit_bytes=VMEM_LIMIT_BYTES)


def _sigmoid(x):
    return 0.5 * (jnp.tanh(0.5 * x) + 1.0)


def _rms(x, g):
    ms = jnp.mean(x * x, axis=-1, keepdims=True)
    return x * lax.rsqrt(ms + RMS_EPS) * g


def _tree_sum(parts):
    while len(parts) > 1:
        parts = [a + b for a, b in zip(parts[::2], parts[1::2])] + (
            [parts[-1]] if len(parts) % 2 else [])
    return parts[0]


FFN_RING = 3


def _ffn_kernel(x_ref, g_ref, wg_hbm, wu_hbm, wd_hbm, wgt_ref, wut_ref, wdt_ref, gn_ref, *rest,
                emit_resid, nfull, ntiles):
    if emit_resid:
        acc_ref, hn_ref, xn_sc, wg_buf, wu_buf, wd_buf, sem = rest
    else:
        acc_ref, xn_sc, wg_buf, wu_buf, wd_buf, sem = rest
        hn_ref = acc_ref
    i = pl.program_id(0)
    j = pl.program_id(1)
    tf = wg_buf.shape[2]
    total = ntiles * nfull
    ahead = FFN_RING - 1

    def tile_copies(seq):
        slot = seq % FFN_RING
        col = pl.multiple_of((seq % nfull) * tf, tf)
        return (pltpu.make_async_copy(wg_hbm.at[:, pl.ds(col, tf)], wg_buf.at[slot], sem.at[0, slot]),
                pltpu.make_async_copy(wu_hbm.at[:, pl.ds(col, tf)], wu_buf.at[slot], sem.at[1, slot]),
                pltpu.make_async_copy(wd_hbm.at[pl.ds(col, tf), :], wd_buf.at[slot], sem.at[2, slot]))

    @pl.when((i == 0) & (j == 0))
    def _():
        for seq in range(min(ahead, total)):
            for c in tile_copies(seq):
                c.start()

    @pl.when(j == 0)
    def _():
        xn_sc[...] = _rms(x_ref[...], g_ref[...]).astype(BF16)
        acc_ref[...] = jnp.zeros_like(acc_ref)

    def accumulate(wg, wu, wd):
        xn = xn_sc[...]
        h = jnp.dot(xn, wg, preferred_element_type=F32)
        u = jnp.dot(xn, wu, preferred_element_type=F32)
        a = (h * _sigmoid(h) * u).astype(BF16)
        acc_ref[...] += jnp.dot(a, wd, preferred_element_type=F32)

    @pl.when(j < nfull)
    def _():
        seq = i * nfull + j
        for c in tile_copies(seq):
            c.wait()

        @pl.when(seq + ahead < total)
        def _():
            for c in tile_copies(seq + ahead):
                c.start()

        slot = seq % FFN_RING
        accumulate(wg_buf[slot], wu_buf[slot], wd_buf[slot])

    @pl.when(j == nfull)
    def _():
        accumulate(wgt_ref[...], wut_ref[...], wdt_ref[...])
        y = x_ref[...] + 0.5 * acc_ref[...]
        if emit_resid:
            acc_ref[...] = y
        hn_ref[...] = _rms(y, gn_ref[...]).astype(hn_ref.dtype)


def _ffn(x, g, wg, wu, wd, gn, *, emit_resid, tm, tf):
    m, d = x.shape
    f = wg.shape[1]
    nfull = (f - 1) // tf
    tail = f - nfull * tf
    assert nfull >= 1 and tail % LANES == 0
    wgt, wut, wdt = wg[:, nfull * tf:], wu[:, nfull * tf:], wd[nfull * tf:, :]
    fixed = pl.Buffered(1)
    hbm = pl.BlockSpec(memory_space=pl.ANY)
    row = pl.BlockSpec((tm, d), lambda i, j: (i, 0))
    vec = pl.BlockSpec((1, d), lambda i, j: (0, 0))
    out_shape = [jax.ShapeDtypeStruct((m, d), F32)]
    out_specs = [row]
    if emit_resid:
        out_shape.append(jax.ShapeDtypeStruct((m, d), BF16))
        out_specs.append(row)
    return pl.pallas_call(
        functools.partial(_ffn_kernel, emit_resid=emit_resid, nfull=nfull, ntiles=m // tm),
        grid=(m // tm, nfull + 1),
        in_specs=[row, vec, hbm, hbm, hbm,
                  pl.BlockSpec((d, tail), lambda i, j: (0, 0), pipeline_mode=fixed),
                  pl.BlockSpec((d, tail), lambda i, j: (0, 0), pipeline_mode=fixed),
                  pl.BlockSpec((tail, d), lambda i, j: (0, 0), pipeline_mode=fixed),
                  vec],
        out_specs=out_specs,
        out_shape=out_shape,
        scratch_shapes=[pltpu.VMEM((tm, d), BF16),
                        pltpu.VMEM((FFN_RING, d, tf), BF16),
                        pltpu.VMEM((FFN_RING, d, tf), BF16),
                        pltpu.VMEM((FFN_RING, tf, d), BF16),
                        pltpu.SemaphoreType.DMA((3, FFN_RING))],
        compiler_params=_params(("arbitrary", "arbitrary")),
        name="ffn_resid" if emit_resid else "ffn_final",
    )(x, g, wg, wu, wd, wgt, wut, wdt, gn)


def _in_proj_kernel(x_ref, wt_ref, o_ref, wb_sc, *, lane_blocks):
    @pl.when(pl.program_id(1) == 0)
    def _():
        wb_sc[...] = wt_ref[...].T.astype(BF16)

    res = jnp.dot(x_ref[...], wb_sc[...], preferred_element_type=F32)
    if lane_blocks:
        for c in range(res.shape[1] // LANES):
            o_ref[c] = res[:, c * LANES:(c + 1) * LANES].astype(o_ref.dtype)
    else:
        o_ref[...] = res.astype(o_ref.dtype)


def _in_proj(x, wt, row0, n, out_dtype, *, tm, tn, name, lane_blocks=False):
    m, k = x.shape
    assert row0 % SUBLANES == 0 and n % tn == 0
    if lane_blocks:
        out_shape = jax.ShapeDtypeStruct((n // LANES, m, LANES), out_dtype)
        out_spec = pl.BlockSpec((tn // LANES, tm, LANES), lambda j, i: (j, i, 0))
    else:
        out_shape = jax.ShapeDtypeStruct((m, n), out_dtype)
        out_spec = pl.BlockSpec((tm, tn), lambda j, i: (i, j))
    return pl.pallas_call(
        functools.partial(_in_proj_kernel, lane_blocks=lane_blocks),
        grid=(n // tn, m // tm),
        in_specs=[pl.BlockSpec((tm, k), lambda j, i: (i, 0)),
                  pl.BlockSpec((pl.Element(tn), pl.Element(k)),
                               lambda j, i: (pl.multiple_of(row0 + j * tn, SUBLANES), 0))],
        out_specs=out_spec,
        out_shape=out_shape,
        scratch_shapes=[pltpu.VMEM((k, tn), BF16)],
        compiler_params=_params(("parallel", "arbitrary")),
        name=name,
    )(x, wt)


def _mm_res_kernel(x_ref, w_ref, r_ref, o_ref):
    o_ref[...] = r_ref[...] + jnp.dot(x_ref[...], w_ref[...], preferred_element_type=F32)


def _matmul_residual(x, w, r, *, tm, tn):
    m, k = x.shape
    n = w.shape[1]
    return pl.pallas_call(
        _mm_res_kernel,
        grid=(n // tn, m // tm),
        in_specs=[pl.BlockSpec((tm, k), lambda j, i: (i, 0)),
                  pl.BlockSpec((k, tn), lambda j, i: (0, j)),
                  pl.BlockSpec((tm, tn), lambda j, i: (i, j))],
        out_specs=pl.BlockSpec((tm, tn), lambda j, i: (i, j)),
        out_shape=jax.ShapeDtypeStruct((m, n), F32),
        compiler_params=_params(("parallel", "parallel")),
        name="out_proj",
    )(x, w, r)


def _merge_kernel(yr_ref, ya_ref, wr_ref, wa_ref, gr_ref, ga_ref, o_ref):
    pr = jnp.dot(yr_ref[...], wr_ref[...], preferred_element_type=F32)
    pa = jnp.dot(ya_ref[...], wa_ref[...], preferred_element_type=F32)
    o_ref[...] = (_sigmoid(gr_ref[...]) * pr + _sigmoid(ga_ref[...]) * pa).astype(o_ref.dtype)


def _merge(y_rnn, y_attn, w_r, w_a, pf, gr_col, ga_col, *, tm, tn):
    m, k = y_rnn.shape
    n = w_r.shape[1]
    act = pl.BlockSpec((tm, k), lambda j, i: (i, 0))
    wsp = pl.BlockSpec((k, tn), lambda j, i: (0, j),
                       pipeline_mode=pl.Buffered(1 if n == tn else 2))
    return pl.pallas_call(
        _merge_kernel,
        grid=(n // tn, m // tm),
        in_specs=[act, act, wsp, wsp,
                  pl.BlockSpec((tm, tn), lambda j, i: (i, gr_col // tn + j)),
                  pl.BlockSpec((tm, tn), lambda j, i: (i, ga_col // tn + j))],
        out_specs=pl.BlockSpec((tm, tn), lambda j, i: (i, j)),
        out_shape=jax.ShapeDtypeStruct((m, n), BF16),
        compiler_params=_params(("parallel", "parallel")),
        name="merge",
    )(y_rnn, y_attn, w_r, w_a, pf, pf)


def _rglru_kernel(rx_ref, rg_ref, cw_ref, cb_ref, wax_ref, ba_ref, bx_ref, lam_ref,
                  y_ref, xs_sc, h_sc):
    ts, tc = rx_ref.shape
    ng = ts // SUBLANES

    @pl.when(pl.program_id(2) == 0)
    def _():
        xs_sc[...] = jnp.zeros_like(xs_sc)
        h_sc[...] = jnp.zeros_like(h_sc)

    ri = lax.broadcasted_iota(I32, (1, SUBLANES, tc), 1)
    x = rx_ref[...]
    x3 = x.reshape(ng, SUBLANES, tc)
    tail = xs_sc[...]
    xs_sc[...] = x[ts - SUBLANES:ts, :]
    cw = cw_ref[...]
    xc3 = cb_ref[...][None] + x3 * cw[CONV_WIDTH - 1:CONV_WIDTH, :][None]
    for d in range(1, CONV_WIDTH):
        rot = pltpu.roll(x3, d, axis=1)
        rot_tail = pltpu.roll(tail, d, axis=0)[None]
        rot_prev = jnp.concatenate([rot_tail, rot[:ng - 1]], axis=0)
        k = CONV_WIDTH - 1 - d
        xc3 = xc3 + jnp.where(ri >= d, rot, rot_prev) * cw[k:k + 1, :][None]
    xc = xc3.reshape(ts, tc)

    xcb = xc.astype(BF16)
    r_parts, i_parts = [], []
    for n in range(tc // RNN_BLOCK):
        g = jnp.dot(xcb[:, n * RNN_BLOCK:(n + 1) * RNN_BLOCK], wax_ref[n],
                    preferred_element_type=F32)
        r_parts.append(g[:, :RNN_BLOCK])
        i_parts.append(g[:, RNN_BLOCK:])
    tr = jnp.tanh(jnp.concatenate(r_parts, axis=1) + ba_ref[...])
    ti = jnp.tanh(jnp.concatenate(i_parts, axis=1) + bx_ref[...])

    nl = -lam_ref[...]
    softplus = jnp.maximum(nl, 0.0) + jnp.log1p(jnp.exp(-jnp.abs(nl)))
    nla = (tr + 1.0) * ((0.5 * RG_C) * softplus)
    a = jnp.exp2(nla * (-LOG2E))
    m2 = jnp.tanh(nla) * (a * a + 1.0)
    mult = jnp.where(m2 > 0.0, m2 * lax.rsqrt(m2), 0.0)
    u = mult * ((0.5 * ti + 0.5) * xc)

    av = a.reshape(ng, SUBLANES, tc)
    bv = u.reshape(ng, SUBLANES, tc)
    for d in (1, 2, 4):
        a_prev = jnp.where(ri >= d, pltpu.roll(av, d, axis=1), 1.0)
        b_prev = jnp.where(ri >= d, pltpu.roll(bv, d, axis=1), 0.0)
        bv = av * b_prev + bv
        av = av * a_prev
    carry = h_sc[0:1, :]
    hs = []
    for k in range(ng):
        hk = bv[k] + av[k] * carry
        hs.append(hk)
        carry = hk[SUBLANES - 1:SUBLANES, :]
    h_sc[0:1, :] = carry
    h = jnp.concatenate(hs, axis=0)

    gt = rg_ref[...]
    c1 = math.sqrt(2.0 / math.pi)
    inner = gt * (c1 + (c1 * 0.044715) * (gt * gt))
    y_ref[...] = ((h * (0.5 * gt)) * (1.0 + jnp.tanh(inner))).astype(y_ref.dtype)


def _rglru(pf, rx_col, rg_col, conv_w, conv_b, wax, b_a, b_x, lam, *, width, ts, tc):
    bsz, s, _ = pf.shape
    chan = lambda rows: pl.BlockSpec((rows, tc), lambda b, c, t: (0, c))
    return pl.pallas_call(
        _rglru_kernel,
        grid=(bsz, width // tc, s // ts),
        in_specs=[pl.BlockSpec((None, ts, tc), lambda b, c, t: (b, t, rx_col // tc + c)),
                  pl.BlockSpec((None, ts, tc), lambda b, c, t: (b, t, rg_col // tc + c)),
                  chan(CONV_WIDTH), chan(1),
                  pl.BlockSpec((tc // RNN_BLOCK, RNN_BLOCK, 2 * RNN_BLOCK),
                               lambda b, c, t: (c, 0, 0)),
                  chan(1), chan(1), chan(1)],
        out_specs=pl.BlockSpec((None, ts, tc), lambda b, c, t: (b, t, c)),
        out_shape=jax.ShapeDtypeStruct((bsz, s, width), BF16),
        scratch_shapes=[pltpu.VMEM((SUBLANES, tc), F32),
                        pltpu.VMEM((SUBLANES, tc), F32)],
        compiler_params=_params(("parallel", "parallel", "arbitrary")),
        name="rglru",
    )(pf, pf, conv_w, conv_b, wax, b_a, b_x, lam)


def _bias_kernel(rb_ref, o_ref):
    h = pl.program_id(0)
    far = rb_ref[NUM_BUCKETS - 1, h]
    ii = lax.broadcasted_iota(I32, (KEY_BLOCK, KEY_BLOCK), 0)
    jj = lax.broadcasted_iota(I32, (KEY_BLOCK, KEY_BLOCK), 1)
    max_exact = NUM_BUCKETS // 2
    for d in range(2):
        n = jnp.maximum(ii - jj + KEY_BLOCK * d, 0)
        nf = jnp.maximum(n, 1).astype(F32)
        large = max_exact + (jnp.log(nf / max_exact) / math.log(MAX_DISTANCE / max_exact)
                             * (NUM_BUCKETS - max_exact)).astype(I32)
        large = jnp.minimum(large, NUM_BUCKETS - 1)
        bucket = jnp.where(n < max_exact, n, large)
        acc = jnp.zeros((KEY_BLOCK, KEY_BLOCK), F32)
        for b in range(NUM_BUCKETS):
            acc = jnp.where(bucket == b, rb_ref[b, h], acc)
        o_ref[0, d] = (acc - far) * LOG2E
    o_ref[0, 2] = jnp.zeros((KEY_BLOCK, KEY_BLOCK), F32)


def _bias_tiles(rel_bias):
    heads = rel_bias.shape[1]
    return pl.pallas_call(
        _bias_kernel,
        grid=(heads,),
        in_specs=[pl.BlockSpec(memory_space=pltpu.SMEM)],
        out_specs=pl.BlockSpec((1, 3, KEY_BLOCK, KEY_BLOCK), lambda h: (h, 0, 0, 0)),
        out_shape=jax.ShapeDtypeStruct((heads, 3, KEY_BLOCK, KEY_BLOCK), F32),
        compiler_params=_params(("parallel",)),
        name="bias_tiles",
    )(rel_bias)


SCORE_CHUNK = 4


def _attn_body(q_ref, k_ref, v_ref, iq_ref, ik_ref, iw_ref, bt_ref, y_ref,
               keys_sc, keyst_sc, am_sc, wb_sc, pb_sc, va_sc, *, k_top, nkb, class_blocks):
    heads, tq, _ = q_ref.shape
    groups = heads // KV_HEADS
    nk = nkb * KEY_BLOCK
    qi = pl.program_id(1)
    row = qi * tq + lax.broadcasted_iota(I32, (tq, KEY_BLOCK), 0)
    lane = lax.broadcasted_iota(I32, (tq, KEY_BLOCK), 1)

    w_scale = (IDX_HEADS ** -0.5) * (IDX_DIM ** -0.5)
    iw = iw_ref[...]
    for h in range(IDX_HEADS):
        col = iw[:, IDX_DIM + h:IDX_DIM + h + 1] * w_scale
        wb_sc[h] = jnp.broadcast_to(col, (tq, KEY_BLOCK))
    per_slab = LANES // IDX_DIM
    iq_rows = jnp.concatenate(
        [iq_ref[h // per_slab][:, (h % per_slab) * IDX_DIM:(h % per_slab + 1) * IDX_DIM]
         for h in range(IDX_HEADS)], axis=0)

    for c0 in range(0, nkb, SCORE_CHUNK):
        nb = min(SCORE_CHUNK, nkb - c0)
        ikc = ik_ref[c0 * KEY_BLOCK:(c0 + nb) * KEY_BLOCK, 0:IDX_DIM].astype(BF16)
        dots = lax.dot_general(iq_rows, ikc, (((1,), (1,)), ((), ())),
                               preferred_element_type=F32)
        for sub in range(nb):
            kb = c0 + sub
            acc = jnp.zeros((tq, KEY_BLOCK), F32)
            for h in range(IDX_HEADS):
                d = dots[h * tq:(h + 1) * tq, sub * KEY_BLOCK:(sub + 1) * KEY_BLOCK]
                acc = acc + jnp.maximum(d, 0.0) * wb_sc[h]
            score = jnp.where(kb * KEY_BLOCK + lane <= row, acc, -jnp.inf)
            bits = pltpu.bitcast(score, I32)
            key = bits ^ ((bits >> 31) & 0x7FFFFFFF)
            keys_sc[kb] = key
            keyst_sc[kb] = key.T.reshape(KEY_BLOCK // SUBLANES, SUBLANES, tq)

    def count_where(pred, ref8, blocks=nkb):
        parts = [jnp.sum(pred(keyst_sc[kb], ref8).astype(I32), axis=0) for kb in range(blocks)]
        return jnp.sum(_tree_sum(parts), axis=0, keepdims=True)

    def search(blocks):
        def bit_step(it, carry):
            thr, cnt = carry
            cand = thr + lax.shift_left(jnp.int32(1), 31 - it)
            total = count_where(jnp.greater_equal,
                                jnp.broadcast_to(cand, (SUBLANES, tq))[None], blocks)
            take = total >= k_top
            return jnp.where(take, cand, thr), jnp.where(take, total, cnt)

        return lax.fori_loop(0, 32, bit_step, (jnp.full((1, tq), INT32_MIN, I32),
                                               jnp.full((1, tq), blocks * KEY_BLOCK, I32)))

    short = nkb - class_blocks // 2
    if 0 < short < nkb:
        thr, cnt = lax.cond(qi < short, lambda: search(short), lambda: search(nkb))
    else:
        thr, cnt = search(nkb)
    full_row = thr > KEY_NEG_INF
    thr = jnp.maximum(thr, KEY_NEG_INF + 1)
    has_ties = jnp.max(jnp.where(full_row, cnt, 0)) > k_top

    thr_b = jnp.broadcast_to(thr, (KEY_BLOCK, tq)).T
    for kb in range(nkb):
        am_sc[kb] = jnp.where(keys_sc[kb] >= thr_b, 0.0, NEG_LOGIT)

    @pl.when(has_ties)
    def _():
        thr_t = jnp.broadcast_to(thr, (KEY_BLOCK, tq))
        n_eq = count_where(jnp.equal, jnp.broadcast_to(thr, (SUBLANES, tq))[None])
        keep = jnp.where(full_row, k_top - (cnt - n_eq), nk).astype(F32)
        keep_t = jnp.broadcast_to(keep, (KEY_BLOCK, tq))
        ri = lax.broadcasted_iota(I32, (KEY_BLOCK, KEY_BLOCK), 0)
        ci = lax.broadcasted_iota(I32, (KEY_BLOCK, KEY_BLOCK), 1)
        lower = jnp.where(ri >= ci, 1.0, 0.0).astype(BF16)
        seen = jnp.zeros((1, tq), F32)
        for kb in range(nkb):
            kt = keyst_sc[kb].reshape(KEY_BLOCK, tq)
            eq = jnp.where(kt == thr_t, 1.0, 0.0)
            rank = jnp.dot(lower, eq.astype(BF16), preferred_element_type=F32) + seen
            seen = rank[KEY_BLOCK - 1:KEY_BLOCK, :]
            tie_ok = jnp.where(rank <= keep_t, eq, 0.0)
            sel = jnp.where(kt > thr_t, 1.0, tie_ok)
            am_sc[kb] = jnp.where(sel.T > 0.5, 0.0, NEG_LOGIT)

    first_near = max(nkb - class_blocks - 1, 0)

    def logit(s, g, j, kb):
        sj = (s[j * tq:(j + 1) * tq, kb * KEY_BLOCK:(kb + 1) * KEY_BLOCK] + am_sc[kb])
        if kb >= first_near:
            sj = sj + bt_ref[g * groups + j, jnp.clip(qi - kb, 0, 2)]
        return sj

    scale2 = (HEAD_DIM ** -0.5) * LOG2E

    def raw_logits(g):
        qg = q_ref[g * groups:(g + 1) * groups].reshape(groups * tq, HEAD_DIM)
        return lax.dot_general(qg, k_ref[g, 0:nk, :], (((1,), (1,)), ((), ())),
                               preferred_element_type=F32) * scale2

    s_next = raw_logits(0)
    for g in range(KV_HEADS):
        s = s_next
        if g + 1 < KV_HEADS:
            s_next = raw_logits(g + 1)
        for j in range(groups):
            m_run = logit(s, g, j, 0)
            for kb in range(1, nkb):
                m_run = jnp.maximum(m_run, logit(s, g, j, kb))
            m_b = jnp.broadcast_to(jnp.max(m_run, axis=1, keepdims=True), (tq, KEY_BLOCK))
            for kb in range(nkb):
                pb_sc[g, j * tq:(j + 1) * tq, kb * KEY_BLOCK:(kb + 1) * KEY_BLOCK] = (
                    jnp.exp2(logit(s, g, j, kb) - m_b).astype(BF16))
        o = jnp.dot(pb_sc[g, :, 0:nk], va_sc[g, 0:nk, :], preferred_element_type=F32)
        out = o[:, 0:HEAD_DIM] / o[:, HEAD_DIM:2 * HEAD_DIM]
        for j in range(groups):
            hd = (g * groups + j) * HEAD_DIM
            y_ref[:, hd:hd + HEAD_DIM] = out[j * tq:(j + 1) * tq].astype(y_ref.dtype)


def _attn_kernel(*refs, k_top, class_blocks, nq):
    qi = pl.program_id(1)
    v_ref, va_sc = refs[2], refs[-1]

    @pl.when(qi == 0)
    def _():
        va_sc[:, :, 0:HEAD_DIM] = v_ref[...]
        va_sc[:, :, HEAD_DIM:2 * HEAD_DIM] = jnp.ones(v_ref.shape, v_ref.dtype)

    for c in range(-(-nq // class_blocks)):
        nkb = min((c + 1) * class_blocks, nq)

        @pl.when(qi // class_blocks == c)
        def _(nkb=nkb):
            _attn_body(*refs, k_top=k_top, nkb=nkb, class_blocks=class_blocks)


def _attention(pb, ps, btab, *, bsz, heads, tq, class_blocks):
    s = ps.shape[1]
    nq = s // tq
    iq_slabs = IDX_HEADS * IDX_DIM // LANES
    assert HEAD_DIM == LANES and heads % KV_HEADS == 0 and heads % iq_slabs == 0
    assert pb.shape[0] == heads + 2 * KV_HEADS + iq_slabs
    nkb_max = s // KEY_BLOCK
    groups = heads // KV_HEADS
    k_top = min(TOPK_MAX, s // 4)
    assert tq == KEY_BLOCK and nq % class_blocks == 0
    return pl.pallas_call(
        functools.partial(_attn_kernel, k_top=k_top, class_blocks=class_blocks, nq=nq),
        grid=(bsz, nq),
        in_specs=[pl.BlockSpec((heads, tq, LANES), lambda b, i: (0, b * nq + i, 0)),
                  pl.BlockSpec((KV_HEADS, s, LANES), lambda b, i: (heads // KV_HEADS, b, 0)),
                  pl.BlockSpec((KV_HEADS, s, LANES), lambda b, i: (heads // KV_HEADS + 1, b, 0)),
                  pl.BlockSpec((iq_slabs, tq, LANES),
                               lambda b, i: ((heads + 2 * KV_HEADS) // iq_slabs, b * nq + i, 0)),
                  pl.BlockSpec((None, s, LANES), lambda b, i: (b, 0, 0)),
                  pl.BlockSpec((None, tq, LANES), lambda b, i: (b, i, 0)),
                  pl.BlockSpec(btab.shape, lambda b, i: (0, 0, 0, 0))],
        out_specs=pl.BlockSpec((tq, heads * HEAD_DIM), lambda b, i: (b * nq + i, 0)),
        out_shape=jax.ShapeDtypeStruct((bsz * s, heads * HEAD_DIM), BF16),
        scratch_shapes=[
            pltpu.VMEM((nkb_max, tq, KEY_BLOCK), I32),
            pltpu.VMEM((nkb_max, KEY_BLOCK // SUBLANES, SUBLANES, tq), I32),
            pltpu.VMEM((nkb_max, tq, KEY_BLOCK), F32),
            pltpu.VMEM((IDX_HEADS, tq, KEY_BLOCK), F32),
            pltpu.VMEM((KV_HEADS, groups * tq, s), BF16),
            pltpu.VMEM((KV_HEADS, s, 2 * HEAD_DIM), BF16)],
        compiler_params=_params(("parallel", "arbitrary")),
        name="sparse_attn",
    )(pb, pb, pb, pb, ps, ps, btab)


FFN_TM, FFN_TF = 512, 512
MM_TM, MM_TN = 1024, 1024
MERGE_TN = 2048
RGLRU_TS, RGLRU_TC = 512, 512
ATTN_CLASS_BLOCKS = 4


def _layer(x, ffn1_norm, ffn1_w_gate, ffn1_w_up, ffn1_w_down, mix_norm, w_in, conv_w, conv_b,
           rg_w_a, rg_b_a, rg_w_x, rg_b_x, rg_lambda, btab, w_proj_rnn, w_proj_attn, w_out,
           ffn2_norm, ffn2_w_gate, ffn2_w_up, ffn2_w_down, next_norm, *, last):
    bsz, s, d = x.shape
    m = bsz * s
    width = d
    q_width = d
    kv_width = KV_HEADS * HEAD_DIM
    iq_width = IDX_HEADS * IDX_DIM
    tm = min(FFN_TM, m)
    tmm = min(MM_TM, m)

    o_q = 2 * width
    o_ik = o_q + q_width + 2 * kv_width + iq_width
    o_gr = o_ik + IDX_DIM + IDX_HEADS

    bf = lambda w: w.astype(BF16)
    vec = lambda p: p.reshape(1, -1)

    x1, hn = _ffn(x.reshape(m, d), vec(ffn1_norm), bf(ffn1_w_gate), bf(ffn1_w_up), bf(ffn1_w_down),
                  vec(mix_norm), emit_resid=True, tm=tm, tf=FFN_TF)

    w_t = w_in.T
    pf = _in_proj(hn, w_t, 0, o_q, F32, tm=tmm, tn=MM_TN, name="in_proj_rnn")
    pb = _in_proj(hn, w_t, o_q, o_ik - o_q, BF16, tm=tmm, tn=MM_TN, name="in_proj_attn",
                  lane_blocks=True)
    ps = _in_proj(hn, w_t, o_ik, LANES, F32, tm=tmm, tn=LANES, name="in_proj_idx")
    pg = _in_proj(hn, w_t, o_gr, 2 * d, F32, tm=tmm, tn=MM_TN, name="in_proj_gates")

    wax = bf(0.5 * jnp.concatenate([rg_w_a, rg_w_x], axis=-1))
    y_rnn = _rglru(pf.reshape(bsz, s, -1), 0, width, conv_w, vec(conv_b), wax,
                   vec(0.5 * rg_b_a), vec(0.5 * rg_b_x), vec(rg_lambda),
                   width=width, ts=min(RGLRU_TS, s), tc=RGLRU_TC)

    y_attn = _attention(pb, ps.reshape(bsz, s, LANES), btab, bsz=bsz, heads=q_width // HEAD_DIM,
                        tq=KEY_BLOCK, class_blocks=ATTN_CLASS_BLOCKS)

    merged = _merge(y_rnn.reshape(m, width), y_attn,
                    bf(w_proj_rnn), bf(w_proj_attn), pg, 0, d, tm=tm, tn=MERGE_TN)
    x2 = _matmul_residual(merged, bf(w_out), x1, tm=tmm, tn=MM_TN)

    outs = _ffn(x2, vec(ffn2_norm), bf(ffn2_w_gate), bf(ffn2_w_up), bf(ffn2_w_down),
                vec(next_norm), emit_resid=not last, tm=tm, tf=FFN_TF)
    return outs[0].reshape(bsz, s, d)


def kernel(x, ffn1_norm, ffn1_w_gate, ffn1_w_up, ffn1_w_down, mix_norm, w_in, conv_w, conv_b,
           rg_w_a, rg_b_a, rg_w_x, rg_b_x, rg_lambda, rel_bias, w_proj_rnn, w_proj_attn, w_out,
           ffn2_norm, ffn2_w_gate, ffn2_w_up, ffn2_w_down, final_norm):
    depth = ffn1_norm.shape[0]
    btab = _bias_tiles(rel_bias)
    for l in range(depth):
        last = l == depth - 1
        x = _layer(x, ffn1_norm[l], ffn1_w_gate[l], ffn1_w_up[l], ffn1_w_down[l], mix_norm[l],
                   w_in[l], conv_w[l], conv_b[l], rg_w_a[l], rg_b_a[l], rg_w_x[l], rg_b_x[l],
                   rg_lambda[l], btab, w_proj_rnn[l], w_proj_attn[l], w_out[l], ffn2_norm[l],
                   ffn2_w_gate[l], ffn2_w_up[l], ffn2_w_down[l],
                   final_norm if last else ffn1_norm[l + 1], last=last)
    return x
```
